```python
import jax, jax.numpy as jnp
from jax import lax
import numpy as np

D_MODEL = 1024
BATCH = 4
SEQ = 8192
DEPTH = 4
DEC_BATCH = 8
DEC_SEQ = 16
PAST_LEN = 2048

CHUNK = 64
Q_BLOCK = 128
N_BRANCH = 4
BRANCH_W = 256
HA = 4
DKA = 64
DVA = 64
HB = 4
DHB = 64
W_LORA = 32
A_LORA = 32
G_LORA = 64
HC = 4
NOPE = 64
ROPE = 32
VH = 64
Q_LORA = 192
KV_LORA = 128
ROPE_THETA = 10000.0
HD = 4
DHD = 64
N_EXPERTS = 16
N_GROUPS = 4
EXPERTS_PER_GROUP = N_EXPERTS // N_GROUPS
TOP_K = 2
EXPERT_FF = 256
PLE_DIM = 256
A_COLS = 2 * HA * DKA + 2 * HA * DVA
B_COLS = 3 * HB * DHB + W_LORA + A_LORA + G_LORA
C_COLS = Q_LORA + KV_LORA + ROPE
D_COLS = 3 * HD * DHD + HD
A_OFF = 0
B_OFF = A_OFF + A_COLS
C_OFF = B_OFF + B_COLS
D_OFF = C_OFF + C_COLS
N_IN = D_OFF + D_COLS

ALPHA = (2 * DEPTH) ** 0.25
BETA = (8 * DEPTH) ** -0.25
LN_EPS = 1e-5
RMS_EPS = 1e-6
RWKV_GN_EPS = 64e-5
NEG_INF = -1e30
F32 = jnp.float32

kernel_name = 'hybrid_streaming_encoder_step'


def layer_norm(x, g, b):
    xf = x.astype(F32)
    mu = jnp.mean(xf, -1, keepdims=True)
    var = jnp.mean(jnp.square(xf - mu), -1, keepdims=True)
    return ((xf - mu) * lax.rsqrt(var + LN_EPS) * g.astype(F32) + b.astype(F32)).astype(x.dtype)


def rms_norm(x, g):
    xf = x.astype(F32)
    return (xf * lax.rsqrt(jnp.mean(xf * xf, -1, keepdims=True) + RMS_EPS) * g.astype(F32)).astype(x.dtype)


def rotary(x, pos):
    half = ROPE // 2
    inv = 1.0 / (ROPE_THETA ** (jnp.arange(half, dtype=F32) / half))
    ang = pos.astype(F32)[:, None] * inv[None, :]
    shape = (pos.shape[0],) + (1,) * (x.ndim - 3) + (half,)
    cos, sin = jnp.cos(ang).reshape(shape), jnp.sin(ang).reshape(shape)
    xf = x.astype(F32)
    x1, x2 = xf[..., :half], xf[..., half:]
    return jnp.concatenate([x1 * cos - x2 * sin, x1 * sin + x2 * cos], -1).astype(x.dtype)


def over_query_blocks(attend, parts, n_q):
    if n_q > Q_BLOCK and n_q % Q_BLOCK == 0:
        nb = n_q // Q_BLOCK
        split = lambda a: jnp.moveaxis(a.reshape(a.shape[0], nb, Q_BLOCK, *a.shape[2:]), 1, 0)
        blocks = tuple(split(a) for a in parts)
        qidx = jnp.arange(n_q, dtype=jnp.int32).reshape(nb, Q_BLOCK)
        out = lax.map(lambda args: attend(args[0], args[1]), (blocks, qidx))
        out = jnp.moveaxis(out, 0, 1)
        return out.reshape(out.shape[0], n_q, *out.shape[3:])
    return attend(parts, jnp.arange(n_q, dtype=jnp.int32))


def hgrn_lower_bounds(lb_logits):
    pr = jax.nn.softmax(lb_logits.astype(F32), axis=0)
    return jnp.cumsum(pr, axis=0) - pr


def hgrn2_recurrence(q, logf, k, v, s0):
    B, L = q.shape[:2]
    c = CHUNK if L % CHUNK == 0 else L
    n = L // c
    to_chunks = lambda a: jnp.moveaxis(a.reshape(B, n, c, *a.shape[2:]), 1, 0)
    causal = jnp.tril(jnp.ones((c, c), dtype=bool))

    def step(S, inp):
        qc, lfc, kc, vc = inp
        a = jnp.cumsum(lfc.astype(F32), axis=1)
        diff = a[:, :, None] - a[:, None, :]
        decay = jnp.exp(jnp.where(causal[None, :, :, None, None], diff, -jnp.inf))
        scores = jnp.einsum('bthk,btshk,bshk->bhts', qc, decay, kc)
        o = jnp.einsum('bhts,bshv->bthv', scores, vc) + jnp.einsum('bthk,bhkv->bthv', qc * jnp.exp(a), S)
        a_last = a[:, -1]
        S = jnp.exp(a_last)[..., None] * S + jnp.einsum('bshk,bshv->bhkv', kc * jnp.exp(a_last[:, None] - a), vc)
        return S, o

    S, o = lax.scan(step, s0.astype(F32), tuple(to_chunks(t) for t in (q, logf, k, v)))
    o = jnp.moveaxis(o, 0, 1)
    return o.reshape(B, L, *o.shape[3:]), S


def hgrn2_branch(cols, lb, norm_g, s0):
    B, L, _ = cols.shape
    hk, hv = HA * DKA, HA * DVA
    q = cols[..., :hk]
    fz = cols[..., hk:2 * hk]
    i_in = cols[..., 2 * hk:2 * hk + hv]
    g = cols[..., 2 * hk + hv:]
    f = lb.astype(F32) + (1.0 - lb.astype(F32)) * jax.nn.sigmoid(fz.astype(F32))
    logf = jnp.log(f)
    k = 1.0 - f
    heads = lambda t, d: t.reshape(B, L, HA, d)
    o, s = hgrn2_recurrence(heads(q, DKA), heads(logf, DKA), heads(k, DKA), heads(i_in, DVA), s0)
    o = rms_norm(o, norm_g.reshape(HA, DVA)) * jax.nn.silu(g.astype(F32)).reshape(B, L, HA, DVA)
    return o.reshape(B, L, hv).astype(cols.dtype), s


def rwkv7_recurrence(r, w, k, v, a, b, s0):
    xs = tuple(jnp.moveaxis(t, 1, 0) for t in (r, w, k, v, a, b))

    def step(S, inp):
        rt, wt, kt, vt, at, bt = inp
        sa = jnp.einsum('bhvk,bhk->bhv', S, at)
        S = S * wt[:, :, None, :] + sa[..., None] * bt[:, :, None, :] + vt[..., None] * kt[:, :, None, :]
        return S, jnp.einsum('bhvk,bhk->bhv', S, rt)

    S, y = lax.scan(step, s0.astype(F32), xs)
    return jnp.moveaxis(y, 0, 1), S


def rwkv7_branch(cols, prev_row, s0, mu, w0, w2, a0, a2, g2, kk_s, ka, rk, lnx_g, lnx_b):
    B, L, _ = cols.shape
    C = HB * DHB
    shifted = jnp.concatenate([prev_row[:, None, :].astype(cols.dtype), cols[:, :-1]], axis=1)
    m = cols + (shifted - cols) * mu
    r, k, v = m[..., :C], m[..., C:2 * C], m[..., 2 * C:3 * C]
    o1 = 3 * C
    wl = m[..., o1:o1 + W_LORA]
    al = m[..., o1 + W_LORA:o1 + W_LORA + A_LORA]
    gl = m[..., o1 + W_LORA + A_LORA:]
    w_log = -jax.nn.softplus(-(w0 + jnp.tanh(wl) @ w2).astype(F32)) - 0.5
    decay = jnp.exp(-jnp.exp(w_log))
    a_rate = jax.nn.sigmoid((a0 + al @ a2).astype(F32))
    g = jax.nn.sigmoid(gl) @ g2
    heads = lambda t: t.reshape(B, L, HB, DHB)
    kk = heads((k * kk_s).astype(F32))
    kk = kk / jnp.maximum(jnp.sqrt(jnp.sum(kk * kk, -1, keepdims=True)), 1e-12)
    ah = heads(a_rate)
    kh = heads(k).astype(F32) * (1.0 + (ah - 1.0) * ka.reshape(HB, DHB).astype(F32))
    rh = heads(r).astype(F32)
    vh = heads(v).astype(F32)
    y, s = rwkv7_recurrence(rh, heads(decay), kh, vh, -kk, kk * ah, s0)
    y_mu = jnp.mean(y, -1, keepdims=True)
    y_var = jnp.mean(jnp.square(y - y_mu), -1, keepdims=True)
    y = (y - y_mu) * lax.rsqrt(y_var + RWKV_GN_EPS) * lnx_g.reshape(HB, DHB).astype(F32) + lnx_b.reshape(HB, DHB).astype(F32)
    y = y + jnp.sum(rh * kh * rk.astype(F32), -1, keepdims=True) * vh
    out = y.reshape(B, L, C) * g.astype(F32)
    return out.astype(cols.dtype), s, cols[:, -1]


def mla_branch(cols, lat_past, kr_past, qn_g, w_uq, kvn_g, w_ukv):
    B, L, _ = cols.shape
    P = lat_past.shape[1]
    q_lat = cols[..., :Q_LORA]
    kv_lat = cols[..., Q_LORA:Q_LORA + KV_LORA]
    kr = cols[..., Q_LORA + KV_LORA:]
    pos = P + jnp.arange(L, dtype=jnp.int32)
    q = (rms_norm(q_lat, qn_g) @ w_uq).reshape(B, L, HC, NOPE + ROPE)
    q_nope, q_rope = q[..., :NOPE], rotary(q[..., NOPE:], pos)
    c_new = rms_norm(kv_lat, kvn_g)
    kr_new = rotary(kr, pos)
    c_all = jnp.concatenate([lat_past.astype(c_new.dtype), c_new], axis=1)
    kr_all = jnp.concatenate([kr_past.astype(kr_new.dtype), kr_new], axis=1)
    kv = (c_all @ w_ukv).reshape(B, P + L, HC, NOPE + VH)
    k_nope, v = kv[..., :NOPE], kv[..., NOPE:]
    kchunk = jnp.arange(P + L, dtype=jnp.int32) // CHUNK
    scale = (NOPE + ROPE) ** -0.5

    def attend(parts, qidx):
        qn, qr = parts
        s = (jnp.einsum('bqhd,bkhd->bhqk', qn, k_nope) + jnp.einsum('bqhd,bkd->bhqk', qr, kr_all)).astype(F32) * scale
        mask = kchunk[None, :] <= ((P + qidx) // CHUNK)[:, None]
        pr = jax.nn.softmax(jnp.where(mask, s, NEG_INF), axis=-1).astype(v.dtype)
        return jnp.einsum('bhqk,bkhd->bqhd', pr, v)

    o = over_query_blocks(attend, (q_nope, q_rope), L)
    return o.reshape(B, L, HC * VH).astype(cols.dtype), c_new, kr_new


def fox_branch(cols, k_past, v_past, lf_past, bf):
    B, L, _ = cols.shape
    P = k_past.shape[1]
    C = HD * DHD
    q = cols[..., :C].reshape(B, L, HD, DHD)
    k = cols[..., C:2 * C].reshape(B, L, HD, DHD)
    v = cols[..., 2 * C:3 * C].reshape(B, L, HD, DHD)
    lf = jax.nn.log_sigmoid((cols[..., 3 * C:] + bf).astype(F32))
    k_all = jnp.concatenate([k_past.astype(k.dtype), k], axis=1)
    v_all = jnp.concatenate([v_past.astype(v.dtype), v], axis=1)
    c = jnp.cumsum(jnp.concatenate([lf_past.astype(F32), lf], axis=1), axis=1)
    ck = jnp.moveaxis(c, -1, 1)
    kpos = jnp.arange(P + L, dtype=jnp.int32)
    scale = DHD ** -0.5

    def attend(parts, qidx):
        qb, cq = parts
        s = jnp.einsum('bqhd,bkhd->bhqk', qb, k_all).astype(F32) * scale
        s = s + jnp.moveaxis(cq, -1, 1)[..., :, None] - ck[:, :, None, :]
        mask = kpos[None, :] <= (P + qidx)[:, None]
        pr = jax.nn.softmax(jnp.where(mask, s, NEG_INF), axis=-1).astype(v_all.dtype)
        return jnp.einsum('bhqk,bkhd->bqhd', pr, v_all)

    o = over_query_blocks(attend, (q, c[:, P:]), L)
    return o.reshape(B, L, C).astype(cols.dtype), k, v, lf


def routed_moe(x, router_w, router_b, w_gate, w_up, w_down):
    B, L, D = x.shape
    t = x.reshape(B * L, D)
    scores = jax.nn.sigmoid((t @ router_w).astype(F32))
    biased = (scores + router_b.astype(F32)).reshape(-1, N_GROUPS, EXPERTS_PER_GROUP)
    group_score = jnp.sum(lax.top_k(biased, TOP_K)[0], axis=-1)
    g_sel = jnp.argmax(group_score, axis=-1)
    in_group = jnp.take_along_axis(biased, g_sel[:, None, None], axis=1)[:, 0]
    _, local = lax.top_k(in_group, TOP_K)
    e_sel = g_sel[:, None] * EXPERTS_PER_GROUP + local
    wts = jnp.take_along_axis(scores, e_sel, axis=1)
    wts = wts / jnp.sum(wts, -1, keepdims=True)
    gate = jnp.sum(jax.nn.one_hot(e_sel, N_EXPERTS, dtype=F32) * wts[..., None], axis=1).astype(x.dtype)
    out = jnp.zeros_like(t)
    for e in range(N_EXPERTS):
        h = jax.nn.silu(t @ w_gate[e]) * (t @ w_up[e])
        out = out + gate[:, e:e + 1] * (h @ w_down[e])
    return out.reshape(B, L, D)


def trunk_layer(x, p, hgrn_s, rwkv_s, rwkv_prev, mla_lat_past, mla_kr_past, fox_k_past, fox_v_past, fox_lf_past,
                w_in, lb, hgrn_norm_g, rwkv_mu, rwkv_w0, rwkv_w2, rwkv_a0, rwkv_a2, rwkv_g2, rwkv_kk, rwkv_ka,
                rwkv_rk, rwkv_lnx_g, rwkv_lnx_b, mla_qnorm_g, mla_w_uq, mla_kvnorm_g, mla_w_ukv, fox_bf,
                w_br, w_mg, w_o, ln1_g, ln1_b, router_w, router_b, exp_w_gate, exp_w_up, exp_w_down,
                ln2_g, ln2_b, ple_w, ple_gate_w, ln3_g, ln3_b):
    B, L, D = x.shape
    cols = x @ w_in
    o_a, hgrn_new = hgrn2_branch(cols[..., A_OFF:B_OFF], lb, hgrn_norm_g, hgrn_s)
    o_b, rwkv_new, shift_new = rwkv7_branch(cols[..., B_OFF:C_OFF], rwkv_prev, rwkv_s, rwkv_mu, rwkv_w0, rwkv_w2,
                                            rwkv_a0, rwkv_a2, rwkv_g2, rwkv_kk, rwkv_ka, rwkv_rk, rwkv_lnx_g, rwkv_lnx_b)
    o_c, lat_new, kr_new = mla_branch(cols[..., C_OFF:D_OFF], mla_lat_past, mla_kr_past, mla_qnorm_g, mla_w_uq,
                                      mla_kvnorm_g, mla_w_ukv)
    o_d, fk_new, fv_new, flf_new = fox_branch(cols[..., D_OFF:N_IN], fox_k_past, fox_v_past, fox_lf_past, fox_bf)
    branches = jnp.stack([o_a, o_b, o_c, o_d], axis=2)
    proj = jnp.einsum('blnc,ncd->blnd', branches, w_br)
    gates = jax.nn.sigmoid(x @ w_mg).reshape(B, L, N_BRANCH, D)
    mix = jnp.einsum('blnd,blnd->bld', gates, proj) @ w_o
    x = layer_norm(ALPHA * x + mix, ln1_g, ln1_b)
    x = layer_norm(ALPHA * x + routed_moe(x, router_w, router_b, exp_w_gate, exp_w_up, exp_w_down), ln2_g, ln2_b)
    ple = (p @ ple_w) * jax.nn.sigmoid(x @ ple_gate_w)
    x = layer_norm(ALPHA * x + ple, ln3_g, ln3_b)
    dt = x.dtype
    return x, (hgrn_new.astype(dt), rwkv_new.astype(dt), shift_new.astype(dt), lat_new.astype(dt),
               kr_new.astype(dt), fk_new.astype(dt), fv_new.astype(dt), flf_new.astype(dt))


def setup_inputs(seed: int = 0) -> dict:
    key = jax.random.key(seed)
    ks = iter(jax.random.split(key, 64))
    nrm = lambda shape, scale=1.0: scale * jax.random.normal(next(ks), shape, F32)
    gain = lambda shape: 1.0 + 0.02 * jax.random.normal(next(ks), shape, F32)
    D = D_MODEL
    return {
        'x_prompt': nrm((BATCH, SEQ, D)),
        'x_sample': nrm((DEC_BATCH, DEC_SEQ, D)),
        'state_hgrn': nrm((DEPTH, DEC_BATCH, HA, DKA, DVA), 0.5),
        'state_rwkv': nrm((DEPTH, DEC_BATCH, HB, DHB, DHB), 0.5),
        'state_rwkv_shift': nrm((DEPTH, DEC_BATCH, B_COLS)),
        'cache_mla_latent': nrm((DEPTH, DEC_BATCH, PAST_LEN, KV_LORA)),
        'cache_mla_krope': nrm((DEPTH, DEC_BATCH, PAST_LEN, ROPE)),
        'cache_fox_k': nrm((DEPTH, DEC_BATCH, PAST_LEN, HD, DHD)),
        'cache_fox_v': nrm((DEPTH, DEC_BATCH, PAST_LEN, HD, DHD)),
        'cache_fox_logf': jax.nn.log_sigmoid(3.0 + nrm((DEPTH, DEC_BATCH, PAST_LEN, HD))),
        'p_prompt': nrm((DEPTH, BATCH, SEQ, PLE_DIM)),
        'p_sample': nrm((DEPTH, DEC_BATCH, DEC_SEQ, PLE_DIM)),
        'ln_in_g': gain((D,)),
        'ln_in_b': nrm((D,), 0.02),
        'w_in': nrm((DEPTH, D, N_IN), D ** -0.5),
        'hgrn_lb_logits': nrm((DEPTH, HA * DKA), 0.5),
        'hgrn_norm_g': gain((DEPTH, HA * DVA)),
        'rwkv_mu': jax.random.uniform(next(ks), (DEPTH, B_COLS), F32),
        'rwkv_w0': nrm((DEPTH, HB * DHB), 0.5),
        'rwkv_w2': nrm((DEPTH, W_LORA, HB * DHB), W_LORA ** -0.5),
        'rwkv_a0': nrm((DEPTH, HB * DHB), 0.5),
        'rwkv_a2': nrm((DEPTH, A_LORA, HB * DHB), A_LORA ** -0.5),
        'rwkv_g2': nrm((DEPTH, G_LORA, HB * DHB), G_LORA ** -0.5),
        'rwkv_kk': 0.85 + nrm((DEPTH, HB * DHB), 0.05),
        'rwkv_ka': 1.0 + nrm((DEPTH, HB * DHB), 0.05),
        'rwkv_rk': nrm((DEPTH, HB, DHB), 0.1),
        'rwkv_lnx_g': gain((DEPTH, HB * DHB)),
        'rwkv_lnx_b': nrm((DEPTH, HB * DHB), 0.02),
        'mla_qnorm_g': gain((DEPTH, Q_LORA)),
        'mla_w_uq': nrm((DEPTH, Q_LORA, HC * (NOPE + ROPE)), Q_LORA ** -0.5),
        'mla_kvnorm_g': gain((DEPTH, KV_LORA)),
        'mla_w_ukv': nrm((DEPTH, KV_LORA, HC * (NOPE + VH)), KV_LORA ** -0.5),
        'fox_bf': 3.0 + nrm((DEPTH, HD), 0.1),
        'w_br': nrm((DEPTH, N_BRANCH, BRANCH_W, D), BRANCH_W ** -0.5),
        'w_mg': nrm((DEPTH, D, N_BRANCH * D), D ** -0.5),
        'w_o': nrm((DEPTH, D, D), BETA * D ** -0.5),
        'ln1_g': gain((DEPTH, D)),
        'ln1_b': nrm((DEPTH, D), 0.02),
        'router_w': nrm((D, N_EXPERTS), D ** -0.5),
        'router_b': nrm((N_EXPERTS,), 0.01),
        'exp_w_gate': nrm((DEPTH, N_EXPERTS, D, EXPERT_FF), D ** -0.5),
        'exp_w_up': nrm((DEPTH, N_EXPERTS, D, EXPERT_FF), D ** -0.5),
        'exp_w_down': nrm((DEPTH, N_EXPERTS, EXPERT_FF, D), BETA * EXPERT_FF ** -0.5),
        'ln2_g': gain((DEPTH, D)),
        'ln2_b': nrm((DEPTH, D), 0.02),
        'ple_w': nrm((DEPTH, PLE_DIM, D), BETA * PLE_DIM ** -0.5),
        'ple_gate_w': nrm((DEPTH, D, D), D ** -0.5),
        'ln3_g': gain((DEPTH, D)),
        'ln3_b': nrm((DEPTH, D), 0.02),
    }


def reference(x_prompt, x_sample, state_hgrn, state_rwkv, state_rwkv_shift, cache_mla_latent, cache_mla_krope,
              cache_fox_k, cache_fox_v, cache_fox_logf, p_prompt, p_sample, ln_in_g, ln_in_b, w_in,
              hgrn_lb_logits, hgrn_norm_g, rwkv_mu, rwkv_w0, rwkv_w2, rwkv_a0, rwkv_a2, rwkv_g2, rwkv_kk,
              rwkv_ka, rwkv_rk, rwkv_lnx_g, rwkv_lnx_b, mla_qnorm_g, mla_w_uq, mla_kvnorm_g, mla_w_ukv, fox_bf,
              w_br, w_mg, w_o, ln1_g, ln1_b, router_w, router_b, exp_w_gate, exp_w_up, exp_w_down, ln2_g, ln2_b,
              ple_w, ple_gate_w, ln3_g, ln3_b):
    lb_all = hgrn_lower_bounds(hgrn_lb_logits)
    xp = layer_norm(x_prompt, ln_in_g, ln_in_b)
    xs = layer_norm(x_sample, ln_in_g, ln_in_b)
    bp, dt = x_prompt.shape[0], x_prompt.dtype
    fresh = (jnp.zeros((bp, HA, DKA, DVA), dt), jnp.zeros((bp, HB, DHB, DHB), dt), jnp.zeros((bp, B_COLS), dt),
             jnp.zeros((bp, 0, KV_LORA), dt), jnp.zeros((bp, 0, ROPE), dt), jnp.zeros((bp, 0, HD, DHD), dt),
             jnp.zeros((bp, 0, HD, DHD), dt), jnp.zeros((bp, 0, HD), dt))
    new_p, new_s = [], []
    for i in range(DEPTH):
        lw = (w_in[i], lb_all[i], hgrn_norm_g[i], rwkv_mu[i], rwkv_w0[i], rwkv_w2[i], rwkv_a0[i], rwkv_a2[i],
              rwkv_g2[i], rwkv_kk[i], rwkv_ka[i], rwkv_rk[i], rwkv_lnx_g[i], rwkv_lnx_b[i], mla_qnorm_g[i],
              mla_w_uq[i], mla_kvnorm_g[i], mla_w_ukv[i], fox_bf[i], w_br[i], w_mg[i], w_o[i], ln1_g[i], ln1_b[i],
              router_w, router_b, exp_w_gate[i], exp_w_up[i], exp_w_down[i], ln2_g[i], ln2_b[i], ple_w[i],
              ple_gate_w[i], ln3_g[i], ln3_b[i])
        xp, st_p = trunk_layer(xp, p_prompt[i], *fresh, *lw)
        new_p.append(st_p)
        xs, st_s = trunk_layer(xs, p_sample[i], state_hgrn[i], state_rwkv[i], state_rwkv_shift[i],
                               cache_mla_latent[i], cache_mla_krope[i], cache_fox_k[i], cache_fox_v[i],
                               cache_fox_logf[i], *lw)
        new_s.append(st_s)
    stack = lambda sts, j: jnp.stack([s[j] for s in sts], axis=0)
    p_hgrn, p_rwkv, p_shift, p_lat, p_kr, p_fk, p_fv, p_flf = (stack(new_p, j) for j in range(8))
    s_hgrn, s_rwkv, s_shift, s_lat, s_kr, s_fk, s_fv, s_flf = (stack(new_s, j) for j in range(8))
    return (xp, xs, p_hgrn, p_rwkv, p_shift, p_lat, p_kr, p_fk, p_fv, p_flf,
            s_hgrn, s_rwkv, s_shift, s_lat, s_kr, s_fk, s_fv, s_flf)
```

```python
import functools
import math

import numpy as np
import jax
import jax.numpy as jnp
from jax import lax
from jax.experimental import pallas as pl
from jax.experimental.pallas import tpu as pltpu

F32 = jnp.float32
BF16 = jnp.bfloat16

D_MODEL = 1024
N_HEADS = 4
HEAD_DIM = 64
BRANCH_W = N_HEADS * HEAD_DIM
CHUNK = 64
W_LORA, A_LORA, G_LORA = 32, 32, 64
NOPE, ROPE, Q_LORA, KV_LORA = 64, 32, 192, 128
ROPE_THETA = 10000.0
N_EXPERTS, N_GROUPS, EXPERT_FF = 16, 4, 256
GROUP_SIZE = N_EXPERTS // N_GROUPS
PLE_DIM = 256
A_COLS = 4 * BRANCH_W
B_COLS = 3 * BRANCH_W + W_LORA + A_LORA + G_LORA
C_COLS = Q_LORA + KV_LORA + ROPE
D_COLS = 3 * BRANCH_W + N_HEADS
DEPTH_ALPHA_POW = 0.25
LN_EPS = 1e-5
RMS_EPS = 1e-6
RWKV_GN_EPS = 64e-5
MASK_VALUE = -1e30

LANES = 128
SUBLANES = 8
VMEM_LIMIT_BYTES = 56 * 1024 * 1024

NN = ((1,), (0,))
NT = ((1,), (1,))
TN = ((0,), (0,))


def _dg(a, b, dims=NN):
    return lax.dot_general(a, b, (dims, ((), ())), preferred_element_type=F32)


def _bdot(a, b, dims=NN):
    return _dg(a.astype(BF16), b.astype(BF16), dims)


def _split(x, pieces):
    out = []
    r = x
    for i in range(pieces):
        p = r.astype(BF16)
        out.append(p)
        if i + 1 < pieces:
            r = r - p.astype(F32)
    return out


def _dot3(a, b, dims=NN):
    ah, al = _split(a, 2)
    bh, bl = _split(b, 2)
    return _dg(ah, bh, dims) + (_dg(ah, bl, dims) + _dg(al, bh, dims))


def _dotc(a, c, dims=NN, pieces=3):
    ps = _split(a, pieces)
    acc = _dg(ps[0], c, dims)
    for p in ps[1:]:
        acc = acc + _dg(p, c, dims)
    return acc


def _iota(shape, dim):
    return lax.broadcasted_iota(jnp.int32, shape, dim)


def _tri_incl(n, dtype=BF16):
    return (_iota((n, n), 0) >= _iota((n, n), 1)).astype(dtype)


def _head_indicator():
    r = _iota((BRANCH_W, BRANCH_W), 0) // HEAD_DIM
    c = _iota((BRANCH_W, BRANCH_W), 1) // HEAD_DIM
    return (r == c).astype(BF16)


def _layer_norm(x, g, b):
    mu = jnp.mean(x, axis=-1, keepdims=True)
    xc = x - mu
    var = jnp.mean(xc * xc, axis=-1, keepdims=True)
    return xc * lax.rsqrt(var + LN_EPS) * g + b


def _softplus(x):
    return jnp.maximum(x, 0.0) + jnp.log(1.0 + jnp.exp(-jnp.abs(x)))


def _silu(x):
    return x * jax.nn.sigmoid(x)


def _cparams(*sem):
    return pltpu.CompilerParams(dimension_semantics=sem, vmem_limit_bytes=VMEM_LIMIT_BYTES)


def _resident(shape):
    nd = len(shape)
    return pl.BlockSpec(shape, lambda *_: (0,) * nd)


def _token_tile(t):
    for tm in (512, 256, 128, 64, 32, 16, 8):
        if t % tm == 0:
            return tm
    raise ValueError(f"token count {t} not a multiple of 8")


def _ln_kernel(x_ref, g_ref, b_ref, o_ref):
    o_ref[...] = _layer_norm(x_ref[...], g_ref[...], b_ref[...])


def _ln_call(x, g, b):
    t, d = x.shape
    tm = _token_tile(t)
    return pl.pallas_call(
        _ln_kernel,
        grid=(t // tm,),
        in_specs=[pl.BlockSpec((tm, d), lambda i: (i, 0)), _resident((1, d)), _resident((1, d))],
        out_specs=pl.BlockSpec((tm, d), lambda i: (i, 0)),
        out_shape=jax.ShapeDtypeStruct((t, d), F32),
        compiler_params=_cparams("parallel"),
        name="ln_in",
    )(x, g.reshape(1, d), b.reshape(1, d))


_IN_SLOTS = (("a", A_COLS, A_COLS), ("b", B_COLS, B_COLS), ("qlat", Q_LORA, 256), ("kvlat", KV_LORA, 128),
             ("kr", ROPE, 128), ("fq", BRANCH_W, BRANCH_W), ("fk", BRANCH_W, BRANCH_W), ("fv", BRANCH_W, BRANCH_W),
             ("ff", N_HEADS, 128))


def _relayout_w_in(w_in):
    parts, off = [], 0
    for _, width, slot in _IN_SLOTS:
        w = w_in[:, off:off + width]
        if slot > width:
            w = jnp.pad(w, ((0, 0), (0, slot - width)))
        parts.append(w)
        off += width
    assert off == w_in.shape[1]
    return jnp.concatenate(parts, axis=1).astype(BF16)


def _in_kernel(x_ref, w_ref, *o_refs):
    xb = x_ref[...].astype(BF16)
    off = 0
    for o_ref, (_, _, slot) in zip(o_refs, _IN_SLOTS):
        o_ref[...] = _dg(xb, w_ref[:, off:off + slot])
        off += slot


def _in_call(x, w_in_p):
    t, d = x.shape
    tm = _token_tile(t)
    n = w_in_p.shape[1]
    return pl.pallas_call(
        _in_kernel,
        grid=(t // tm,),
        in_specs=[pl.BlockSpec((tm, d), lambda i: (i, 0)), _resident((d, n))],
        out_specs=[pl.BlockSpec((tm, slot), lambda i: (i, 0)) for _, _, slot in _IN_SLOTS],
        out_shape=[jax.ShapeDtypeStruct((t, slot), F32) for _, _, slot in _IN_SLOTS],
        compiler_params=_cparams("parallel"),
        name="in_proj",
    )(x, w_in_p)


def _hgrn_kernel(c_ref, lb_ref, ng_ref, s0_ref, o_ref, s_ref, st_scr, *, chunk):
    j = pl.program_id(1)
    c = chunk

    @pl.when(j == 0)
    def _():
        st_scr[...] = s0_ref[0]

    cols = c_ref[0]
    w = BRANCH_W
    q, fz, iv, g = cols[:, :w], cols[:, w:2 * w], cols[:, 2 * w:3 * w], cols[:, 3 * w:]
    lb = lb_ref[...]
    f = lb + (1.0 - lb) * jax.nn.sigmoid(fz)
    logf = jnp.log(f)
    k = 1.0 - f
    a = _dotc(_tri_incl(c), logf, pieces=3)
    a_last = a[c - 1:c, :]
    st = st_scr[...]
    ind = _head_indicator()

    o_inter = _bdot(q * jnp.exp(a), st, NT)
    kd = k * jnp.exp(a_last - a)
    rr = _iota((w, w), 0) // HEAD_DIM
    cc = _iota((w, w), 1) // HEAD_DIM
    upd = jnp.where(rr == cc, _bdot(iv, kd, TN), 0.0)
    st_scr[...] = st * jnp.exp(a_last) + upd

    row = _iota((SUBLANES, w), 0)
    outs = []
    for tb in range(c // SUBLANES):
        lo = tb * SUBLANES
        q_tb = q[lo:lo + SUBLANES]
        a_tb = a[lo:lo + SUBLANES]
        pieces = []
        for s in range(lo + SUBLANES):
            d = a_tb - a[s:s + 1]
            if s >= lo:
                d = jnp.where(row >= s - lo, d, MASK_VALUE)
            pieces.append(q_tb * jnp.exp(d) * k[s:s + 1])
        e = _dotc(jnp.concatenate(pieces, axis=0), ind, pieces=2)
        acc = e[0:SUBLANES] * iv[0:1]
        for s in range(1, lo + SUBLANES):
            acc = acc + e[s * SUBLANES:(s + 1) * SUBLANES] * iv[s:s + 1]
        outs.append(acc)
    o = o_inter + jnp.concatenate(outs, axis=0)

    ms = _dotc(o * o, ind, pieces=2) * (1.0 / HEAD_DIM)
    o = o * lax.rsqrt(ms + RMS_EPS) * ng_ref[...] * _silu(g)
    o_ref[0] = o.astype(o_ref.dtype)

    @pl.when(j == pl.num_programs(1) - 1)
    def _():
        s_ref[0] = st_scr[...]


def _hgrn_call(cols_a, lb, norm_g, st0):
    b, l, _ = cols_a.shape
    c = CHUNK if l % CHUNK == 0 else l
    w = BRANCH_W
    return pl.pallas_call(
        functools.partial(_hgrn_kernel, chunk=c),
        grid=(b, l // c),
        in_specs=[pl.BlockSpec((1, c, A_COLS), lambda i, j: (i, j, 0)), _resident((1, w)), _resident((1, w)),
                  pl.BlockSpec((1, w, w), lambda i, j: (i, 0, 0))],
        out_specs=[pl.BlockSpec((1, c, w), lambda i, j: (i, j, 0)), pl.BlockSpec((1, w, w), lambda i, j: (i, 0, 0))],
        out_shape=[jax.ShapeDtypeStruct((b, l, w), BF16), jax.ShapeDtypeStruct((b, w, w), F32)],
        scratch_shapes=[pltpu.VMEM((w, w), F32)],
        compiler_params=_cparams("parallel", "arbitrary"),
        name="hgrn",
    )(cols_a, lb.reshape(1, w), norm_g.reshape(1, w), st0)


def _hgrn_state_to_blockdiag(s):
    b = s.shape[0]
    out = jnp.zeros((b, N_HEADS, HEAD_DIM, N_HEADS, HEAD_DIM), F32)
    for h in range(N_HEADS):
        out = out.at[:, h, :, h, :].set(jnp.swapaxes(s[:, h], -1, -2))
    return out.reshape(b, BRANCH_W, BRANCH_W)


def _hgrn_state_from_blockdiag(st):
    hs = [st[:, h * HEAD_DIM:(h + 1) * HEAD_DIM, h * HEAD_DIM:(h + 1) * HEAD_DIM] for h in range(N_HEADS)]
    return jnp.swapaxes(jnp.stack(hs, axis=1), -1, -2)


def _rwkv_prep_kernel(c_ref, pblk_ref, prow_ref, mu_ref, w0_ref, lora_ref, a0_ref, kk_ref, ka_ref,
                      rk_ref, r_ref, lw_ref, k_ref, v_ref, a_ref, b_ref, g_ref, bonus_ref):
    j = pl.program_id(1)
    cols = c_ref[0]
    prev = jnp.where(j == 0, prow_ref[0], pblk_ref[0, SUBLANES - 1:SUBLANES, :])
    shifted = jnp.where(_iota(cols.shape, 0) == 0, prev, pltpu.roll(cols, 1, 0))
    m = cols + (shifted - cols) * mu_ref[...]
    w = BRANCH_W
    r, k, v = m[:, :w], m[:, w:2 * w], m[:, 2 * w:3 * w]
    slab = m[:, 3 * w:]
    lane = _iota(slab.shape, 1)
    act = jnp.where(lane < W_LORA, jnp.tanh(slab),
                    jnp.where(lane < W_LORA + A_LORA, slab, jax.nn.sigmoid(slab)))
    lora = _bdot(act, lora_ref[...])
    w_log = -_softplus(-(w0_ref[...] + lora[:, :w])) - 0.5
    a_rate = jax.nn.sigmoid(a0_ref[...] + lora[:, w:2 * w])
    g = lora[:, 2 * w:]
    ind = _head_indicator()
    kk = k * kk_ref[...]
    norm = jnp.sqrt(_dotc(kk * kk, ind, pieces=3))
    kk = kk / jnp.maximum(norm, 1e-12)
    kh = k * (1.0 + (a_rate - 1.0) * ka_ref[...])
    r_ref[0] = r
    lw_ref[0] = -jnp.exp(w_log)
    k_ref[0] = kh
    v_ref[0] = v
    a_ref[0] = -kk
    b_ref[0] = kk * a_rate
    g_ref[0] = g
    bonus_ref[0] = _dotc(r * kh * rk_ref[...], ind, pieces=3) * v


def _rwkv_prep_call(cols_b, prev_row, mu, w0, w2, a0, a2, g2, kk_s, ka, rk):
    b, l, nb = cols_b.shape
    tm = _token_tile(l)
    w = BRANCH_W
    row = lambda x: x.reshape(1, -1)
    lora_w = jnp.zeros((LANES, 3 * w), F32)
    lora_w = lora_w.at[:W_LORA, :w].set(w2).at[W_LORA:W_LORA + A_LORA, w:2 * w].set(a2)
    lora_w = lora_w.at[W_LORA + A_LORA:, 2 * w:].set(g2).astype(BF16)
    tok = pl.BlockSpec((1, tm, nb), lambda i, j: (i, j, 0))
    pblk =pl.BlockSpec((1, SUBLANES, nb), lambda i, j: (i, jnp.maximum(j * (tm // SUBLANES) - 1, 0), 0))
    out_spec = pl.BlockSpec((1, tm, w), lambda i, j: (i, j, 0))
    return pl.pallas_call(
        _rwkv_prep_kernel,
        grid=(b, l // tm),
        in_specs=[tok, pblk, pl.BlockSpec((1, 1, nb), lambda i, j: (i, 0, 0)), _resident((1, nb)),
                  _resident((1, w)), _resident((LANES, 3 * w)), _resident((1, w)),
                  _resident((1, w)), _resident((1, w)), _resident((1, w))],
        out_specs=[out_spec] * 8,
        out_shape=[jax.ShapeDtypeStruct((b, l, w), F32)] * 8,
        compiler_params=_cparams("parallel", "parallel"),
        name="rwkv_prep",
    )(cols_b, cols_b, prev_row.reshape(b, 1, nb), row(mu), row(w0), lora_w, row(a0), row(kk_s), row(ka), row(rk))


def _rwkv_chunk_kernel(r_ref, lw_ref, k_ref, v_ref, a_ref, b_ref, g_ref, bonus_ref, lg_ref, lb_ref, s0_ref,
                       o_ref, s_ref, st_scr, *, chunk):
    j = pl.program_id(1)
    c = chunk
    w = BRANCH_W
    hd = HEAD_DIM
    nh = N_HEADS

    @pl.when(j == 0)
    def _():
        for h in range(nh):
            st_scr[:, h * hd:(h + 1) * hd] = s0_ref[0, h]

    r, lw, k, v, a, b = r_ref[0], lw_ref[0], k_ref[0], v_ref[0], a_ref[0], b_ref[0]
    cum = _dotc(_tri_incl(c), lw, pieces=3)
    last = cum[c - 1:c, :]
    at = a * jnp.exp(cum - lw)
    rt = r * jnp.exp(cum)
    e_neg = jnp.exp(-cum)
    bt = b * e_neg
    kt = k * e_neg
    e_last = jnp.exp(last - cum)
    bh = b * e_last
    kh = k * e_last

    lane_head = _iota((c, w), 1) // hd

    def head_rows(x):
        return jnp.concatenate([jnp.where(lane_head == h, x, 0.0) for h in range(nh)], axis=0)

    def head_stack(x):
        return jnp.concatenate([x[:, h * hd:(h + 1) * hd] for h in range(nh)], axis=0)

    n = nh * c
    lhs = jnp.concatenate([head_rows(at), head_rows(rt)], axis=0)
    rhs = jnp.concatenate([bt] * nh + [kt] * nh, axis=0)
    pair = _dot3(lhs, rhs, NT)

    rown = _iota((n, n), 0)
    coln = _iota((n, n), 1)
    same_head = (rown // c) == (coln // c)
    strict = same_head & ((rown % c) > (coln % c))
    incl = same_head & ((rown % c) >= (coln % c))
    l_ab = jnp.where(strict, pair[:n, :n], 0.0)
    l_ak = jnp.where(strict, pair[:n, n:], 0.0)
    a_rb = jnp.where(incl, pair[n:, :n], 0.0)
    a_rk = jnp.where(incl, pair[n:, n:], 0.0)

    eye = (rown == coln).astype(F32)
    tinv = eye + l_ab
    p = l_ab
    for _ in range(int(math.log2(c)) - 1):
        p = _dot3(p, p)
        tinv = tinv + _dot3(tinv, p)

    st = st_scr[...]
    sh = _dot3(lhs, st, NT)
    vs = head_stack(v)
    u = _dot3(tinv, sh[:n] + _dot3(l_ak, vs))
    y = sh[n:] + _dot3(a_rb, u) + _dot3(a_rk, vs)
    upd = _dot3(jnp.concatenate([u, vs], axis=0),
                jnp.concatenate([head_rows(bh), head_rows(kh)], axis=0), TN)
    st_scr[...] = st * jnp.exp(last) + upd

    for h in range(nh):
        yh = y[h * c:(h + 1) * c]
        sl = slice(h * hd, (h + 1) * hd)
        mu = jnp.mean(yh, axis=-1, keepdims=True)
        yc = yh - mu
        var = jnp.mean(yc * yc, axis=-1, keepdims=True)
        yn = yc * lax.rsqrt(var + RWKV_GN_EPS) * lg_ref[:, sl] + lb_ref[:, sl]
        o_ref[0, :, sl] = ((yn + bonus_ref[0, :, sl]) * g_ref[0, :, sl]).astype(o_ref.dtype)

    @pl.when(j == pl.num_programs(1) - 1)
    def _():
        for h in range(nh):
            s_ref[0, h] = st_scr[:, h * hd:(h + 1) * hd]


def _rwkv_chunk_call(prep, lnx_g, lnx_b, s0):
    r = prep[0]
    b, l, w = r.shape
    c = CHUNK if l % CHUNK == 0 else l
    tok = pl.BlockSpec((1, c, w), lambda i, j: (i, j, 0))
    st_spec = pl.BlockSpec((1, N_HEADS, HEAD_DIM, HEAD_DIM), lambda i, j: (i, 0, 0, 0))
    return pl.pallas_call(
        functools.partial(_rwkv_chunk_kernel, chunk=c),
        grid=(b, l // c),
        in_specs=[tok] * 8 + [_resident((1, w)), _resident((1, w)), st_spec],
        out_specs=[tok, st_spec],
        out_shape=[jax.ShapeDtypeStruct((b, l, w), BF16),
                   jax.ShapeDtypeStruct((b, N_HEADS, HEAD_DIM, HEAD_DIM), F32)],
        scratch_shapes=[pltpu.VMEM((HEAD_DIM, w), F32)],
        compiler_params=_cparams("parallel", "arbitrary"),
        name="rwkv_chunk",
    )(*prep, lnx_g.reshape(1, w), lnx_b.reshape(1, w), s0)


def _rope_tables(p, l):
    half = ROPE // 2
    inv = 1.0 / (ROPE_THETA ** (jnp.arange(half, dtype=F32) / half))
    ang = (p + jnp.arange(l, dtype=jnp.int32)).astype(F32)[:, None] * inv[None, :]
    cos, sin = jnp.cos(ang), jnp.sin(ang)
    pad = jnp.zeros((l, LANES - ROPE), F32)
    return jnp.concatenate([cos, cos, pad], axis=1), jnp.concatenate([-sin, sin, pad], axis=1)


def _swap_halves(x, base):
    half = ROPE // 2
    n = x.shape[-1]
    lane = _iota(x.shape, x.ndim - 1) % LANES
    up = pltpu.roll(x, n - half, x.ndim - 1)
    down = pltpu.roll(x, half, x.ndim - 1)
    return jnp.where((lane >= base) & (lane < base + half), up,
                     jnp.where((lane >= base + half) & (lane < base + ROPE), down, 0.0))


def _rms_norm(x, g, width):
    ms = jnp.sum(x * x, axis=-1, keepdims=True) * (1.0 / width)
    return x * lax.rsqrt(ms + RMS_EPS) * g


def _mla_q_kernel(ql_ref, kvl_ref, kr_ref, cos_ref, sin_ref, qg_ref, wuq_ref, kvg_ref, q_ref, c_ref, krn_ref):
    scale = (NOPE + ROPE) ** -0.5
    qn = _rms_norm(ql_ref[0], qg_ref[...], Q_LORA)
    q = _bdot(qn, wuq_ref[...])
    cos, sin = cos_ref[...], sin_ref[...]
    lane = _iota(cos.shape, 1)
    cos_q = jnp.where(lane < NOPE, 1.0, pltpu.roll(cos, NOPE, 1))
    sin_q = pltpu.roll(sin, NOPE, 1)
    for h in range(N_HEADS):
        qh = q[:, h * LANES:(h + 1) * LANES]
        qh = qh * cos_q + _swap_halves(qh, NOPE) * sin_q
        q_ref[0, h] = (qh * scale).astype(q_ref.dtype)
    c_ref[0] = _rms_norm(kvl_ref[0], kvg_ref[...], KV_LORA)
    kr = kr_ref[0]
    krn = kr * cos + _swap_halves(kr, 0) * sin
    krn_ref[0] = krn[:, :ROPE]


def _mla_q_call(qlat, kvlat, kr, cos, sin, qn_g, w_uq_p, kvn_g):
    b, l, _ = qlat.shape
    tm = _token_tile(l)
    tok = lambda n: pl.BlockSpec((1, tm, n), lambda i, j: (i, j, 0))
    tab = pl.BlockSpec((tm, LANES), lambda i, j: (j, 0))
    return pl.pallas_call(
        _mla_q_kernel,
        grid=(b, l // tm),
        in_specs=[tok(256), tok(KV_LORA), tok(LANES), tab, tab, _resident((1, 256)),
                  _resident((256, N_HEADS * LANES)), _resident((1, KV_LORA))],
        out_specs=[pl.BlockSpec((1, N_HEADS, tm, LANES), lambda i, j: (i, 0, j, 0)), tok(KV_LORA), tok(ROPE)],
        out_shape=[jax.ShapeDtypeStruct((b, N_HEADS, l, LANES), BF16), jax.ShapeDtypeStruct((b, l, KV_LORA), F32),
                   jax.ShapeDtypeStruct((b, l, ROPE), F32)],
        compiler_params=_cparams("parallel", "parallel"),
        name="mla_q",
    )(qlat, kvlat, kr, cos, sin, qn_g, w_uq_p, kvn_g)


def _mla_kv_kernel(c_ref, kr_ref, wk_ref, wv_ref, k_ref, v_ref):
    cb = c_ref[0].astype(BF16)
    kall = _dg(cb, wk_ref[...])
    vall = _dg(cb, wv_ref[...])
    tm = cb.shape[0]
    place = (_iota((ROPE, LANES), 0) + NOPE == _iota((ROPE, LANES), 1)).astype(BF16)
    kr_slot = _dg(kr_ref[0].astype(BF16), place)
    one_slot = (_iota((tm, LANES), 1) == HEAD_DIM).astype(F32)
    for h in range(N_HEADS):
        k_ref[0, h] = (kall[:, h * LANES:(h + 1) * LANES] + kr_slot).astype(k_ref.dtype)
        v_ref[0, h] = (vall[:, h * LANES:(h + 1) * LANES] + one_slot).astype(v_ref.dtype)


def _mla_kv_call(c_all, kr_all, w_k_p, w_v_p):
    b, t, _ = c_all.shape
    tm = _token_tile(t)
    tok = lambda n: pl.BlockSpec((1, tm, n), lambda i, j: (i, j, 0))
    hspec = pl.BlockSpec((1, N_HEADS, tm, LANES), lambda i, j: (i, 0, j, 0))
    return pl.pallas_call(
        _mla_kv_kernel,
        grid=(b, t // tm),
        in_specs=[tok(KV_LORA), tok(ROPE), _resident((KV_LORA, N_HEADS * LANES)),
                  _resident((KV_LORA, N_HEADS * LANES))],
        out_specs=[hspec, hspec],
        out_shape=[jax.ShapeDtypeStruct((b, N_HEADS, t, LANES), BF16)] * 2,
        compiler_params=_cparams("parallel", "parallel"),
        name="mla_kv",
    )(c_all, kr_all, w_k_p, w_v_p)


def _fox_lf_kernel(ff_ref, bf_ref, lf_ref, lfw_ref):
    z = ff_ref[0] + bf_ref[...]
    lf = jnp.where(_iota(z.shape, 1) < N_HEADS, -_softplus(-z), 0.0)
    lfw_ref[0] = lf
    lf_ref[0] = lf[:, :N_HEADS]


def _fox_lf_call(ff, bf):
    b, l, _ = ff.shape
    tm = _token_tile(l)
    return pl.pallas_call(
        _fox_lf_kernel,
        grid=(b, l // tm),
        in_specs=[pl.BlockSpec((1, tm, LANES), lambda i, j: (i, j, 0)), _resident((1, LANES))],
        out_specs=[pl.BlockSpec((1, tm, N_HEADS), lambda i, j: (i, j, 0)),
                   pl.BlockSpec((1, tm, LANES), lambda i, j: (i, j, 0))],
        out_shape=[jax.ShapeDtypeStruct((b, l, N_HEADS), F32), jax.ShapeDtypeStruct((b, l, LANES), F32)],
        compiler_params=_cparams("parallel", "parallel"),
        name="fox_lf",
    )(ff, jnp.pad(bf, (0, LANES - N_HEADS)).reshape(1, LANES))


def _cumsum_kernel(x_ref, o_ref, carry):
    @pl.when(pl.program_id(1) == 0)
    def _():
        carry[...] = jnp.zeros_like(carry)

    x = x_ref[0]
    c = _dotc(_tri_incl(x.shape[0]), x, pieces=3) + carry[...]
    o_ref[0] = c
    carry[...] = c[x.shape[0] - 1:, :]


def _cumsum_call(x):
    b, t, n = x.shape
    tm = _token_tile(t)
    spec = pl.BlockSpec((1, tm, n), lambda i, j: (i, j, 0))
    return pl.pallas_call(
        _cumsum_kernel,
        grid=(b, t // tm),
        in_specs=[spec],
        out_specs=spec,
        out_shape=jax.ShapeDtypeStruct((b, t, n), F32),
        scratch_shapes=[pltpu.VMEM((1, n), F32)],
        compiler_params=_cparams("parallel", "arbitrary"),
        name="fox_cumsum",
    )(x)


def _head_slot(x, h):
    pair = x[:, (h // 2) * LANES:(h // 2 + 1) * LANES]
    return pair if h % 2 == 0 else pltpu.roll(pair, HEAD_DIM, 1)


def _bias_lanes(c_col, first, ones_first):
    tm = c_col.shape[0]
    lane = _iota((tm, LANES), 1)
    p0 = c_col.astype(BF16).astype(F32)
    r1 = c_col - p0
    p1 = r1.astype(BF16).astype(F32)
    p2 = r1 - p1
    out = jnp.where(lane == first, p0, jnp.where(lane == first + 1, p1, jnp.where(lane == first + 2, p2, 0.0)))
    return jnp.where((lane >= ones_first) & (lane < ones_first + 3), 1.0, out)


def _fox_q_kernel(q_ref, c_ref, o_ref):
    scale = HEAD_DIM ** -0.5
    q = q_ref[0]
    c = c_ref[0]
    lane = _iota((q.shape[0], LANES), 1)
    for h in range(N_HEADS):
        bias = _bias_lanes(c[:, h:h + 1], HEAD_DIM, HEAD_DIM + 3)
        o_ref[0, h] = jnp.where(lane < HEAD_DIM, _head_slot(q, h) * scale, bias).astype(o_ref.dtype)


def _fox_kv_kernel(k_ref, v_ref, c_ref, ko_ref, vo_ref):
    k, v, c = k_ref[0], v_ref[0], c_ref[0]
    lane = _iota((k.shape[0], LANES), 1)
    for h in range(N_HEADS):
        bias = _bias_lanes(-c[:, h:h + 1], HEAD_DIM + 3, HEAD_DIM)
        ko_ref[0, h] = jnp.where(lane < HEAD_DIM, _head_slot(k, h), bias).astype(ko_ref.dtype)
        vo_ref[0, h] = jnp.where(lane < HEAD_DIM, _head_slot(v, h),
                                 (lane == HEAD_DIM).astype(F32)).astype(vo_ref.dtype)


def _fox_q_call(q, c_new):
    b, l, w = q.shape
    tm = _token_tile(l)
    return pl.pallas_call(
        _fox_q_kernel,
        grid=(b, l // tm),
        in_specs=[pl.BlockSpec((1, tm, w), lambda i, j: (i, j, 0)),
                  pl.BlockSpec((1, tm, LANES), lambda i, j: (i, j, 0))],
        out_specs=pl.BlockSpec((1, N_HEADS, tm, LANES), lambda i, j: (i, 0, j, 0)),
        out_shape=jax.ShapeDtypeStruct((b, N_HEADS, l, LANES), BF16),
        compiler_params=_cparams("parallel", "parallel"),
        name="fox_q",
    )(q, c_new)


def _fox_kv_call(k_all, v_all, c_all):
    b, t, w = k_all.shape
    tm = _token_tile(t)
    tok = pl.BlockSpec((1, tm, w), lambda i, j: (i, j, 0))
    hspec = pl.BlockSpec((1, N_HEADS, tm, LANES), lambda i, j: (i, 0, j, 0))
    return pl.pallas_call(
        _fox_kv_kernel,
        grid=(b, t // tm),
        in_specs=[tok, tok, pl.BlockSpec((1, tm, LANES), lambda i, j: (i, j, 0))],
        out_specs=[hspec, hspec],
        out_shape=[jax.ShapeDtypeStruct((b, N_HEADS, t, LANES), BF16)] * 2,
        compiler_params=_cparams("parallel", "parallel"),
        name="fox_kv",
    )(k_all, v_all, c_all)


def _flash_kernel(q_ref, k_ref, v_ref, o_ref, m_scr, acc_scr, *, tq, tk, past, t_valid, chunk_mask):
    qi = pl.program_id(1)
    q_start = past + qi * tq
    if chunk_mask:
        vis_end = jnp.minimum(((q_start + tq + CHUNK - 1) // CHUNK) * CHUNK, t_valid)
    else:
        vis_end = q_start + tq
    n_blocks = (vis_end + tk - 1) // tk
    n_full = q_start // tk

    for h in range(N_HEADS):
        q = q_ref[0, h]
        m_scr[...] = jnp.full(m_scr.shape, MASK_VALUE, F32)
        acc_scr[...] = jnp.zeros(acc_scr.shape, F32)

        def block(kb, masked):
            ks = pl.multiple_of(kb * tk, tk)
            kblk = k_ref[0, h, pl.ds(ks, tk), :]
            vblk = v_ref[0, h, pl.ds(ks, tk), :]
            s = _dg(q, kblk, NT)
            if masked:
                qpos = q_start + _iota((tq, tk), 0)
                kpos = ks + _iota((tq, tk), 1)
                if chunk_mask:
                    keep = ((kpos // CHUNK) <= (qpos // CHUNK)) & (kpos < t_valid)
                else:
                    keep = kpos <= qpos
                s = jnp.where(keep, s, MASK_VALUE)
            m_old = m_scr[...]
            m_new = jnp.maximum(m_old, jnp.max(s, axis=-1, keepdims=True))
            p = jnp.exp(s - m_new[:, :1])
            acc_scr[...] = acc_scr[...] * jnp.exp(m_old - m_new) + _dg(p.astype(BF16), vblk)
            m_scr[...] = m_new

        def full_body(kb, carry):
            block(kb, False)
            return carry

        def masked_body(kb, carry):
            block(kb, True)
            return carry

        lax.fori_loop(0, n_full, full_body, 0)
        lax.fori_loop(n_full, n_blocks, masked_body, 0)
        acc = acc_scr[...]
        o = acc[:, :HEAD_DIM] / acc[:, HEAD_DIM:HEAD_DIM + 1]
        o_ref[0, :, h * HEAD_DIM:(h + 1) * HEAD_DIM] = o.astype(o_ref.dtype)


def _flash_call(q, k, v, *, past, t_valid, chunk_mask, name):
    b, nh, l, _ = q.shape
    t = k.shape[2]
    tq = _token_tile(l)
    tk = min(512, _token_tile(t))
    kv_spec = pl.BlockSpec((1, nh, t, LANES), lambda i, j: (i, 0, 0, 0))
    return pl.pallas_call(
        functools.partial(_flash_kernel, tq=tq, tk=tk, past=past, t_valid=t_valid, chunk_mask=chunk_mask),
        grid=(b, l // tq),
        in_specs=[pl.BlockSpec((1, nh, tq, LANES), lambda i, j: (i, 0, j, 0)), kv_spec, kv_spec],
        out_specs=pl.BlockSpec((1, tq, BRANCH_W), lambda i, j: (i, j, 0)),
        out_shape=jax.ShapeDtypeStruct((b, l, BRANCH_W), BF16),
        scratch_shapes=[pltpu.VMEM((tq, LANES), F32), pltpu.VMEM((tq, LANES), F32)],
        compiler_params=_cparams("parallel", "arbitrary"),
        name=name,
    )(q, k, v)


def _mix_kernel(x_ref, oa_ref, ob_ref, oc_ref, od_ref, wmg_ref, wbr_ref, wo_ref, g_ref, b_ref, o_ref, *, alpha):
    x = x_ref[...]
    xb = x.astype(BF16)
    d = D_MODEL
    acc = None
    for n, br_ref in enumerate((oa_ref, ob_ref, oc_ref, od_ref)):
        gate = jax.nn.sigmoid(_dg(xb, wmg_ref[:, n * d:(n + 1) * d]))
        term = gate * _dg(br_ref[...], wbr_ref[n])
        acc = term if acc is None else acc + term
    mix = _dg(acc.astype(BF16), wo_ref[...])
    o_ref[...] = _layer_norm(alpha * x + mix, g_ref[...], b_ref[...])


def _mix_call(x, oa, ob, oc, od, w_mg, w_br, w_o, g, b, alpha):
    t, d = x.shape
    tm = _token_tile(t)
    tok = lambda n: pl.BlockSpec((tm, n), lambda i: (i, 0))
    return pl.pallas_call(
        functools.partial(_mix_kernel, alpha=alpha),
        grid=(t // tm,),
        in_specs=[tok(d)] + [tok(BRANCH_W)] * 4 + [_resident(w_mg.shape), _resident(w_br.shape),
                                                   _resident(w_o.shape), _resident((1, d)), _resident((1, d))],
        out_specs=tok(d),
        out_shape=jax.ShapeDtypeStruct((t, d), F32),
        compiler_params=_cparams("parallel"),
        name="mix_ln1",
    )(x, oa, ob, oc, od, w_mg, w_br, w_o, g.reshape(1, d), b.reshape(1, d))


def _route(x, rw_ref, rb_ref):
    scores = jax.nn.sigmoid(_dot3(x, rw_ref[...]))
    biased = scores + rb_ref[...]
    col = [biased[:, e:e + 1] for e in range(N_EXPERTS)]
    gs = []
    for g in range(N_GROUPS):
        v = col[g * GROUP_SIZE:(g + 1) * GROUP_SIZE]
        best = None
        for i in range(GROUP_SIZE):
            for j in range(i + 1, GROUP_SIZE):
                s = v[i] + v[j]
                best = s if best is None else jnp.maximum(best, s)
        gs.append(best)
    gates = []
    for g in range(N_GROUPS):
        sel = None
        for o in range(N_GROUPS):
            if o == g:
                continue
            cond = (gs[g] > gs[o]) if o < g else (gs[g] >= gs[o])
            sel = cond if sel is None else sel & cond
        v = col[g * GROUP_SIZE:(g + 1) * GROUP_SIZE]
        for i in range(GROUP_SIZE):
            rank = None
            for j in range(GROUP_SIZE):
                if j == i:
                    continue
                ahead = (v[j] >= v[i]) if j < i else (v[j] > v[i])
                ahead = ahead.astype(F32)
                rank = ahead if rank is None else rank + ahead
            e = g * GROUP_SIZE + i
            gates.append(jnp.where(sel & (rank < 2.0), scores[:, e:e + 1], 0.0))
    total = gates[0]
    for gt in gates[1:]:
        total = total + gt
    return [gt / total for gt in gates]


def _moe_kernel(x_ref, rw_ref, rb_ref, wgu_ref, wd_ref, g_ref, b_ref, o_ref, acc_scr, gate_scr, *, alpha):
    e = pl.program_id(1)

    @pl.when(e == 0)
    def _():
        gates = _route(x_ref[...], rw_ref, rb_ref)
        lane = _iota(gate_scr.shape, 1)
        gm = jnp.zeros(gate_scr.shape, F32)
        for i, gt in enumerate(gates):
            gm = jnp.where(lane == i, gt, gm)
        gate_scr[...] = gm
        acc_scr[...] = jnp.zeros(acc_scr.shape, F32)

    xb = x_ref[...].astype(BF16)
    hgu = _dg(xb, wgu_ref[0])
    gm = gate_scr[...]
    gate = jnp.sum(jnp.where(_iota(gm.shape, 1) == e, gm, 0.0), axis=-1, keepdims=True)
    h = _silu(hgu[:, :EXPERT_FF]) * hgu[:, EXPERT_FF:]
    acc_scr[...] += gate * _dg(h.astype(BF16), wd_ref[0])

    @pl.when(e == pl.num_programs(1) - 1)
    def _():
        o_ref[...] = _layer_norm(alpha * x_ref[...] + acc_scr[...], g_ref[...], b_ref[...])


def _moe_call(x, router_w, router_b, w_gu, w_d, g, b, alpha):
    t, d = x.shape
    tm = _token_tile(t)
    tok = pl.BlockSpec((tm, d), lambda i, e: (i, 0))
    return pl.pallas_call(
        functools.partial(_moe_kernel, alpha=alpha),
        grid=(t // tm, N_EXPERTS),
        in_specs=[tok, _resident((d, LANES)), _resident((1, LANES)),
                  pl.BlockSpec((1, d, 2 * EXPERT_FF), lambda i, e: (e, 0, 0)),
                  pl.BlockSpec((1, EXPERT_FF, d), lambda i, e: (e, 0, 0)), _resident((1, d)), _resident((1, d))],
        out_specs=tok,
        out_shape=jax.ShapeDtypeStruct((t, d), F32),
        scratch_shapes=[pltpu.VMEM((tm, d), F32), pltpu.VMEM((tm, LANES), F32)],
        compiler_params=_cparams("parallel", "arbitrary"),
        name="moe_ln2",
    )(x, jnp.pad(router_w, ((0, 0), (0, LANES - N_EXPERTS))),
      jnp.pad(router_b, (0, LANES - N_EXPERTS)).reshape(1, LANES), w_gu, w_d, g.reshape(1, d), b.reshape(1, d))


def _ple_kernel(x_ref, p_ref, pw_ref, gw_ref, g_ref, b_ref, o_ref, *, alpha):
    x = x_ref[...]
    ple = _bdot(p_ref[...], pw_ref[...]) * jax.nn.sigmoid(_bdot(x, gw_ref[...]))
    o_ref[...] = _layer_norm(alpha * x + ple, g_ref[...], b_ref[...])


def _ple_call(x, p, ple_w, gate_w, g, b, alpha):
    t, d = x.shape
    tm = _token_tile(t)
    tok = lambda n: pl.BlockSpec((tm, n), lambda i: (i, 0))
    return pl.pallas_call(
        functools.partial(_ple_kernel, alpha=alpha),
        grid=(t // tm,),
        in_specs=[tok(d), tok(PLE_DIM), _resident(ple_w.shape), _resident(gate_w.shape), _resident((1, d)),
                  _resident((1, d))],
        out_specs=tok(d),
        out_shape=jax.ShapeDtypeStruct((t, d), F32),
        compiler_params=_cparams("parallel"),
        name="ple_ln3",
    )(x, p, ple_w, gate_w, g.reshape(1, d), b.reshape(1, d))


def _lb_kernel(x_ref, o_ref):
    x = x_ref[...]
    depth = x.shape[0]
    m = jnp.max(x, axis=0, keepdims=True)
    e = jnp.exp(x - m)
    pr = e / jnp.sum(e, axis=0, keepdims=True)
    run = jnp.zeros_like(pr[0:1])
    for i in range(depth):
        o_ref[i:i + 1, :] = run
        run = run + pr[i:i + 1]


def _lb_call(logits):
    return pl.pallas_call(
        _lb_kernel,
        out_shape=jax.ShapeDtypeStruct(logits.shape, F32),
        name="hgrn_lb",
    )(logits)


def _pad_rows(x, t):
    pad = t - x.shape[1]
    if pad == 0:
        return x
    return jnp.pad(x, ((0, 0), (0, pad)) + ((0, 0),) * (x.ndim - 2))


def _layer(x, p, st, lw, alpha):
    b, l, d = x.shape
    t = b * l
    hgrn_s, rwkv_s, rwkv_prev, lat_past, kr_past, fk_past, fv_past, flf_past = st
    past = 0 if lat_past is None else lat_past.shape[1]
    t_valid = past + l
    t_pad = -(-t_valid // LANES) * LANES if past else t_valid

    cols = _in_call(x.reshape(t, d), lw["w_in"])
    ca, cb, qlat, kvlat, kr, fq, fk, fv, ff = [c.reshape(b, l, -1) for c in cols]

    st0 = jnp.zeros((b, BRANCH_W, BRANCH_W), F32) if hgrn_s is None else _hgrn_state_to_blockdiag(hgrn_s)
    o_a, hgrn_bd = _hgrn_call(ca, lw["lb"], lw["hgrn_norm_g"], st0)
    hgrn_new = _hgrn_state_from_blockdiag(hgrn_bd)

    prev_row = jnp.zeros((b, B_COLS), F32) if rwkv_prev is None else rwkv_prev
    s0 = jnp.zeros((b, N_HEADS, HEAD_DIM, HEAD_DIM), F32) if rwkv_s is None else rwkv_s
    prep = _rwkv_prep_call(cb, prev_row, lw["rwkv_mu"], lw["rwkv_w0"], lw["rwkv_w2"], lw["rwkv_a0"], lw["rwkv_a2"],
                           lw["rwkv_g2"], lw["rwkv_kk"], lw["rwkv_ka"], lw["rwkv_rk"])
    o_b, rwkv_new = _rwkv_chunk_call(prep, lw["rwkv_lnx_g"], lw["rwkv_lnx_b"], s0)
    shift_new = cb[:, l - 1]

    cos, sin = _rope_tables(past, l)
    q_c, lat_new, kr_new = _mla_q_call(qlat, kvlat, kr, cos, sin, lw["mla_qn_g"], lw["mla_w_uq"], lw["mla_kvn_g"])
    if past:
        c_all = _pad_rows(jnp.concatenate([lat_past, lat_new], axis=1), t_pad)
        kr_all = _pad_rows(jnp.concatenate([kr_past, kr_new], axis=1), t_pad)
    else:
        c_all, kr_all = lat_new, kr_new
    k_c, v_c = _mla_kv_call(c_all, kr_all, lw["mla_w_k"], lw["mla_w_v"])
    o_c = _flash_call(q_c, k_c, v_c, past=past, t_valid=t_valid, chunk_mask=True, name="mla_attn")

    lf_new, lf_wide = _fox_lf_call(ff, lw["fox_bf"])
    if past:
        flf_wide = jnp.pad(flf_past, ((0, 0), (0, 0), (0, LANES - N_HEADS)))
        lf_all = _pad_rows(jnp.concatenate([flf_wide, lf_wide], axis=1), t_pad)
        k_all = _pad_rows(jnp.concatenate([fk_past.reshape(b, past, BRANCH_W), fk], axis=1), t_pad)
        v_all = _pad_rows(jnp.concatenate([fv_past.reshape(b, past, BRANCH_W), fv], axis=1), t_pad)
    else:
        lf_all, k_all, v_all = lf_wide, fk, fv
    c_all_f = _cumsum_call(lf_all)
    q_d = _fox_q_call(fq, c_all_f[:, past:past + l])
    k_d, v_d = _fox_kv_call(k_all, v_all, c_all_f)
    o_d = _flash_call(q_d, k_d, v_d, past=past, t_valid=t_valid, chunk_mask=False, name="fox_attn")

    flat = lambda o: o.reshape(t, BRANCH_W)
    x1 = _mix_call(x.reshape(t, d), flat(o_a), flat(o_b), flat(o_c), flat(o_d), lw["w_mg"], lw["w_br"], lw["w_o"],
                   lw["ln1_g"], lw["ln1_b"], alpha)
    x2 = _moe_call(x1, lw["router_w"], lw["router_b"], lw["w_gu"], lw["w_d"], lw["ln2_g"], lw["ln2_b"], alpha)
    x3 = _ple_call(x2, p.reshape(t, PLE_DIM), lw["ple_w"], lw["ple_gate_w"], lw["ln3_g"], lw["ln3_b"], alpha)
    new = (hgrn_new, rwkv_new, shift_new, lat_new, kr_new, fk.reshape(b, l, N_HEADS, HEAD_DIM),
           fv.reshape(b, l, N_HEADS, HEAD_DIM), lf_new)
    return x3.reshape(b, l, d), new


def _relayout_w_uq(w_uq):
    w = w_uq.reshape(Q_LORA, N_HEADS, NOPE + ROPE)
    w = jnp.pad(w, ((0, 256 - Q_LORA), (0, 0), (0, LANES - NOPE - ROPE)))
    return w.reshape(256, N_HEADS * LANES).astype(BF16)


def _relayout_w_ukv(w_ukv):
    w = w_ukv.reshape(KV_LORA, N_HEADS, NOPE + HEAD_DIM)
    pad = lambda x: jnp.pad(x, ((0, 0), (0, 0), (0, LANES - x.shape[-1]))).reshape(KV_LORA, N_HEADS * LANES)
    return pad(w[..., :NOPE]).astype(BF16), pad(w[..., NOPE:]).astype(BF16)


def kernel(x_prompt, x_sample, state_hgrn, state_rwkv, state_rwkv_shift, cache_mla_latent, cache_mla_krope, cache_fox_k, cache_fox_v, cache_fox_logf, p_prompt, p_sample, ln_in_g, ln_in_b, w_in, hgrn_lb_logits, hgrn_norm_g, rwkv_mu, rwkv_w0, rwkv_w2, rwkv_a0, rwkv_a2, rwkv_g2, rwkv_kk, rwkv_ka, rwkv_rk, rwkv_lnx_g, rwkv_lnx_b, mla_qnorm_g, mla_w_uq, mla_kvnorm_g, mla_w_ukv, fox_bf, w_br, w_mg, w_o, ln1_g, ln1_b, router_w, router_b, exp_w_gate, exp_w_up, exp_w_down, ln2_g, ln2_b, ple_w, ple_gate_w, ln3_g, ln3_b):
    depth = w_in.shape[0]
    alpha = (2 * depth) ** DEPTH_ALPHA_POW
    d = x_prompt.shape[-1]
    lb_all = _lb_call(hgrn_lb_logits)

    def ln_in(x):
        return _ln_call(x.reshape(-1, d), ln_in_g, ln_in_b).reshape(x.shape)

    xp, xs = ln_in(x_prompt), ln_in(x_sample)
    new_p, new_s = [], []
    for i in range(depth):
        w_k, w_v = _relayout_w_ukv(mla_w_ukv[i])
        lw = dict(
            w_in=_relayout_w_in(w_in[i]), lb=lb_all[i], hgrn_norm_g=hgrn_norm_g[i], rwkv_mu=rwkv_mu[i],
            rwkv_w0=rwkv_w0[i], rwkv_w2=rwkv_w2[i], rwkv_a0=rwkv_a0[i], rwkv_a2=rwkv_a2[i], rwkv_g2=rwkv_g2[i],
            rwkv_kk=rwkv_kk[i], rwkv_ka=rwkv_ka[i], rwkv_rk=rwkv_rk[i], rwkv_lnx_g=rwkv_lnx_g[i],
            rwkv_lnx_b=rwkv_lnx_b[i],
            mla_qn_g=jnp.pad(mla_qnorm_g[i], (0, 256 - Q_LORA)).reshape(1, 256), mla_w_uq=_relayout_w_uq(mla_w_uq[i]),
            mla_kvn_g=mla_kvnorm_g[i].reshape(1, KV_LORA), mla_w_k=w_k, mla_w_v=w_v, fox_bf=fox_bf[i],
            w_br=w_br[i].astype(BF16), w_mg=w_mg[i].astype(BF16), w_o=w_o[i].astype(BF16),
            ln1_g=ln1_g[i], ln1_b=ln1_b[i], router_w=router_w, router_b=router_b,
            w_gu=jnp.concatenate([exp_w_gate[i], exp_w_up[i]], axis=-1).astype(BF16), w_d=exp_w_down[i].astype(BF16),
            ln2_g=ln2_g[i], ln2_b=ln2_b[i], ple_w=ple_w[i].astype(BF16), ple_gate_w=ple_gate_w[i].astype(BF16),
            ln3_g=ln3_g[i], ln3_b=ln3_b[i])
        xp, st_p = _layer(xp, p_prompt[i], (None,) * 8, lw, alpha)
        new_p.append(st_p)
        st_in = (state_hgrn[i], state_rwkv[i], state_rwkv_shift[i], cache_mla_latent[i], cache_mla_krope[i],
                 cache_fox_k[i], cache_fox_v[i], cache_fox_logf[i])
        xs, st_s = _layer(xs, p_sample[i], st_in, lw, alpha)
        new_s.append(st_s)
    stack = lambda sts, j: jnp.stack([s[j] for s in sts], axis=0)
    outs_p = tuple(stack(new_p, j) for j in range(8))
    outs_s = tuple(stack(new_s, j) for j in range(8))
    return (xp, xs) + outs_p + outs_s
```

```python
import functools
import math

import numpy as np
import jax
import jax.numpy as jnp
from jax import lax
from jax.experimental import pallas as pl
from jax.experimental.pallas import tpu as pltpu

F32 = jnp.float32
BF16 = jnp.bfloat16

D_MODEL = 1024
N_HEADS = 4
HEAD_DIM = 64
BRANCH_W = N_HEADS * HEAD_DIM
CHUNK = 64
W_LORA, A_LORA, G_LORA = 32, 32, 64
NOPE, ROPE, Q_LORA, KV_LORA = 64, 32, 192, 128
ROPE_THETA = 10000.0
N_EXPERTS, N_GROUPS, EXPERT_FF = 16, 4, 256
GROUP_SIZE = N_EXPERTS // N_GROUPS
PLE_DIM = 256
A_COLS = 4 * BRANCH_W
B_COLS = 3 * BRANCH_W + W_LORA + A_LORA + G_LORA
C_COLS = Q_LORA + KV_LORA + ROPE
D_COLS = 3 * BRANCH_W + N_HEADS
DEPTH_ALPHA_POW = 0.25
LN_EPS = 1e-5
RMS_EPS = 1e-6
RWKV_GN_EPS = 64e-5
MASK_VALUE = -1e30
LOG2E = math.log2(math.e)

LANES = 128
SUBLANES = 8
VMEM_LIMIT_BYTES = 56 * 1024 * 1024

NN = ((1,), (0,))
NT = ((1,), (1,))
TN = ((0,), (0,))


def _dg(a, b, dims=NN):
    return lax.dot_general(a, b, (dims, ((), ())), preferred_element_type=F32)


def _bdot(a, b, dims=NN):
    return _dg(a.astype(BF16), b.astype(BF16), dims)


def _split(x, pieces):
    out = []
    r = x
    for i in range(pieces):
        p = r.astype(BF16)
        out.append(p)
        if i + 1 < pieces:
            r = r - p.astype(F32)
    return out


def _dot3(a, b, dims=NN):
    ah, al = _split(a, 2)
    bh, bl = _split(b, 2)
    return _dg(ah, bh, dims) + (_dg(ah, bl, dims) + _dg(al, bh, dims))


def _dotc(a, c, dims=NN, pieces=3):
    ps = _split(a, pieces)
    acc = _dg(ps[0], c, dims)
    for p in ps[1:]:
        acc = acc + _dg(p, c, dims)
    return acc


def _iota(shape, dim):
    return lax.broadcasted_iota(jnp.int32, shape, dim)


def _tri_incl(n, dtype=BF16):
    return (_iota((n, n), 0) >= _iota((n, n), 1)).astype(dtype)


def _head_indicator():
    r = _iota((BRANCH_W, BRANCH_W), 0) // HEAD_DIM
    c = _iota((BRANCH_W, BRANCH_W), 1) // HEAD_DIM
    return (r == c).astype(BF16)


def _layer_norm(x, g, b):
    mu = jnp.mean(x, axis=-1, keepdims=True)
    xc = x - mu
    var = jnp.mean(xc * xc, axis=-1, keepdims=True)
    return xc * lax.rsqrt(var + LN_EPS) * g + b


def _softplus(x):
    return jnp.maximum(x, 0.0) + jnp.log(1.0 + jnp.exp(-jnp.abs(x)))


def _silu(x):
    return x * jax.nn.sigmoid(x)


def _cparams(*sem):
    return pltpu.CompilerParams(dimension_semantics=sem, vmem_limit_bytes=VMEM_LIMIT_BYTES)


def _resident(shape):
    nd = len(shape)
    return pl.BlockSpec(shape, lambda *_: (0,) * nd)


def _token_tile(t):
    for tm in (512, 256, 128, 64, 32, 16, 8):
        if t % tm == 0:
            return tm
    raise ValueError(f"token count {t} not a multiple of 8")


def _ln_kernel(x_ref, g_ref, b_ref, o_ref):
    o_ref[...] = _layer_norm(x_ref[...], g_ref[...], b_ref[...])


def _ln_call(x, g, b):
    t, d = x.shape
    tm = _token_tile(t)
    return pl.pallas_call(
        _ln_kernel,
        grid=(t // tm,),
        in_specs=[pl.BlockSpec((tm, d), lambda i: (i, 0)), _resident((1, d)), _resident((1, d))],
        out_specs=pl.BlockSpec((tm, d), lambda i: (i, 0)),
        out_shape=jax.ShapeDtypeStruct((t, d), F32),
        compiler_params=_cparams("parallel"),
        name="ln_in",
    )(x, g.reshape(1, d), b.reshape(1, d))


_IN_SLOTS = (("a", A_COLS, A_COLS), ("b", B_COLS, B_COLS), ("qlat", Q_LORA, 256), ("kvlat", KV_LORA, 128),
             ("kr", ROPE, 128), ("fq", BRANCH_W, BRANCH_W), ("fk", BRANCH_W, BRANCH_W), ("fv", BRANCH_W, BRANCH_W),
             ("ff", N_HEADS, 128))


def _relayout_w_in(w_in):
    parts, off = [], 0
    for _, width, slot in _IN_SLOTS:
        w = w_in[:, off:off + width]
        if slot > width:
            w = jnp.pad(w, ((0, 0), (0, slot - width)))
        parts.append(w)
        off += width
    assert off == w_in.shape[1]
    return jnp.concatenate(parts, axis=1).astype(BF16)


def _in_kernel(x_ref, w_ref, *o_refs):
    xb = x_ref[...].astype(BF16)
    off = 0
    for o_ref, (_, _, slot) in zip(o_refs, _IN_SLOTS):
        o_ref[...] = _dg(xb, w_ref[:, off:off + slot])
        off += slot


def _in_call(x, w_in_p):
    t, d = x.shape
    tm = _token_tile(t)
    n = w_in_p.shape[1]
    return pl.pallas_call(
        _in_kernel,
        grid=(t // tm,),
        in_specs=[pl.BlockSpec((tm, d), lambda i: (i, 0)), _resident((d, n))],
        out_specs=[pl.BlockSpec((tm, slot), lambda i: (i, 0)) for _, _, slot in _IN_SLOTS],
        out_shape=[jax.ShapeDtypeStruct((t, slot), F32) for _, _, slot in _IN_SLOTS],
        compiler_params=_cparams("parallel"),
        name="in_proj",
    )(x, w_in_p)


def _hgrn_kernel(c_ref, lb_ref, ng_ref, s0_ref, o_ref, s_ref, st_scr, *, chunk, group):
    j = pl.program_id(1)

    @pl.when(j == 0)
    def _():
        st_scr[...] = s0_ref[...]

    for bi in range(group):
        _hgrn_chunk_one(bi, c_ref, lb_ref, ng_ref, o_ref, st_scr, chunk)

    @pl.when(j == pl.num_programs(1) - 1)
    def _():
        s_ref[...] = st_scr[...]


def _hgrn_chunk_one(bi, c_ref, lb_ref, ng_ref, o_ref, st_scr, chunk):
    c = chunk
    cols = c_ref[bi]
    w = BRANCH_W
    q, fz, iv, g = cols[:, :w], cols[:, w:2 * w], cols[:, 2 * w:3 * w], cols[:, 3 * w:]
    lb = lb_ref[...]
    f = lb + (1.0 - lb) * jax.nn.sigmoid(fz)
    k = 1.0 - f
    a = _dotc(_tri_incl(c), jnp.log(f) * LOG2E, pieces=3)
    a_last = a[c - 1:c, :]
    st = st_scr[bi]
    ind = _head_indicator()

    o_inter = _bdot(q * jnp.exp2(a), st, NT)
    kd = k * jnp.exp2(a_last - a)
    rr = _iota((w, w), 0) // HEAD_DIM
    cc = _iota((w, w), 1) // HEAD_DIM
    upd = jnp.where(rr == cc, _bdot(iv, kd, TN), 0.0)
    st_scr[bi] = st * jnp.exp2(a_last) + upd

    row = _iota((SUBLANES, w), 0)
    outs = []
    for tb in range(c // SUBLANES):
        lo = tb * SUBLANES
        q_tb = q[lo:lo + SUBLANES]
        a_tb = a[lo:lo + SUBLANES]
        pieces = []
        for s in range(lo + SUBLANES):
            d = a_tb - a[s:s + 1]
            if s >= lo:
                d = jnp.where(row >= s - lo, d, MASK_VALUE)
            pieces.append(q_tb * jnp.exp2(d) * k[s:s + 1])
        e = _dotc(jnp.concatenate(pieces, axis=0), ind, pieces=1)
        acc = e[0:SUBLANES] * iv[0:1]
        for s in range(1, lo + SUBLANES):
            acc = acc + e[s * SUBLANES:(s + 1) * SUBLANES] * iv[s:s + 1]
        outs.append(acc)
    o = o_inter + jnp.concatenate(outs, axis=0)

    ms = _dotc(o * o, ind, pieces=2) * (1.0 / HEAD_DIM)
    o = o * lax.rsqrt(ms + RMS_EPS) * ng_ref[...] * _silu(g)
    o_ref[bi] = o.astype(o_ref.dtype)


def _hgrn_call(cols_a, lb, norm_g, st0):
    b, l, _ = cols_a.shape
    c = CHUNK if l % CHUNK == 0 else l
    w = BRANCH_W
    bg = _batch_group(b)
    return pl.pallas_call(
        functools.partial(_hgrn_kernel, chunk=c, group=bg),
        grid=(b // bg, l // c),
        in_specs=[pl.BlockSpec((bg, c, A_COLS), lambda i, j: (i, j, 0)), _resident((1, w)), _resident((1, w)),
                  pl.BlockSpec((bg, w, w), lambda i, j: (i, 0, 0))],
        out_specs=[pl.BlockSpec((bg, c, w), lambda i, j: (i, j, 0)),
                   pl.BlockSpec((bg, w, w), lambda i, j: (i, 0, 0))],
        out_shape=[jax.ShapeDtypeStruct((b, l, w), BF16), jax.ShapeDtypeStruct((b, w, w), F32)],
        scratch_shapes=[pltpu.VMEM((bg, w, w), F32)],
        compiler_params=_cparams("parallel", "arbitrary"),
        name="hgrn",
    )(cols_a, lb.reshape(1, w), norm_g.reshape(1, w), st0)


def _hgrn_state_to_blockdiag(s):
    b = s.shape[0]
    out = jnp.zeros((b, N_HEADS, HEAD_DIM, N_HEADS, HEAD_DIM), F32)
    for h in range(N_HEADS):
        out = out.at[:, h, :, h, :].set(jnp.swapaxes(s[:, h], -1, -2))
    return out.reshape(b, BRANCH_W, BRANCH_W)


def _hgrn_state_from_blockdiag(st):
    hs = [st[:, h * HEAD_DIM:(h + 1) * HEAD_DIM, h * HEAD_DIM:(h + 1) * HEAD_DIM] for h in range(N_HEADS)]
    return jnp.swapaxes(jnp.stack(hs, axis=1), -1, -2)


def _rwkv_prep_kernel(c_ref, pblk_ref, prow_ref, mu_ref, w0_ref, lora_ref, a0_ref, kk_ref, ka_ref,
                      rk_ref, r_ref, lw_ref, k_ref, v_ref, a_ref, b_ref, g_ref, bonus_ref):
    j = pl.program_id(1)
    cols = c_ref[0]
    prev = jnp.where(j == 0, prow_ref[0], pblk_ref[0, SUBLANES - 1:SUBLANES, :])
    shifted = jnp.where(_iota(cols.shape, 0) == 0, prev, pltpu.roll(cols, 1, 0))
    m = cols + (shifted - cols) * mu_ref[...]
    w = BRANCH_W
    r, k, v = m[:, :w], m[:, w:2 * w], m[:, 2 * w:3 * w]
    slab = m[:, 3 * w:]
    lane = _iota(slab.shape, 1)
    act = jnp.where(lane < W_LORA, jnp.tanh(slab),
                    jnp.where(lane < W_LORA + A_LORA, slab, jax.nn.sigmoid(slab)))
    lora = _bdot(act, lora_ref[...])
    w_log = -_softplus(-(w0_ref[...] + lora[:, :w])) - 0.5
    a_rate = jax.nn.sigmoid(a0_ref[...] + lora[:, w:2 * w])
    g = lora[:, 2 * w:]
    ind = _head_indicator()
    kk = k * kk_ref[...]
    norm = jnp.sqrt(_dotc(kk * kk, ind, pieces=3))
    kk = kk / jnp.maximum(norm, 1e-12)
    kh = k * (1.0 + (a_rate - 1.0) * ka_ref[...])
    r_ref[0] = r
    lw_ref[0] = -jnp.exp(w_log)
    k_ref[0] = kh
    v_ref[0] = v
    a_ref[0] = -kk
    b_ref[0] = kk * a_rate
    g_ref[0] = g
    bonus_ref[0] = _dotc(r * kh * rk_ref[...], ind, pieces=3) * v


def _rwkv_prep_call(cols_b, prev_row, mu, w0, w2, a0, a2, g2, kk_s, ka, rk):
    b, l, nb = cols_b.shape
    tm = _token_tile(l)
    w = BRANCH_W
    row = lambda x: x.reshape(1, -1)
    lora_w = jnp.zeros((LANES, 3 * w), F32)
    lora_w = lora_w.at[:W_LORA, :w].set(w2).at[W_LORA:W_LORA + A_LORA, w:2 * w].set(a2)
    lora_w = lora_w.at[W_LORA + A_LORA:, 2 * w:].set(g2).astype(BF16)
    tok = pl.BlockSpec((1, tm, nb), lambda i, j: (i, j, 0))
    pblk =pl.BlockSpec((1, SUBLANES, nb), lambda i, j: (i, jnp.maximum(j * (tm // SUBLANES) - 1, 0), 0))
    out_spec = pl.BlockSpec((1, tm, w), lambda i, j: (i, j, 0))
    return pl.pallas_call(
        _rwkv_prep_kernel,
        grid=(b, l // tm),
        in_specs=[tok, pblk, pl.BlockSpec((1, 1, nb), lambda i, j: (i, 0, 0)), _resident((1, nb)),
                  _resident((1, w)), _resident((LANES, 3 * w)), _resident((1, w)),
                  _resident((1, w)), _resident((1, w)), _resident((1, w))],
        out_specs=[out_spec] * 8,
        out_shape=[jax.ShapeDtypeStruct((b, l, w), F32)] * 8,
        compiler_params=_cparams("parallel", "parallel"),
        name="rwkv_prep",
    )(cols_b, cols_b, prev_row.reshape(b, 1, nb), row(mu), row(w0), lora_w, row(a0), row(kk_s), row(ka), row(rk))


def _rwkv_chunk_kernel(r_ref, lw_ref, k_ref, v_ref, a_ref, b_ref, g_ref, bonus_ref, lg_ref, lb_ref, s0_ref,
                       o_ref, s_ref, st_scr, *, chunk, group):
    j = pl.program_id(1)
    hd = HEAD_DIM

    @pl.when(j == 0)
    def _():
        for bi in range(group):
            for h in range(N_HEADS):
                st_scr[bi, :, h * hd:(h + 1) * hd] = s0_ref[bi, h]

    for bi in range(group):
        _rwkv_chunk_one(bi, r_ref, lw_ref, k_ref, v_ref, a_ref, b_ref, g_ref, bonus_ref, lg_ref, lb_ref,
                        o_ref, st_scr, chunk)

    @pl.when(j == pl.num_programs(1) - 1)
    def _():
        for bi in range(group):
            for h in range(N_HEADS):
                s_ref[bi, h] = st_scr[bi, :, h * hd:(h + 1) * hd]


def _rwkv_chunk_one(bi, r_ref, lw_ref, k_ref, v_ref, a_ref, b_ref, g_ref, bonus_ref, lg_ref, lb_ref, o_ref, st_scr,
                    chunk):
    c = chunk
    w = BRANCH_W
    hd = HEAD_DIM
    nh = N_HEADS
    r, lw, k, v, a, b = r_ref[bi], lw_ref[bi], k_ref[bi], v_ref[bi], a_ref[bi], b_ref[bi]
    cum = _dotc(_tri_incl(c), lw, pieces=3)
    last = cum[c - 1:c, :]
    at = a * jnp.exp(cum - lw)
    rt = r * jnp.exp(cum)
    e_neg = jnp.exp(-cum)
    bt = b * e_neg
    kt = k * e_neg
    e_last = jnp.exp(last - cum)
    bh = b * e_last
    kh = k * e_last

    lane_head = _iota((c, w), 1) // hd

    def head_rows(x):
        return jnp.concatenate([jnp.where(lane_head == h, x, 0.0) for h in range(nh)], axis=0)

    def head_stack(x):
        return jnp.concatenate([x[:, h * hd:(h + 1) * hd] for h in range(nh)], axis=0)

    n = nh * c
    lhs = jnp.concatenate([head_rows(at), head_rows(rt)], axis=0)
    rhs = jnp.concatenate([bt] * nh + [kt] * nh, axis=0)
    pair = _dot3(lhs, rhs, NT)

    rown = _iota((n, n), 0)
    coln = _iota((n, n), 1)
    same_head = (rown // c) == (coln // c)
    strict = same_head & ((rown % c) > (coln % c))
    incl = same_head & ((rown % c) >= (coln % c))
    l_ab = jnp.where(strict, pair[:n, :n], 0.0)
    l_ak = jnp.where(strict, pair[:n, n:], 0.0)
    a_rb = jnp.where(incl, pair[n:, :n], 0.0)
    a_rk = jnp.where(incl, pair[n:, n:], 0.0)

    eye = (rown == coln).astype(F32)
    tinv = eye + l_ab
    p = l_ab
    for _ in range(int(math.log2(c)) - 1):
        p = _dot3(p, p)
        tinv = tinv + _dot3(tinv, p)

    st = st_scr[bi]
    sh = _dot3(lhs, st, NT)
    vs = head_stack(v)
    u = _dot3(tinv, sh[:n] + _dot3(l_ak, vs))
    y = sh[n:] + _dot3(a_rb, u) + _dot3(a_rk, vs)
    upd = _dot3(jnp.concatenate([u, vs], axis=0),
                jnp.concatenate([head_rows(bh), head_rows(kh)], axis=0), TN)
    st_scr[bi] = st * jnp.exp(last) + upd

    for h in range(nh):
        yh = y[h * c:(h + 1) * c]
        sl = slice(h * hd, (h + 1) * hd)
        mu = jnp.mean(yh, axis=-1, keepdims=True)
        yc = yh - mu
        var = jnp.mean(yc * yc, axis=-1, keepdims=True)
        yn = yc * lax.rsqrt(var + RWKV_GN_EPS) * lg_ref[:, sl] + lb_ref[:, sl]
        o_ref[bi, :, sl] = ((yn + bonus_ref[bi, :, sl]) * g_ref[bi, :, sl]).astype(o_ref.dtype)


def _batch_group(b):
    return 4 if b % 4 == 0 else 1


def _rwkv_chunk_call(prep, lnx_g, lnx_b, s0):
    r = prep[0]
    b, l, w = r.shape
    c = CHUNK if l % CHUNK == 0 else l
    bg = _batch_group(b)
    tok = pl.BlockSpec((bg, c, w), lambda i, j: (i, j, 0))
    st_spec = pl.BlockSpec((bg, N_HEADS, HEAD_DIM, HEAD_DIM), lambda i, j: (i, 0, 0, 0))
    return pl.pallas_call(
        functools.partial(_rwkv_chunk_kernel, chunk=c, group=bg),
        grid=(b // bg, l // c),
        in_specs=[tok] * 8 + [_resident((1, w)), _resident((1, w)), st_spec],
        out_specs=[tok, st_spec],
        out_shape=[jax.ShapeDtypeStruct((b, l, w), BF16),
                   jax.ShapeDtypeStruct((b, N_HEADS, HEAD_DIM, HEAD_DIM), F32)],
        scratch_shapes=[pltpu.VMEM((bg, HEAD_DIM, w), F32)],
        compiler_params=_cparams("parallel", "arbitrary"),
        name="rwkv_chunk",
    )(*prep, lnx_g.reshape(1, w), lnx_b.reshape(1, w), s0)


def _rope_tables(p, l):
    half = ROPE // 2
    inv = 1.0 / (ROPE_THETA ** (jnp.arange(half, dtype=F32) / half))
    ang = (p + jnp.arange(l, dtype=jnp.int32)).astype(F32)[:, None] * inv[None, :]
    cos, sin = jnp.cos(ang), jnp.sin(ang)
    pad = jnp.zeros((l, LANES - ROPE), F32)
    return jnp.concatenate([cos, cos, pad], axis=1), jnp.concatenate([-sin, sin, pad], axis=1)


def _swap_halves(x, base):
    half = ROPE // 2
    n = x.shape[-1]
    lane = _iota(x.shape, x.ndim - 1) % LANES
    up = pltpu.roll(x, n - half, x.ndim - 1)
    down = pltpu.roll(x, half, x.ndim - 1)
    return jnp.where((lane >= base) & (lane < base + half), up,
                     jnp.where((lane >= base + half) & (lane < base + ROPE), down, 0.0))


def _rms_norm(x, g, width):
    ms = jnp.sum(x * x, axis=-1, keepdims=True) * (1.0 / width)
    return x * lax.rsqrt(ms + RMS_EPS) * g


def _mla_q_kernel(ql_ref, kvl_ref, kr_ref, cos_ref, sin_ref, qg_ref, wuq_ref, kvg_ref, q_ref, c_ref, krn_ref):
    scale = (NOPE + ROPE) ** -0.5 * LOG2E
    qn = _rms_norm(ql_ref[0], qg_ref[...], Q_LORA)
    q = _bdot(qn, wuq_ref[...])
    cos, sin = cos_ref[...], sin_ref[...]
    lane = _iota(cos.shape, 1)
    cos_q = jnp.where(lane < NOPE, 1.0, pltpu.roll(cos, NOPE, 1))
    sin_q = pltpu.roll(sin, NOPE, 1)
    for h in range(N_HEADS):
        qh = q[:, h * LANES:(h + 1) * LANES]
        qh = qh * cos_q + _swap_halves(qh, NOPE) * sin_q
        q_ref[0, h] = (qh * scale).astype(q_ref.dtype)
    c_ref[0] = _rms_norm(kvl_ref[0], kvg_ref[...], KV_LORA)
    kr = kr_ref[0]
    krn = kr * cos + _swap_halves(kr, 0) * sin
    krn_ref[0] = krn[:, :ROPE]


def _mla_q_call(qlat, kvlat, kr, cos, sin, qn_g, w_uq_p, kvn_g):
    b, l, _ = qlat.shape
    tm = _token_tile(l)
    tok = lambda n: pl.BlockSpec((1, tm, n), lambda i, j: (i, j, 0))
    tab = pl.BlockSpec((tm, LANES), lambda i, j: (j, 0))
    return pl.pallas_call(
        _mla_q_kernel,
        grid=(b, l // tm),
        in_specs=[tok(256), tok(KV_LORA), tok(LANES), tab, tab, _resident((1, 256)),
                  _resident((256, N_HEADS * LANES)), _resident((1, KV_LORA))],
        out_specs=[pl.BlockSpec((1, N_HEADS, tm, LANES), lambda i, j: (i, 0, j, 0)), tok(KV_LORA), tok(ROPE)],
        out_shape=[jax.ShapeDtypeStruct((b, N_HEADS, l, LANES), BF16), jax.ShapeDtypeStruct((b, l, KV_LORA), F32),
                   jax.ShapeDtypeStruct((b, l, ROPE), F32)],
        compiler_params=_cparams("parallel", "parallel"),
        name="mla_q",
    )(qlat, kvlat, kr, cos, sin, qn_g, w_uq_p, kvn_g)


def _mla_kv_kernel(c_ref, kr_ref, wk_ref, wv_ref, k_ref, v_ref):
    cb = c_ref[0].astype(BF16)
    kall = _dg(cb, wk_ref[...])
    vall = _dg(cb, wv_ref[...])
    tm = cb.shape[0]
    place = (_iota((ROPE, LANES), 0) + NOPE == _iota((ROPE, LANES), 1)).astype(BF16)
    kr_slot = _dg(kr_ref[0].astype(BF16), place)
    one_slot = (_iota((tm, LANES), 1) == HEAD_DIM).astype(F32)
    for h in range(N_HEADS):
        k_ref[0, h] = (kall[:, h * LANES:(h + 1) * LANES] + kr_slot).astype(k_ref.dtype)
        v_ref[0, h, 0] = (vall[:, h * LANES:(h + 1) * LANES] + one_slot).T.astype(v_ref.dtype)


def _kv_specs(b, t, tm):
    kspec = pl.BlockSpec((1, N_HEADS, tm, LANES), lambda i, j: (i, 0, j, 0))
    vspec = pl.BlockSpec((1, N_HEADS, 1, LANES, tm), lambda i, j: (i, 0, j, 0, 0))
    shapes = [jax.ShapeDtypeStruct((b, N_HEADS, t, LANES), BF16),
              jax.ShapeDtypeStruct((b, N_HEADS, t // tm, LANES, tm), BF16)]
    return [kspec, vspec], shapes


def _mla_kv_call(c_all, kr_all, w_k_p, w_v_p):
    b, t, _ = c_all.shape
    tm = _token_tile(t)
    tok = lambda n: pl.BlockSpec((1, tm, n), lambda i, j: (i, j, 0))
    out_specs, out_shape = _kv_specs(b, t, tm)
    return pl.pallas_call(
        _mla_kv_kernel,
        grid=(b, t // tm),
        in_specs=[tok(KV_LORA), tok(ROPE), _resident((KV_LORA, N_HEADS * LANES)),
                  _resident((KV_LORA, N_HEADS * LANES))],
        out_specs=out_specs,
        out_shape=out_shape,
        compiler_params=_cparams("parallel", "parallel"),
        name="mla_kv",
    )(c_all, kr_all, w_k_p, w_v_p)


def _fox_lf_kernel(ff_ref, bf_ref, lf_ref, lfw_ref):
    z = ff_ref[0] + bf_ref[...]
    lf = jnp.where(_iota(z.shape, 1) < N_HEADS, -_softplus(-z), 0.0)
    lfw_ref[0] = lf
    lf_ref[0] = lf[:, :N_HEADS]


def _fox_lf_call(ff, bf):
    b, l, _ = ff.shape
    tm = _token_tile(l)
    return pl.pallas_call(
        _fox_lf_kernel,
        grid=(b, l // tm),
        in_specs=[pl.BlockSpec((1, tm, LANES), lambda i, j: (i, j, 0)), _resident((1, LANES))],
        out_specs=[pl.BlockSpec((1, tm, N_HEADS), lambda i, j: (i, j, 0)),
                   pl.BlockSpec((1, tm, LANES), lambda i, j: (i, j, 0))],
        out_shape=[jax.ShapeDtypeStruct((b, l, N_HEADS), F32), jax.ShapeDtypeStruct((b, l, LANES), F32)],
        compiler_params=_cparams("parallel", "parallel"),
        name="fox_lf",
    )(ff, jnp.pad(bf, (0, LANES - N_HEADS)).reshape(1, LANES))


def _cumsum_kernel(x_ref, o_ref, carry):
    @pl.when(pl.program_id(1) == 0)
    def _():
        carry[...] = jnp.zeros_like(carry)

    x = x_ref[0]
    c = _dotc(_tri_incl(x.shape[0]), x, pieces=3) + carry[...]
    o_ref[0] = c
    carry[...] = c[x.shape[0] - 1:, :]


def _cumsum_call(x):
    b, t, n = x.shape
    tm = _token_tile(t)
    spec = pl.BlockSpec((1, tm, n), lambda i, j: (i, j, 0))
    return pl.pallas_call(
        _cumsum_kernel,
        grid=(b, t // tm),
        in_specs=[spec],
        out_specs=spec,
        out_shape=jax.ShapeDtypeStruct((b, t, n), F32),
        scratch_shapes=[pltpu.VMEM((1, n), F32)],
        compiler_params=_cparams("parallel", "arbitrary"),
        name="fox_cumsum",
    )(x)


def _head_slot(x, h):
    pair = x[:, (h // 2) * LANES:(h // 2 + 1) * LANES]
    return pair if h % 2 == 0 else pltpu.roll(pair, HEAD_DIM, 1)


def _bias_lanes(c_col, first, ones_first):
    tm = c_col.shape[0]
    lane = _iota((tm, LANES), 1)
    p0 = c_col.astype(BF16).astype(F32)
    r1 = c_col - p0
    p1 = r1.astype(BF16).astype(F32)
    p2 = r1 - p1
    out = jnp.where(lane == first, p0, jnp.where(lane == first + 1, p1, jnp.where(lane == first + 2, p2, 0.0)))
    return jnp.where((lane >= ones_first) & (lane < ones_first + 3), 1.0, out)


def _fox_q_kernel(q_ref, c_ref, o_ref):
    scale = HEAD_DIM ** -0.5 * LOG2E
    q = q_ref[0]
    c = c_ref[0] * LOG2E
    lane = _iota((q.shape[0], LANES), 1)
    for h in range(N_HEADS):
        bias = _bias_lanes(c[:, h:h + 1], HEAD_DIM, HEAD_DIM + 3)
        o_ref[0, h] = jnp.where(lane < HEAD_DIM, _head_slot(q, h) * scale, bias).astype(o_ref.dtype)


def _fox_kv_kernel(k_ref, v_ref, c_ref, ko_ref, vo_ref):
    k, v, c = k_ref[0], v_ref[0], c_ref[0] * LOG2E
    lane = _iota((k.shape[0], LANES), 1)
    for h in range(N_HEADS):
        bias = _bias_lanes(-c[:, h:h + 1], HEAD_DIM + 3, HEAD_DIM)
        ko_ref[0, h] = jnp.where(lane < HEAD_DIM, _head_slot(k, h), bias).astype(ko_ref.dtype)
        vo_ref[0, h, 0] = jnp.where(lane < HEAD_DIM, _head_slot(v, h),
                                    (lane == HEAD_DIM).astype(F32)).T.astype(vo_ref.dtype)


def _fox_q_call(q, c_new):
    b, l, w = q.shape
    tm = _token_tile(l)
    return pl.pallas_call(
        _fox_q_kernel,
        grid=(b, l // tm),
        in_specs=[pl.BlockSpec((1, tm, w), lambda i, j: (i, j, 0)),
                  pl.BlockSpec((1, tm, LANES), lambda i, j: (i, j, 0))],
        out_specs=pl.BlockSpec((1, N_HEADS, tm, LANES), lambda i, j: (i, 0, j, 0)),
        out_shape=jax.ShapeDtypeStruct((b, N_HEADS, l, LANES), BF16),
        compiler_params=_cparams("parallel", "parallel"),
        name="fox_q",
    )(q, c_new)


def _fox_kv_call(k_all, v_all, c_all):
    b, t, w = k_all.shape
    tm = _token_tile(t)
    tok = pl.BlockSpec((1, tm, w), lambda i, j: (i, j, 0))
    out_specs, out_shape = _kv_specs(b, t, tm)
    return pl.pallas_call(
        _fox_kv_kernel,
        grid=(b, t // tm),
        in_specs=[tok, tok, pl.BlockSpec((1, tm, LANES), lambda i, j: (i, j, 0))],
        out_specs=out_specs,
        out_shape=out_shape,
        compiler_params=_cparams("parallel", "parallel"),
        name="fox_kv",
    )(k_all, v_all, c_all)


def _flash_kernel(q_ref, k_ref, v_ref, o_ref, m_scr, acc_scr, s_scr, *, tq, tk, past, t_valid, chunk_mask):
    qi = pl.program_id(1)
    q_start = past + qi * tq
    if chunk_mask:
        vis_end = jnp.minimum(((q_start + tq + CHUNK - 1) // CHUNK) * CHUNK, t_valid)
    else:
        vis_end = q_start + tq
    n_blocks = (vis_end + tk - 1) // tk
    n_full = q_start // tk

    m_scr[...] = jnp.full(m_scr.shape, MASK_VALUE, F32)
    acc_scr[...] = jnp.zeros(acc_scr.shape, F32)

    def logits(h, kb):
        ks = pl.multiple_of(kb * tk, tk)
        return _dg(k_ref[0, h, pl.ds(ks, tk), :], q_ref[0, h], NT)

    s_scr[...] = logits(0, 0)

    def block(kb, masked):
        if masked:
            kpos = kb * tk + _iota((tk, tq), 0)
            qpos = q_start + _iota((tk, tq), 1)
            if chunk_mask:
                keep = ((kpos // CHUNK) <= (qpos // CHUNK)) & (kpos < t_valid)
            else:
                keep = kpos <= qpos
        s_next = s_scr[...]
        for h in range(N_HEADS):
            s = jnp.where(keep, s_next, MASK_VALUE) if masked else s_next
            if h + 1 < N_HEADS:
                s_next = logits(h + 1, kb)
            else:
                s_next = logits(0, jnp.minimum(kb + 1, n_blocks - 1))
            m_old = m_scr[h]
            m_new = jnp.maximum(m_old, jnp.max(s, axis=0, keepdims=True))
            p = jnp.exp2(s - m_new[:1])
            pv = _dg(v_ref[0, h, kb], p.astype(BF16))
            acc_scr[h] = acc_scr[h] * jnp.exp2(m_old - m_new)[:1] + pv
            m_scr[h] = m_new
        s_scr[...] = s_next

    def full_body(kb, carry):
        block(kb, False)
        return carry

    def masked_body(kb, carry):
        block(kb, True)
        return carry

    lax.fori_loop(0, n_full, full_body, 0)
    lax.fori_loop(n_full, n_blocks, masked_body, 0)
    for h in range(N_HEADS):
        acc = acc_scr[h].T
        o = acc[:, :HEAD_DIM] / acc[:, HEAD_DIM:HEAD_DIM + 1]
        o_ref[0, :, h * HEAD_DIM:(h + 1) * HEAD_DIM] = o.astype(o_ref.dtype)


def _flash_call(q, k, v, *, past, t_valid, chunk_mask, name):
    b, nh, l, _ = q.shape
    t = k.shape[2]
    tk = v.shape[-1]
    lq = max(l, LANES)
    if lq != l:
        q = jnp.pad(q, ((0, 0), (0, 0), (0, lq - l), (0, 0)))
    tq = _token_tile(lq)
    out = pl.pallas_call(
        functools.partial(_flash_kernel, tq=tq, tk=tk, past=past, t_valid=t_valid, chunk_mask=chunk_mask),
        grid=(b, lq // tq),
        in_specs=[pl.BlockSpec((1, nh, tq, LANES), lambda i, j: (i, 0, j, 0)),
                  pl.BlockSpec((1, nh, t, LANES), lambda i, j: (i, 0, 0, 0)),
                  pl.BlockSpec((1, nh, t // tk, LANES, tk), lambda i, j: (i, 0, 0, 0, 0))],
        out_specs=pl.BlockSpec((1, tq, BRANCH_W), lambda i, j: (i, j, 0)),
        out_shape=jax.ShapeDtypeStruct((b, lq, BRANCH_W), BF16),
        scratch_shapes=[pltpu.VMEM((nh, SUBLANES, tq), F32), pltpu.VMEM((nh, LANES, tq), F32),
                        pltpu.VMEM((tk, tq), F32)],
        compiler_params=_cparams("parallel", "arbitrary"),
        name=name,
    )(q, k, v)
    return out[:, :l]


def _mix_kernel(x_ref, oa_ref, ob_ref, oc_ref, od_ref, wmg_ref, wbr_ref, wo_ref, g_ref, b_ref, o_ref, *, alpha):
    x = x_ref[...]
    xb = x.astype(BF16)
    d = D_MODEL
    acc = None
    for n, br_ref in enumerate((oa_ref, ob_ref, oc_ref, od_ref)):
        gate = jax.nn.sigmoid(_dg(xb, wmg_ref[:, n * d:(n + 1) * d]))
        term = gate * _dg(br_ref[...], wbr_ref[n])
        acc = term if acc is None else acc + term
    mix = _dg(acc.astype(BF16), wo_ref[...])
    o_ref[...] = _layer_norm(alpha * x + mix, g_ref[...], b_ref[...])


def _mix_call(x, oa, ob, oc, od, w_mg, w_br, w_o, g, b, alpha):
    t, d = x.shape
    tm = _token_tile(t)
    tok = lambda n: pl.BlockSpec((tm, n), lambda i: (i, 0))
    return pl.pallas_call(
        functools.partial(_mix_kernel, alpha=alpha),
        grid=(t // tm,),
        in_specs=[tok(d)] + [tok(BRANCH_W)] * 4 + [_resident(w_mg.shape), _resident(w_br.shape),
                                                   _resident(w_o.shape), _resident((1, d)), _resident((1, d))],
        out_specs=tok(d),
        out_shape=jax.ShapeDtypeStruct((t, d), F32),
        compiler_params=_cparams("parallel"),
        name="mix_ln1",
    )(x, oa, ob, oc, od, w_mg, w_br, w_o, g.reshape(1, d), b.reshape(1, d))


def _route(x, rw_ref, rb_ref):
    scores = jax.nn.sigmoid(_dot3(x, rw_ref[...]))
    biased = scores + rb_ref[...]
    col = [biased[:, e:e + 1] for e in range(N_EXPERTS)]
    gs = []
    for g in range(N_GROUPS):
        v = col[g * GROUP_SIZE:(g + 1) * GROUP_SIZE]
        best = None
        for i in range(GROUP_SIZE):
            for j in range(i + 1, GROUP_SIZE):
                s = v[i] + v[j]
                best = s if best is None else jnp.maximum(best, s)
        gs.append(best)
    gates = []
    for g in range(N_GROUPS):
        sel = None
        for o in range(N_GROUPS):
            if o == g:
                continue
            cond = (gs[g] > gs[o]) if o < g else (gs[g] >= gs[o])
            sel = cond if sel is None else sel & cond
        v = col[g * GROUP_SIZE:(g + 1) * GROUP_SIZE]
        for i in range(GROUP_SIZE):
            rank = None
            for j in range(GROUP_SIZE):
                if j == i:
                    continue
                ahead = (v[j] >= v[i]) if j < i else (v[j] > v[i])
                ahead = ahead.astype(F32)
                rank = ahead if rank is None else rank + ahead
            e = g * GROUP_SIZE + i
            gates.append(jnp.where(sel & (rank < 2.0), scores[:, e:e + 1], 0.0))
    total = gates[0]
    for gt in gates[1:]:
        total = total + gt
    return [gt / total for gt in gates]


def _moe_kernel(x_ref, rw_ref, rb_ref, wgu_ref, wd_ref, g_ref, b_ref, o_ref, acc_scr, gate_scr, *, alpha):
    e = pl.program_id(1)

    @pl.when(e == 0)
    def _():
        gates = _route(x_ref[...], rw_ref, rb_ref)
        lane = _iota(gate_scr.shape, 1)
        gm = jnp.zeros(gate_scr.shape, F32)
        for i, gt in enumerate(gates):
            gm = jnp.where(lane == i, gt, gm)
        gate_scr[...] = gm
        acc_scr[...] = jnp.zeros(acc_scr.shape, F32)

    xb = x_ref[...].astype(BF16)
    hgu = _dg(xb, wgu_ref[0])
    gm = gate_scr[...]
    gate = jnp.sum(jnp.where(_iota(gm.shape, 1) == e, gm, 0.0), axis=-1, keepdims=True)
    h = _silu(hgu[:, :EXPERT_FF]) * hgu[:, EXPERT_FF:]
    acc_scr[...] += gate * _dg(h.astype(BF16), wd_ref[0])

    @pl.when(e == pl.num_programs(1) - 1)
    def _():
        o_ref[...] = _layer_norm(alpha * x_ref[...] + acc_scr[...], g_ref[...], b_ref[...])


def _moe_call(x, router_w, router_b, w_gu, w_d, g, b, alpha):
    t, d = x.shape
    tm = _token_tile(t)
    tok = pl.BlockSpec((tm, d), lambda i, e: (i, 0))
    return pl.pallas_call(
        functools.partial(_moe_kernel, alpha=alpha),
        grid=(t // tm, N_EXPERTS),
        in_specs=[tok, _resident((d, LANES)), _resident((1, LANES)),
                  pl.BlockSpec((1, d, 2 * EXPERT_FF), lambda i, e: (e, 0, 0)),
                  pl.BlockSpec((1, EXPERT_FF, d), lambda i, e: (e, 0, 0)), _resident((1, d)), _resident((1, d))],
        out_specs=tok,
        out_shape=jax.ShapeDtypeStruct((t, d), F32),
        scratch_shapes=[pltpu.VMEM((tm, d), F32), pltpu.VMEM((tm, LANES), F32)],
        compiler_params=_cparams("parallel", "arbitrary"),
        name="moe_ln2",
    )(x, jnp.pad(router_w, ((0, 0), (0, LANES - N_EXPERTS))),
      jnp.pad(router_b, (0, LANES - N_EXPERTS)).reshape(1, LANES), w_gu, w_d, g.reshape(1, d), b.reshape(1, d))


def _ple_kernel(x_ref, p_ref, pw_ref, gw_ref, g_ref, b_ref, o_ref, *, alpha):
    x = x_ref[...]
    ple = _bdot(p_ref[...], pw_ref[...]) * jax.nn.sigmoid(_bdot(x, gw_ref[...]))
    o_ref[...] = _layer_norm(alpha * x + ple, g_ref[...], b_ref[...])


def _ple_call(x, p, ple_w, gate_w, g, b, alpha):
    t, d = x.shape
    tm = _token_tile(t)
    tok = lambda n: pl.BlockSpec((tm, n), lambda i: (i, 0))
    return pl.pallas_call(
        functools.partial(_ple_kernel, alpha=alpha),
        grid=(t // tm,),
        in_specs=[tok(d), tok(PLE_DIM), _resident(ple_w.shape), _resident(gate_w.shape), _resident((1, d)),
                  _resident((1, d))],
        out_specs=tok(d),
        out_shape=jax.ShapeDtypeStruct((t, d), F32),
        compiler_params=_cparams("parallel"),
        name="ple_ln3",
    )(x, p, ple_w, gate_w, g.reshape(1, d), b.reshape(1, d))


def _lb_kernel(x_ref, o_ref):
    x = x_ref[...]
    depth = x.shape[0]
    m = jnp.max(x, axis=0, keepdims=True)
    e = jnp.exp(x - m)
    pr = e / jnp.sum(e, axis=0, keepdims=True)
    run = jnp.zeros_like(pr[0:1])
    for i in range(depth):
        o_ref[i:i + 1, :] = run
        run = run + pr[i:i + 1]


def _lb_call(logits):
    return pl.pallas_call(
        _lb_kernel,
        out_shape=jax.ShapeDtypeStruct(logits.shape, F32),
        name="hgrn_lb",
    )(logits)


def _pad_rows(x, t):
    pad = t - x.shape[1]
    if pad == 0:
        return x
    return jnp.pad(x, ((0, 0), (0, pad)) + ((0, 0),) * (x.ndim - 2))


def _layer(x, p, st, lw, alpha):
    b, l, d = x.shape
    t = b * l
    hgrn_s, rwkv_s, rwkv_prev, lat_past, kr_past, fk_past, fv_past, flf_past = st
    past = 0 if lat_past is None else lat_past.shape[1]
    t_valid = past + l
    t_pad = -(-t_valid // LANES) * LANES if past else t_valid

    cols = _in_call(x.reshape(t, d), lw["w_in"])
    ca, cb, qlat, kvlat, kr, fq, fk, fv, ff = [c.reshape(b, l, -1) for c in cols]

    st0 = jnp.zeros((b, BRANCH_W, BRANCH_W), F32) if hgrn_s is None else _hgrn_state_to_blockdiag(hgrn_s)
    o_a, hgrn_bd = _hgrn_call(ca, lw["lb"], lw["hgrn_norm_g"], st0)
    hgrn_new = _hgrn_state_from_blockdiag(hgrn_bd)

    prev_row = jnp.zeros((b, B_COLS), F32) if rwkv_prev is None else rwkv_prev
    s0 = jnp.zeros((b, N_HEADS, HEAD_DIM, HEAD_DIM), F32) if rwkv_s is None else rwkv_s
    prep = _rwkv_prep_call(cb, prev_row, lw["rwkv_mu"], lw["rwkv_w0"], lw["rwkv_w2"], lw["rwkv_a0"], lw["rwkv_a2"],
                           lw["rwkv_g2"], lw["rwkv_kk"], lw["rwkv_ka"], lw["rwkv_rk"])
    o_b, rwkv_new = _rwkv_chunk_call(prep, lw["rwkv_lnx_g"], lw["rwkv_lnx_b"], s0)
    shift_new = cb[:, l - 1]

    cos, sin = _rope_tables(past, l)
    q_c, lat_new, kr_new = _mla_q_call(qlat, kvlat, kr, cos, sin, lw["mla_qn_g"], lw["mla_w_uq"], lw["mla_kvn_g"])
    if past:
        c_all = _pad_rows(jnp.concatenate([lat_past, lat_new], axis=1), t_pad)
        kr_all = _pad_rows(jnp.concatenate([kr_past, kr_new], axis=1), t_pad)
    else:
        c_all, kr_all = lat_new, kr_new
    k_c, v_c = _mla_kv_call(c_all, kr_all, lw["mla_w_k"], lw["mla_w_v"])
    o_c = _flash_call(q_c, k_c, v_c, past=past, t_valid=t_valid, chunk_mask=True, name="mla_attn")

    lf_new, lf_wide = _fox_lf_call(ff, lw["fox_bf"])
    if past:
        flf_wide = jnp.pad(flf_past, ((0, 0), (0, 0), (0, LANES - N_HEADS)))
        lf_all = _pad_rows(jnp.concatenate([flf_wide, lf_wide], axis=1), t_pad)
        k_all = _pad_rows(jnp.concatenate([fk_past.reshape(b, past, BRANCH_W), fk], axis=1), t_pad)
        v_all = _pad_rows(jnp.concatenate([fv_past.reshape(b, past, BRANCH_W), fv], axis=1), t_pad)
    else:
        lf_all, k_all, v_all = lf_wide, fk, fv
    c_all_f = _cumsum_call(lf_all)
    q_d = _fox_q_call(fq, c_all_f[:, past:past + l])
    k_d, v_d = _fox_kv_call(k_all, v_all, c_all_f)
    o_d = _flash_call(q_d, k_d, v_d, past=past, t_valid=t_valid, chunk_mask=False, name="fox_attn")

    flat = lambda o: o.reshape(t, BRANCH_W)
    x1 = _mix_call(x.reshape(t, d), flat(o_a), flat(o_b), flat(o_c), flat(o_d), lw["w_mg"], lw["w_br"], lw["w_o"],
                   lw["ln1_g"], lw["ln1_b"], alpha)
    x2 = _moe_call(x1, lw["router_w"], lw["router_b"], lw["w_gu"], lw["w_d"], lw["ln2_g"], lw["ln2_b"], alpha)
    x3 = _ple_call(x2, p.reshape(t, PLE_DIM), lw["ple_w"], lw["ple_gate_w"], lw["ln3_g"], lw["ln3_b"], alpha)
    new = (hgrn_new, rwkv_new, shift_new, lat_new, kr_new, fk.reshape(b, l, N_HEADS, HEAD_DIM),
           fv.reshape(b, l, N_HEADS, HEAD_DIM), lf_new)
    return x3.reshape(b, l, d), new


def _relayout_w_uq(w_uq):
    w = w_uq.reshape(Q_LORA, N_HEADS, NOPE + ROPE)
    w = jnp.pad(w, ((0, 256 - Q_LORA), (0, 0), (0, LANES - NOPE - ROPE)))
    return w.reshape(256, N_HEADS * LANES).astype(BF16)


def _relayout_w_ukv(w_ukv):
    w = w_ukv.reshape(KV_LORA, N_HEADS, NOPE + HEAD_DIM)
    pad = lambda x: jnp.pad(x, ((0, 0), (0, 0), (0, LANES - x.shape[-1]))).reshape(KV_LORA, N_HEADS * LANES)
    return pad(w[..., :NOPE]).astype(BF16), pad(w[..., NOPE:]).astype(BF16)


def kernel(x_prompt, x_sample, state_hgrn, state_rwkv, state_rwkv_shift, cache_mla_latent, cache_mla_krope, cache_fox_k, cache_fox_v, cache_fox_logf, p_prompt, p_sample, ln_in_g, ln_in_b, w_in, hgrn_lb_logits, hgrn_norm_g, rwkv_mu, rwkv_w0, rwkv_w2, rwkv_a0, rwkv_a2, rwkv_g2, rwkv_kk, rwkv_ka, rwkv_rk, rwkv_lnx_g, rwkv_lnx_b, mla_qnorm_g, mla_w_uq, mla_kvnorm_g, mla_w_ukv, fox_bf, w_br, w_mg, w_o, ln1_g, ln1_b, router_w, router_b, exp_w_gate, exp_w_up, exp_w_down, ln2_g, ln2_b, ple_w, ple_gate_w, ln3_g, ln3_b):
    depth = w_in.shape[0]
    alpha = (2 * depth) ** DEPTH_ALPHA_POW
    d = x_prompt.shape[-1]
    lb_all = _lb_call(hgrn_lb_logits)

    def ln_in(x):
        return _ln_call(x.reshape(-1, d), ln_in_g, ln_in_b).reshape(x.shape)

    xp, xs = ln_in(x_prompt), ln_in(x_sample)
    new_p, new_s = [], []
    for i in range(depth):
        w_k, w_v = _relayout_w_ukv(mla_w_ukv[i])
        lw = dict(
            w_in=_relayout_w_in(w_in[i]), lb=lb_all[i], hgrn_norm_g=hgrn_norm_g[i], rwkv_mu=rwkv_mu[i],
            rwkv_w0=rwkv_w0[i], rwkv_w2=rwkv_w2[i], rwkv_a0=rwkv_a0[i], rwkv_a2=rwkv_a2[i], rwkv_g2=rwkv_g2[i],
            rwkv_kk=rwkv_kk[i], rwkv_ka=rwkv_ka[i], rwkv_rk=rwkv_rk[i], rwkv_lnx_g=rwkv_lnx_g[i],
            rwkv_lnx_b=rwkv_lnx_b[i],
            mla_qn_g=jnp.pad(mla_qnorm_g[i], (0, 256 - Q_LORA)).reshape(1, 256), mla_w_uq=_relayout_w_uq(mla_w_uq[i]),
            mla_kvn_g=mla_kvnorm_g[i].reshape(1, KV_LORA), mla_w_k=w_k, mla_w_v=w_v, fox_bf=fox_bf[i],
            w_br=w_br[i].astype(BF16), w_mg=w_mg[i].astype(BF16), w_o=w_o[i].astype(BF16),
            ln1_g=ln1_g[i], ln1_b=ln1_b[i], router_w=router_w, router_b=router_b,
            w_gu=jnp.concatenate([exp_w_gate[i], exp_w_up[i]], axis=-1).astype(BF16), w_d=exp_w_down[i].astype(BF16),
            ln2_g=ln2_g[i], ln2_b=ln2_b[i], ple_w=ple_w[i].astype(BF16), ple_gate_w=ple_gate_w[i].astype(BF16),
            ln3_g=ln3_g[i], ln3_b=ln3_b[i])
        xp, st_p = _layer(xp, p_prompt[i], (None,) * 8, lw, alpha)
        new_p.append(st_p)
        st_in = (state_hgrn[i], state_rwkv[i], state_rwkv_shift[i], cache_mla_latent[i], cache_mla_krope[i],
                 cache_fox_k[i], cache_fox_v[i], cache_fox_logf[i])
        xs, st_s = _layer(xs, p_sample[i], st_in, lw, alpha)
        new_s.append(st_s)
    stack = lambda sts, j: jnp.stack([s[j] for s in sts], axis=0)
    outs_p = tuple(stack(new_p, j) for j in range(8))
    outs_s = tuple(stack(new_s, j) for j in range(8))
    return (xp, xs) + outs_p + outs_s
```

```python
import functools
import math

import numpy as np
import jax
import jax.numpy as jnp
from jax import lax
from jax.experimental import pallas as pl
from jax.experimental.pallas import tpu as pltpu

F32 = jnp.float32
BF16 = jnp.bfloat16

D_MODEL = 1024
N_HEADS = 4
HEAD_DIM = 64
BRANCH_W = N_HEADS * HEAD_DIM
CHUNK = 64
W_LORA, A_LORA, G_LORA = 32, 32, 64
NOPE, ROPE, Q_LORA, KV_LORA = 64, 32, 192, 128
ROPE_THETA = 10000.0
N_EXPERTS, N_GROUPS, EXPERT_FF = 16, 4, 256
GROUP_SIZE = N_EXPERTS // N_GROUPS
PLE_DIM = 256
A_COLS = 4 * BRANCH_W
B_COLS = 3 * BRANCH_W + W_LORA + A_LORA + G_LORA
C_COLS = Q_LORA + KV_LORA + ROPE
D_COLS = 3 * BRANCH_W + N_HEADS
DEPTH_ALPHA_POW = 0.25
LN_EPS = 1e-5
RMS_EPS = 1e-6
RWKV_GN_EPS = 64e-5
MASK_VALUE = -1e30
LOG2E = math.log2(math.e)

LANES = 128
SUBLANES = 8
VMEM_LIMIT_BYTES = 56 * 1024 * 1024

NN = ((1,), (0,))
NT = ((1,), (1,))
TN = ((0,), (0,))


def _dg(a, b, dims=NN):
    return lax.dot_general(a, b, (dims, ((), ())), preferred_element_type=F32)


def _bdot(a, b, dims=NN):
    return _dg(a.astype(BF16), b.astype(BF16), dims)


def _split(x, pieces):
    out = []
    r = x
    for i in range(pieces):
        p = r.astype(BF16)
        out.append(p)
        if i + 1 < pieces:
            r = r - p.astype(F32)
    return out


def _dot3(a, b, dims=NN):
    ah, al = _split(a, 2)
    bh, bl = _split(b, 2)
    return _dg(ah, bh, dims) + (_dg(ah, bl, dims) + _dg(al, bh, dims))


def _dotc(a, c, dims=NN, pieces=3):
    ps = _split(a, pieces)
    acc = _dg(ps[0], c, dims)
    for p in ps[1:]:
        acc = acc + _dg(p, c, dims)
    return acc


def _iota(shape, dim):
    return lax.broadcasted_iota(jnp.int32, shape, dim)


def _tri_incl(n, dtype=BF16):
    return (_iota((n, n), 0) >= _iota((n, n), 1)).astype(dtype)


def _head_indicator():
    r = _iota((BRANCH_W, BRANCH_W), 0) // HEAD_DIM
    c = _iota((BRANCH_W, BRANCH_W), 1) // HEAD_DIM
    return (r == c).astype(BF16)


def _layer_norm(x, g, b):
    mu = jnp.mean(x, axis=-1, keepdims=True)
    xc = x - mu
    var = jnp.mean(xc * xc, axis=-1, keepdims=True)
    return xc * lax.rsqrt(var + LN_EPS) * g + b


def _softplus(x):
    return jnp.maximum(x, 0.0) + jnp.log(1.0 + jnp.exp(-jnp.abs(x)))


def _silu(x):
    return x * jax.nn.sigmoid(x)


def _cparams(*sem):
    return pltpu.CompilerParams(dimension_semantics=sem, vmem_limit_bytes=VMEM_LIMIT_BYTES)


def _resident(shape):
    nd = len(shape)
    return pl.BlockSpec(shape, lambda *_: (0,) * nd)


TOKEN_TILE = 512
MOE_TOKEN_TILE = 1024


def _token_tile(t, largest=TOKEN_TILE):
    tm = largest
    while tm >= SUBLANES:
        if t % tm == 0:
            return tm
        tm //= 2
    raise ValueError(f"token count {t} not a multiple of 8")


def _ln_kernel(x_ref, g_ref, b_ref, o_ref):
    o_ref[...] = _layer_norm(x_ref[...], g_ref[...], b_ref[...])


def _ln_call(x, g, b):
    t, d = x.shape
    tm = _token_tile(t)
    return pl.pallas_call(
        _ln_kernel,
        grid=(t // tm,),
        in_specs=[pl.BlockSpec((tm, d), lambda i: (i, 0)), _resident((1, d)), _resident((1, d))],
        out_specs=pl.BlockSpec((tm, d), lambda i: (i, 0)),
        out_shape=jax.ShapeDtypeStruct((t, d), F32),
        compiler_params=_cparams("parallel"),
        name="ln_in",
    )(x, g.reshape(1, d), b.reshape(1, d))


_IN_SLOTS = (("a", A_COLS, A_COLS), ("b", B_COLS, B_COLS), ("qlat", Q_LORA, 256), ("kvlat", KV_LORA, 128),
             ("kr", ROPE, 128), ("fq", BRANCH_W, BRANCH_W), ("fk", BRANCH_W, BRANCH_W), ("fv", BRANCH_W, BRANCH_W),
             ("ff", N_HEADS, 128))


def _relayout_w_in(w_in):
    parts, off = [], 0
    for _, width, slot in _IN_SLOTS:
        w = w_in[:, off:off + width]
        if slot > width:
            w = jnp.pad(w, ((0, 0), (0, slot - width)))
        parts.append(w)
        off += width
    assert off == w_in.shape[1]
    return jnp.concatenate(parts, axis=1).astype(BF16)


def _in_kernel(x_ref, w_ref, *o_refs):
    xb = x_ref[...].astype(BF16)
    off = 0
    for o_ref, (_, _, slot) in zip(o_refs, _IN_SLOTS):
        o_ref[...] = _dg(xb, w_ref[:, off:off + slot])
        off += slot


def _in_call(x, w_in_p):
    t, d = x.shape
    tm = _token_tile(t)
    n = w_in_p.shape[1]
    return pl.pallas_call(
        _in_kernel,
        grid=(t // tm,),
        in_specs=[pl.BlockSpec((tm, d), lambda i: (i, 0)), _resident((d, n))],
        out_specs=[pl.BlockSpec((tm, slot), lambda i: (i, 0)) for _, _, slot in _IN_SLOTS],
        out_shape=[jax.ShapeDtypeStruct((t, slot), F32) for _, _, slot in _IN_SLOTS],
        compiler_params=_cparams("parallel"),
        name="in_proj",
    )(x, w_in_p)


def _hgrn_kernel(c_ref, lb_ref, ng_ref, s0_ref, o_ref, s_ref, st_scr, *, chunk, group):
    j = pl.program_id(1)

    @pl.when(j == 0)
    def _():
        st_scr[...] = s0_ref[...]

    for bi in range(group):
        _hgrn_chunk_one(bi, c_ref, lb_ref, ng_ref, o_ref, st_scr, chunk)

    @pl.when(j == pl.num_programs(1) - 1)
    def _():
        s_ref[...] = st_scr[...]


def _hgrn_chunk_one(bi, c_ref, lb_ref, ng_ref, o_ref, st_scr, chunk):
    c = chunk
    cols = c_ref[bi]
    w = BRANCH_W
    q, fz, iv, g = cols[:, :w], cols[:, w:2 * w], cols[:, 2 * w:3 * w], cols[:, 3 * w:]
    lb = lb_ref[...]
    f = lb + (1.0 - lb) * jax.nn.sigmoid(fz)
    k = 1.0 - f
    a = _dotc(_tri_incl(c), jnp.log(f) * LOG2E, pieces=3)
    a_last = a[c - 1:c, :]
    st = st_scr[bi]
    ind = _head_indicator()

    o_inter = _bdot(q * jnp.exp2(a), st, NT)
    kd = k * jnp.exp2(a_last - a)
    rr = _iota((w, w), 0) // HEAD_DIM
    cc = _iota((w, w), 1) // HEAD_DIM
    upd = jnp.where(rr == cc, _bdot(iv, kd, TN), 0.0)
    st_scr[bi] = st * jnp.exp2(a_last) + upd

    row = _iota((SUBLANES, w), 0)
    outs = []
    for tb in range(c // SUBLANES):
        lo = tb * SUBLANES
        q_tb = q[lo:lo + SUBLANES]
        a_tb = a[lo:lo + SUBLANES]
        pieces = []
        for s in range(lo + SUBLANES):
            d = a_tb - a[s:s + 1]
            if s >= lo:
                d = jnp.where(row >= s - lo, d, MASK_VALUE)
            pieces.append(q_tb * jnp.exp2(d) * k[s:s + 1])
        e = _dotc(jnp.concatenate(pieces, axis=0), ind, pieces=1)
        acc = e[0:SUBLANES] * iv[0:1]
        for s in range(1, lo + SUBLANES):
            acc = acc + e[s * SUBLANES:(s + 1) * SUBLANES] * iv[s:s + 1]
        outs.append(acc)
    o = o_inter + jnp.concatenate(outs, axis=0)

    ms = _dotc(o * o, ind, pieces=2) * (1.0 / HEAD_DIM)
    o = o * lax.rsqrt(ms + RMS_EPS) * ng_ref[...] * _silu(g)
    o_ref[bi] = o.astype(o_ref.dtype)


def _hgrn_call(cols_a, lb, norm_g, st0):
    b, l, _ = cols_a.shape
    c = CHUNK if l % CHUNK == 0 else l
    w = BRANCH_W
    bg = _batch_group(b)
    return pl.pallas_call(
        functools.partial(_hgrn_kernel, chunk=c, group=bg),
        grid=(b // bg, l // c),
        in_specs=[pl.BlockSpec((bg, c, A_COLS), lambda i, j: (i, j, 0)), _resident((1, w)), _resident((1, w)),
                  pl.BlockSpec((bg, w, w), lambda i, j: (i, 0, 0))],
        out_specs=[pl.BlockSpec((bg, c, w), lambda i, j: (i, j, 0)),
                   pl.BlockSpec((bg, w, w), lambda i, j: (i, 0, 0))],
        out_shape=[jax.ShapeDtypeStruct((b, l, w), BF16), jax.ShapeDtypeStruct((b, w, w), F32)],
        scratch_shapes=[pltpu.VMEM((bg, w, w), F32)],
        compiler_params=_cparams("parallel", "arbitrary"),
        name="hgrn",
    )(cols_a, lb.reshape(1, w), norm_g.reshape(1, w), st0)


def _hgrn_state_to_blockdiag(s):
    b = s.shape[0]
    out = jnp.zeros((b, N_HEADS, HEAD_DIM, N_HEADS, HEAD_DIM), F32)
    for h in range(N_HEADS):
        out = out.at[:, h, :, h, :].set(jnp.swapaxes(s[:, h], -1, -2))
    return out.reshape(b, BRANCH_W, BRANCH_W)


def _hgrn_state_from_blockdiag(st):
    hs = [st[:, h * HEAD_DIM:(h + 1) * HEAD_DIM, h * HEAD_DIM:(h + 1) * HEAD_DIM] for h in range(N_HEADS)]
    return jnp.swapaxes(jnp.stack(hs, axis=1), -1, -2)


def _rwkv_prep_kernel(c_ref, pblk_ref, prow_ref, mu_ref, w0_ref, lora_ref, a0_ref, kk_ref, ka_ref,
                      rk_ref, r_ref, lw_ref, k_ref, v_ref, a_ref, b_ref, g_ref, bonus_ref):
    j = pl.program_id(1)
    cols = c_ref[0]
    prev = jnp.where(j == 0, prow_ref[0], pblk_ref[0, SUBLANES - 1:SUBLANES, :])
    shifted = jnp.where(_iota(cols.shape, 0) == 0, prev, pltpu.roll(cols, 1, 0))
    m = cols + (shifted - cols) * mu_ref[...]
    w = BRANCH_W
    r, k, v = m[:, :w], m[:, w:2 * w], m[:, 2 * w:3 * w]
    slab = m[:, 3 * w:]
    lane = _iota(slab.shape, 1)
    act = jnp.where(lane < W_LORA, jnp.tanh(slab),
                    jnp.where(lane < W_LORA + A_LORA, slab, jax.nn.sigmoid(slab)))
    lora = _bdot(act, lora_ref[...])
    w_log = -_softplus(-(w0_ref[...] + lora[:, :w])) - 0.5
    a_rate = jax.nn.sigmoid(a0_ref[...] + lora[:, w:2 * w])
    g = lora[:, 2 * w:]
    ind = _head_indicator()
    kk = k * kk_ref[...]
    norm = jnp.sqrt(_dotc(kk * kk, ind, pieces=3))
    kk = kk / jnp.maximum(norm, 1e-12)
    kh = k * (1.0 + (a_rate - 1.0) * ka_ref[...])
    r_ref[0] = r
    lw_ref[0] = -jnp.exp(w_log)
    k_ref[0] = kh
    v_ref[0] = v
    a_ref[0] = -kk
    b_ref[0] = kk * a_rate
    g_ref[0] = g
    bonus_ref[0] = _dotc(r * kh * rk_ref[...], ind, pieces=3) * v


def _rwkv_prep_call(cols_b, prev_row, mu, w0, w2, a0, a2, g2, kk_s, ka, rk):
    b, l, nb = cols_b.shape
    tm = _token_tile(l)
    w = BRANCH_W
    row = lambda x: x.reshape(1, -1)
    lora_w = jnp.zeros((LANES, 3 * w), F32)
    lora_w = lora_w.at[:W_LORA, :w].set(w2).at[W_LORA:W_LORA + A_LORA, w:2 * w].set(a2)
    lora_w = lora_w.at[W_LORA + A_LORA:, 2 * w:].set(g2).astype(BF16)
    tok = pl.BlockSpec((1, tm, nb), lambda i, j: (i, j, 0))
    pblk =pl.BlockSpec((1, SUBLANES, nb), lambda i, j: (i, jnp.maximum(j * (tm // SUBLANES) - 1, 0), 0))
    out_spec = pl.BlockSpec((1, tm, w), lambda i, j: (i, j, 0))
    return pl.pallas_call(
        _rwkv_prep_kernel,
        grid=(b, l // tm),
        in_specs=[tok, pblk, pl.BlockSpec((1, 1, nb), lambda i, j: (i, 0, 0)), _resident((1, nb)),
                  _resident((1, w)), _resident((LANES, 3 * w)), _resident((1, w)),
                  _resident((1, w)), _resident((1, w)), _resident((1, w))],
        out_specs=[out_spec] * 8,
        out_shape=[jax.ShapeDtypeStruct((b, l, w), F32)] * 8,
        compiler_params=_cparams("parallel", "parallel"),
        name="rwkv_prep",
    )(cols_b, cols_b, prev_row.reshape(b, 1, nb), row(mu), row(w0), lora_w, row(a0), row(kk_s), row(ka), row(rk))


def _rwkv_chunk_kernel(r_ref, lw_ref, k_ref, v_ref, a_ref, b_ref, g_ref, bonus_ref, lg_ref, lb_ref, s0_ref,
                       o_ref, s_ref, st_scr, *, chunk, group):
    j = pl.program_id(1)
    c = chunk
    w = BRANCH_W
    hd = HEAD_DIM
    nh = N_HEADS
    n = nh * c
    seqs = range(group)

    @pl.when(j == 0)
    def _():
        st_scr[...] = jnp.zeros(st_scr.shape, F32)
        for bi in seqs:
            for h in range(nh):
                st_scr[bi, h * hd:(h + 1) * hd, h * hd:(h + 1) * hd] = s0_ref[bi, h]

    tri = _tri_incl(c)
    ind = _head_indicator()
    t_w = _iota((c, n), 0)
    s_w = _iota((c, n), 1) % c
    strict_w = t_w > s_w
    incl_w = t_w >= s_w
    eye_w = (t_w == s_w).astype(F32)
    bd_keep = (_iota((n, n), 0) // c) == (_iota((n, n), 1) // c)
    hr_keep = (_iota((n, w), 0) // c) == (_iota((n, w), 1) // hd)
    st_keep = (_iota((w, w), 0) // hd) == (_iota((w, w), 1) // hd)

    def bd(x):
        return jnp.where(bd_keep, jnp.concatenate([x] * nh, axis=0), 0.0)

    def head_rows(x):
        return jnp.where(hr_keep, jnp.concatenate([x] * nh, axis=0), 0.0)

    sq = []
    for bi in seqs:
        r, lw, k, v, a, b = r_ref[bi], lw_ref[bi], k_ref[bi], v_ref[bi], a_ref[bi], b_ref[bi]
        cum = _dotc(tri, lw, pieces=3)
        last = cum[c - 1:c, :]
        e_neg = jnp.exp(-cum)
        e_last = jnp.exp(last - cum)
        sq.append(dict(v=v, last=last, lhs=jnp.concatenate([a * jnp.exp(cum - lw), r * jnp.exp(cum)], axis=0),
                       bt=b * e_neg, kt=k * e_neg, hat=jnp.concatenate([b * e_last, k * e_last], axis=0)))

    for q in sq:
        pb = _dot3(q["lhs"], head_rows(q["bt"]), NT)
        pk = _dot3(q["lhs"], head_rows(q["kt"]), NT)
        q["l_ab"] = jnp.where(strict_w, pb[:c], 0.0)
        q["a_rb"] = jnp.where(incl_w, pb[c:], 0.0)
        q["lk"] = jnp.concatenate([jnp.where(strict_w, pk[:c], 0.0), jnp.where(incl_w, pk[c:], 0.0)], axis=0)
        q["tinv"] = eye_w + q["l_ab"]
        q["p"] = q["l_ab"]

    for _ in range(int(math.log2(c)) - 1):
        for q in sq:
            q["p"] = _dot3(q["p"], bd(q["p"]))
        for q in sq:
            q["tinv"] = q["tinv"] + _dot3(q["tinv"], bd(q["p"]))

    for bi, q in zip(seqs, sq):
        q["st"] = st_scr[bi]
        q["sh"] = _dot3(q["lhs"], q["st"], NT)
        q["lv"] = _dot3(q["lk"], head_rows(q["v"]))
    for q in sq:
        q["u"] = _dot3(q["tinv"], head_rows(q["sh"][:c] + q["lv"][:c]))
    for bi, q in zip(seqs, sq):
        upd = _dot3(jnp.concatenate([q["u"], q["v"]], axis=0), q["hat"], TN)
        st_scr[bi] = q["st"] * jnp.exp(q["last"]) + jnp.where(st_keep, upd, 0.0)
    for q in sq:
        q["y"] = q["sh"][c:] + _dot3(q["a_rb"], head_rows(q["u"])) + q["lv"][c:]

    for bi, q in zip(seqs, sq):
        y = q["y"]
        mu = _dotc(y, ind, pieces=3) * (1.0 / hd)
        yc = y - mu
        var = _dotc(yc * yc, ind, pieces=2) * (1.0 / hd)
        yn = yc * lax.rsqrt(var + RWKV_GN_EPS) * lg_ref[...] + lb_ref[...]
        o_ref[bi] = ((yn + bonus_ref[bi]) * g_ref[bi]).astype(o_ref.dtype)

    @pl.when(j == pl.num_programs(1) - 1)
    def _():
        for bi in seqs:
            for h in range(nh):
                s_ref[bi, h] = st_scr[bi, h * hd:(h + 1) * hd, h * hd:(h + 1) * hd]


def _batch_group(b):
    return 4 if b % 4 == 0 else 1


def _rwkv_chunk_call(prep, lnx_g, lnx_b, s0):
    r = prep[0]
    b, l, w = r.shape
    c = CHUNK if l % CHUNK == 0 else l
    bg = _batch_group(b)
    tok = pl.BlockSpec((bg, c, w), lambda i, j: (i, j, 0))
    st_spec = pl.BlockSpec((bg, N_HEADS, HEAD_DIM, HEAD_DIM), lambda i, j: (i, 0, 0, 0))
    return pl.pallas_call(
        functools.partial(_rwkv_chunk_kernel, chunk=c, group=bg),
        grid=(b // bg, l // c),
        in_specs=[tok] * 8 + [_resident((1, w)), _resident((1, w)), st_spec],
        out_specs=[tok, st_spec],
        out_shape=[jax.ShapeDtypeStruct((b, l, w), BF16),
                   jax.ShapeDtypeStruct((b, N_HEADS, HEAD_DIM, HEAD_DIM), F32)],
        scratch_shapes=[pltpu.VMEM((bg, w, w), F32)],
        compiler_params=_cparams("parallel", "arbitrary"),
        name="rwkv_chunk",
    )(*prep, lnx_g.reshape(1, w), lnx_b.reshape(1, w), s0)


def _rope_tables(p, l):
    half = ROPE // 2
    inv = 1.0 / (ROPE_THETA ** (jnp.arange(half, dtype=F32) / half))
    ang = (p + jnp.arange(l, dtype=jnp.int32)).astype(F32)[:, None] * inv[None, :]
    cos, sin = jnp.cos(ang), jnp.sin(ang)
    pad = jnp.zeros((l, LANES - ROPE), F32)
    return jnp.concatenate([cos, cos, pad], axis=1), jnp.concatenate([-sin, sin, pad], axis=1)


def _swap_halves(x, base):
    half = ROPE // 2
    n = x.shape[-1]
    lane = _iota(x.shape, x.ndim - 1) % LANES
    up = pltpu.roll(x, n - half, x.ndim - 1)
    down = pltpu.roll(x, half, x.ndim - 1)
    return jnp.where((lane >= base) & (lane < base + half), up,
                     jnp.where((lane >= base + half) & (lane < base + ROPE), down, 0.0))


def _rms_norm(x, g, width):
    ms = jnp.sum(x * x, axis=-1, keepdims=True) * (1.0 / width)
    return x * lax.rsqrt(ms + RMS_EPS) * g


def _mla_q_kernel(ql_ref, kvl_ref, kr_ref, cos_ref, sin_ref, qg_ref, wuq_ref, kvg_ref, q_ref, c_ref, krn_ref):
    scale = (NOPE + ROPE) ** -0.5 * LOG2E
    qn = _rms_norm(ql_ref[0], qg_ref[...], Q_LORA)
    q = _bdot(qn, wuq_ref[...])
    cos, sin = cos_ref[...], sin_ref[...]
    lane = _iota(cos.shape, 1)
    cos_q = jnp.where(lane < NOPE, 1.0, pltpu.roll(cos, NOPE, 1))
    sin_q = pltpu.roll(sin, NOPE, 1)
    for h in range(N_HEADS):
        qh = q[:, h * LANES:(h + 1) * LANES]
        qh = qh * cos_q + _swap_halves(qh, NOPE) * sin_q
        q_ref[0, h] = (qh * scale).astype(q_ref.dtype)
    c_ref[0] = _rms_norm(kvl_ref[0], kvg_ref[...], KV_LORA)
    kr = kr_ref[0]
    krn = kr * cos + _swap_halves(kr, 0) * sin
    krn_ref[0] = krn[:, :ROPE]


def _mla_q_call(qlat, kvlat, kr, cos, sin, qn_g, w_uq_p, kvn_g):
    b, l, _ = qlat.shape
    tm = _token_tile(l)
    tok = lambda n: pl.BlockSpec((1, tm, n), lambda i, j: (i, j, 0))
    tab = pl.BlockSpec((tm, LANES), lambda i, j: (j, 0))
    return pl.pallas_call(
        _mla_q_kernel,
        grid=(b, l // tm),
        in_specs=[tok(256), tok(KV_LORA), tok(LANES), tab, tab, _resident((1, 256)),
                  _resident((256, N_HEADS * LANES)), _resident((1, KV_LORA))],
        out_specs=[pl.BlockSpec((1, N_HEADS, tm, LANES), lambda i, j: (i, 0, j, 0)), tok(KV_LORA), tok(ROPE)],
        out_shape=[jax.ShapeDtypeStruct((b, N_HEADS, l, LANES), BF16), jax.ShapeDtypeStruct((b, l, KV_LORA), F32),
                   jax.ShapeDtypeStruct((b, l, ROPE), F32)],
        compiler_params=_cparams("parallel", "parallel"),
        name="mla_q",
    )(qlat, kvlat, kr, cos, sin, qn_g, w_uq_p, kvn_g)


def _mla_kv_kernel(c_ref, kr_ref, wk_ref, wv_ref, k_ref, v_ref):
    cb = c_ref[0].astype(BF16)
    kall = _dg(cb, wk_ref[...])
    vall = _dg(cb, wv_ref[...])
    tm = cb.shape[0]
    place = (_iota((ROPE, LANES), 0) + NOPE == _iota((ROPE, LANES), 1)).astype(BF16)
    kr_slot = _dg(kr_ref[0].astype(BF16), place)
    one_slot = (_iota((tm, LANES), 1) == HEAD_DIM).astype(F32)
    for h in range(N_HEADS):
        k_ref[0, h] = (kall[:, h * LANES:(h + 1) * LANES] + kr_slot).astype(k_ref.dtype)
        v_ref[0, h, 0] = (vall[:, h * LANES:(h + 1) * LANES] + one_slot).T.astype(v_ref.dtype)


def _kv_specs(b, t, tm):
    kspec = pl.BlockSpec((1, N_HEADS, tm, LANES), lambda i, j: (i, 0, j, 0))
    vspec = pl.BlockSpec((1, N_HEADS, 1, LANES, tm), lambda i, j: (i, 0, j, 0, 0))
    shapes = [jax.ShapeDtypeStruct((b, N_HEADS, t, LANES), BF16),
              jax.ShapeDtypeStruct((b, N_HEADS, t // tm, LANES, tm), BF16)]
    return [kspec, vspec], shapes


def _mla_kv_call(c_all, kr_all, w_k_p, w_v_p):
    b, t, _ = c_all.shape
    tm = _token_tile(t)
    tok = lambda n: pl.BlockSpec((1, tm, n), lambda i, j: (i, j, 0))
    out_specs, out_shape = _kv_specs(b, t, tm)
    return pl.pallas_call(
        _mla_kv_kernel,
        grid=(b, t // tm),
        in_specs=[tok(KV_LORA), tok(ROPE), _resident((KV_LORA, N_HEADS * LANES)),
                  _resident((KV_LORA, N_HEADS * LANES))],
        out_specs=out_specs,
        out_shape=out_shape,
        compiler_params=_cparams("parallel", "parallel"),
        name="mla_kv",
    )(c_all, kr_all, w_k_p, w_v_p)


def _fox_lf_kernel(ff_ref, bf_ref, lf_ref, lfw_ref):
    z = ff_ref[0] + bf_ref[...]
    lf = jnp.where(_iota(z.shape, 1) < N_HEADS, -_softplus(-z), 0.0)
    lfw_ref[0] = lf
    lf_ref[0] = lf[:, :N_HEADS]


def _fox_lf_call(ff, bf):
    b, l, _ = ff.shape
    tm = _token_tile(l)
    return pl.pallas_call(
        _fox_lf_kernel,
        grid=(b, l // tm),
        in_specs=[pl.BlockSpec((1, tm, LANES), lambda i, j: (i, j, 0)), _resident((1, LANES))],
        out_specs=[pl.BlockSpec((1, tm, N_HEADS), lambda i, j: (i, j, 0)),
                   pl.BlockSpec((1, tm, LANES), lambda i, j: (i, j, 0))],
        out_shape=[jax.ShapeDtypeStruct((b, l, N_HEADS), F32), jax.ShapeDtypeStruct((b, l, LANES), F32)],
        compiler_params=_cparams("parallel", "parallel"),
        name="fox_lf",
    )(ff, jnp.pad(bf, (0, LANES - N_HEADS)).reshape(1, LANES))


def _cumsum_kernel(x_ref, o_ref, carry):
    @pl.when(pl.program_id(1) == 0)
    def _():
        carry[...] = jnp.zeros_like(carry)

    x = x_ref[0]
    c = _dotc(_tri_incl(x.shape[0]), x, pieces=3) + carry[...]
    o_ref[0] = c
    carry[...] = c[x.shape[0] - 1:, :]


def _cumsum_call(x):
    b, t, n = x.shape
    tm = _token_tile(t)
    spec = pl.BlockSpec((1, tm, n), lambda i, j: (i, j, 0))
    return pl.pallas_call(
        _cumsum_kernel,
        grid=(b, t // tm),
        in_specs=[spec],
        out_specs=spec,
        out_shape=jax.ShapeDtypeStruct((b, t, n), F32),
        scratch_shapes=[pltpu.VMEM((1, n), F32)],
        compiler_params=_cparams("parallel", "arbitrary"),
        name="fox_cumsum",
    )(x)


def _head_slot(x, h):
    pair = x[:, (h // 2) * LANES:(h // 2 + 1) * LANES]
    return pair if h % 2 == 0 else pltpu.roll(pair, HEAD_DIM, 1)


def _bias_lanes(c_col, first, ones_first):
    tm = c_col.shape[0]
    lane = _iota((tm, LANES), 1)
    p0 = c_col.astype(BF16).astype(F32)
    r1 = c_col - p0
    p1 = r1.astype(BF16).astype(F32)
    p2 = r1 - p1
    out = jnp.where(lane == first, p0, jnp.where(lane == first + 1, p1, jnp.where(lane == first + 2, p2, 0.0)))
    return jnp.where((lane >= ones_first) & (lane < ones_first + 3), 1.0, out)


def _fox_q_kernel(q_ref, c_ref, o_ref):
    scale = HEAD_DIM ** -0.5 * LOG2E
    q = q_ref[0]
    c = c_ref[0] * LOG2E
    lane = _iota((q.shape[0], LANES), 1)
    for h in range(N_HEADS):
        bias = _bias_lanes(c[:, h:h + 1], HEAD_DIM, HEAD_DIM + 3)
        o_ref[0, h] = jnp.where(lane < HEAD_DIM, _head_slot(q, h) * scale, bias).astype(o_ref.dtype)


def _fox_kv_kernel(k_ref, v_ref, c_ref, ko_ref, vo_ref):
    k, v, c = k_ref[0], v_ref[0], c_ref[0] * LOG2E
    lane = _iota((k.shape[0], LANES), 1)
    for h in range(N_HEADS):
        bias = _bias_lanes(-c[:, h:h + 1], HEAD_DIM + 3, HEAD_DIM)
        ko_ref[0, h] = jnp.where(lane < HEAD_DIM, _head_slot(k, h), bias).astype(ko_ref.dtype)
        vo_ref[0, h, 0] = jnp.where(lane < HEAD_DIM, _head_slot(v, h),
                                    (lane == HEAD_DIM).astype(F32)).T.astype(vo_ref.dtype)


def _fox_q_call(q, c_new):
    b, l, w = q.shape
    tm = _token_tile(l)
    return pl.pallas_call(
        _fox_q_kernel,
        grid=(b, l // tm),
        in_specs=[pl.BlockSpec((1, tm, w), lambda i, j: (i, j, 0)),
                  pl.BlockSpec((1, tm, LANES), lambda i, j: (i, j, 0))],
        out_specs=pl.BlockSpec((1, N_HEADS, tm, LANES), lambda i, j: (i, 0, j, 0)),
        out_shape=jax.ShapeDtypeStruct((b, N_HEADS, l, LANES), BF16),
        compiler_params=_cparams("parallel", "parallel"),
        name="fox_q",
    )(q, c_new)


def _fox_kv_call(k_all, v_all, c_all):
    b, t, w = k_all.shape
    tm = _token_tile(t)
    tok = pl.BlockSpec((1, tm, w), lambda i, j: (i, j, 0))
    out_specs, out_shape = _kv_specs(b, t, tm)
    return pl.pallas_call(
        _fox_kv_kernel,
        grid=(b, t // tm),
        in_specs=[tok, tok, pl.BlockSpec((1, tm, LANES), lambda i, j: (i, j, 0))],
        out_specs=out_specs,
        out_shape=out_shape,
        compiler_params=_cparams("parallel", "parallel"),
        name="fox_kv",
    )(k_all, v_all, c_all)


def _flash_kernel(q_ref, k_ref, v_ref, o_ref, m_scr, acc_scr, s_scr, *, tq, tk, past, t_valid, chunk_mask):
    qi = pl.program_id(1)
    q_start = past + qi * tq
    if chunk_mask:
        vis_end = jnp.minimum(((q_start + tq + CHUNK - 1) // CHUNK) * CHUNK, t_valid)
    else:
        vis_end = q_start + tq
    n_blocks = (vis_end + tk - 1) // tk
    n_full = q_start // tk

    m_scr[...] = jnp.full(m_scr.shape, MASK_VALUE, F32)
    acc_scr[...] = jnp.zeros(acc_scr.shape, F32)

    def logits(h, kb):
        ks = pl.multiple_of(kb * tk, tk)
        return _dg(k_ref[0, h, pl.ds(ks, tk), :], q_ref[0, h], NT)

    s_scr[...] = logits(0, 0)

    def block(kb, masked):
        if masked:
            kpos = kb * tk + _iota((tk, tq), 0)
            qpos = q_start + _iota((tk, tq), 1)
            if chunk_mask:
                keep = ((kpos // CHUNK) <= (qpos // CHUNK)) & (kpos < t_valid)
            else:
                keep = kpos <= qpos
        s_next = s_scr[...]
        for h in range(N_HEADS):
            s = jnp.where(keep, s_next, MASK_VALUE) if masked else s_next
            if h + 1 < N_HEADS:
                s_next = logits(h + 1, kb)
            else:
                s_next = logits(0, jnp.minimum(kb + 1, n_blocks - 1))
            m_old = m_scr[h]
            m_new = jnp.maximum(m_old, jnp.max(s, axis=0, keepdims=True))
            p = jnp.exp2(s - m_new[:1])
            pv = _dg(v_ref[0, h, kb], p.astype(BF16))
            acc_scr[h] = acc_scr[h] * jnp.exp2(m_old - m_new)[:1] + pv
            m_scr[h] = m_new
        s_scr[...] = s_next

    def full_body(kb, carry):
        block(kb, False)
        return carry

    def masked_body(kb, carry):
        block(kb, True)
        return carry

    lax.fori_loop(0, n_full, full_body, 0)
    lax.fori_loop(n_full, n_blocks, masked_body, 0)
    for h in range(N_HEADS):
        acc = acc_scr[h].T
        o = acc[:, :HEAD_DIM] / acc[:, HEAD_DIM:HEAD_DIM + 1]
        o_ref[0, :, h * HEAD_DIM:(h + 1) * HEAD_DIM] = o.astype(o_ref.dtype)


def _flash_call(q, k, v, *, past, t_valid, chunk_mask, name):
    b, nh, l, _ = q.shape
    t = k.shape[2]
    tk = v.shape[-1]
    lq = max(l, LANES)
    if lq != l:
        q = jnp.pad(q, ((0, 0), (0, 0), (0, lq - l), (0, 0)))
    tq = _token_tile(lq)
    out = pl.pallas_call(
        functools.partial(_flash_kernel, tq=tq, tk=tk, past=past, t_valid=t_valid, chunk_mask=chunk_mask),
        grid=(b, lq // tq),
        in_specs=[pl.BlockSpec((1, nh, tq, LANES), lambda i, j: (i, 0, j, 0)),
                  pl.BlockSpec((1, nh, t, LANES), lambda i, j: (i, 0, 0, 0)),
                  pl.BlockSpec((1, nh, t // tk, LANES, tk), lambda i, j: (i, 0, 0, 0, 0))],
        out_specs=pl.BlockSpec((1, tq, BRANCH_W), lambda i, j: (i, j, 0)),
        out_shape=jax.ShapeDtypeStruct((b, lq, BRANCH_W), BF16),
        scratch_shapes=[pltpu.VMEM((nh, SUBLANES, tq), F32), pltpu.VMEM((nh, LANES, tq), F32),
                        pltpu.VMEM((tk, tq), F32)],
        compiler_params=_cparams("parallel", "arbitrary"),
        name=name,
    )(q, k, v)
    return out[:, :l]


def _mix_kernel(x_ref, oa_ref, ob_ref, oc_ref, od_ref, wmg_ref, wbr_ref, wo_ref, g_ref, b_ref, o_ref, *, alpha):
    x = x_ref[...]
    xb = x.astype(BF16)
    d = D_MODEL
    acc = None
    for n, br_ref in enumerate((oa_ref, ob_ref, oc_ref, od_ref)):
        gate = jax.nn.sigmoid(_dg(xb, wmg_ref[:, n * d:(n + 1) * d]))
        term = gate * _dg(br_ref[...], wbr_ref[n])
        acc = term if acc is None else acc + term
    mix = _dg(acc.astype(BF16), wo_ref[...])
    o_ref[...] = _layer_norm(alpha * x + mix, g_ref[...], b_ref[...])


def _mix_call(x, oa, ob, oc, od, w_mg, w_br, w_o, g, b, alpha):
    t, d = x.shape
    tm = _token_tile(t)
    tok = lambda n: pl.BlockSpec((tm, n), lambda i: (i, 0))
    return pl.pallas_call(
        functools.partial(_mix_kernel, alpha=alpha),
        grid=(t // tm,),
        in_specs=[tok(d)] + [tok(BRANCH_W)] * 4 + [_resident(w_mg.shape), _resident(w_br.shape),
                                                   _resident(w_o.shape), _resident((1, d)), _resident((1, d))],
        out_specs=tok(d),
        out_shape=jax.ShapeDtypeStruct((t, d), F32),
        compiler_params=_cparams("parallel"),
        name="mix_ln1",
    )(x, oa, ob, oc, od, w_mg, w_br, w_o, g.reshape(1, d), b.reshape(1, d))


def _route(x, rw_ref, rb_ref):
    scores = jax.nn.sigmoid(_dot3(rw_ref[...], x, NT)[:N_EXPERTS])
    biased = scores + rb_ref[...][:N_EXPERTS]
    col = [biased[e:e + 1] for e in range(N_EXPERTS)]
    gs = []
    for g in range(N_GROUPS):
        v = col[g * GROUP_SIZE:(g + 1) * GROUP_SIZE]
        best = None
        for i in range(GROUP_SIZE):
            for j in range(i + 1, GROUP_SIZE):
                s = v[i] + v[j]
                best = s if best is None else jnp.maximum(best, s)
        gs.append(best)
    gates = []
    for g in range(N_GROUPS):
        sel = None
        for o in range(N_GROUPS):
            if o == g:
                continue
            cond = (gs[g] > gs[o]) if o < g else (gs[g] >= gs[o])
            sel = cond if sel is None else sel & cond
        v = col[g * GROUP_SIZE:(g + 1) * GROUP_SIZE]
        for i in range(GROUP_SIZE):
            rank = None
            for j in range(GROUP_SIZE):
                if j == i:
                    continue
                ahead = (v[j] >= v[i]) if j < i else (v[j] > v[i])
                ahead = ahead.astype(F32)
                rank = ahead if rank is None else rank + ahead
            e = g * GROUP_SIZE + i
            gates.append(jnp.where(sel & (rank < 2.0), scores[e:e + 1], 0.0))
    total = gates[0]
    for gt in gates[1:]:
        total = total + gt
    return [gt / total for gt in gates]


def _moe_kernel(x_ref, rw_ref, rb_ref, wgu_ref, wd_ref, g_ref, b_ref, o_ref, acc_scr, gate_scr, *, alpha):
    e = pl.program_id(1)

    @pl.when(e == 0)
    def _():
        gates = _route(x_ref[...], rw_ref, rb_ref)
        tm = gate_scr.shape[0]
        gt = jnp.concatenate(gates + [jnp.zeros((LANES - N_EXPERTS, tm), F32)], axis=0)
        gate_scr[...] = gt.T
        acc_scr[...] = jnp.zeros(acc_scr.shape, F32)

    xb = x_ref[...].astype(BF16)
    hgu = _dg(xb, wgu_ref[0])
    gm = gate_scr[...]
    gate = jnp.sum(jnp.where(_iota(gm.shape, 1) == e, gm, 0.0), axis=-1, keepdims=True)
    h = _silu(hgu[:, :EXPERT_FF]) * hgu[:, EXPERT_FF:]
    acc_scr[...] += gate * _dg(h.astype(BF16), wd_ref[0])

    @pl.when(e == pl.num_programs(1) - 1)
    def _():
        o_ref[...] = _layer_norm(alpha * x_ref[...] + acc_scr[...], g_ref[...], b_ref[...])


def _moe_call(x, router_w, router_b, w_gu, w_d, g, b, alpha):
    t, d = x.shape
    tm = _token_tile(t, MOE_TOKEN_TILE)
    tok = pl.BlockSpec((tm, d), lambda i, e: (i, 0))
    return pl.pallas_call(
        functools.partial(_moe_kernel, alpha=alpha),
        grid=(t // tm, N_EXPERTS),
        in_specs=[tok, _resident((LANES, d)), _resident((LANES, 1)),
                  pl.BlockSpec((1, d, 2 * EXPERT_FF), lambda i, e: (e, 0, 0)),
                  pl.BlockSpec((1, EXPERT_FF, d), lambda i, e: (e, 0, 0)), _resident((1, d)), _resident((1, d))],
        out_specs=tok,
        out_shape=jax.ShapeDtypeStruct((t, d), F32),
        scratch_shapes=[pltpu.VMEM((tm, d), F32), pltpu.VMEM((tm, LANES), F32)],
        compiler_params=_cparams("parallel", "arbitrary"),
        name="moe_ln2",
    )(x, jnp.pad(router_w.T, ((0, LANES - N_EXPERTS), (0, 0))),
      jnp.pad(router_b, (0, LANES - N_EXPERTS)).reshape(LANES, 1), w_gu, w_d, g.reshape(1, d), b.reshape(1, d))


def _ple_kernel(x_ref, p_ref, pw_ref, gw_ref, g_ref, b_ref, o_ref, *, alpha):
    x = x_ref[...]
    ple = _bdot(p_ref[...], pw_ref[...]) * jax.nn.sigmoid(_bdot(x, gw_ref[...]))
    o_ref[...] = _layer_norm(alpha * x + ple, g_ref[...], b_ref[...])


def _ple_call(x, p, ple_w, gate_w, g, b, alpha):
    t, d = x.shape
    tm = _token_tile(t)
    tok = lambda n: pl.BlockSpec((tm, n), lambda i: (i, 0))
    return pl.pallas_call(
        functools.partial(_ple_kernel, alpha=alpha),
        grid=(t // tm,),
        in_specs=[tok(d), tok(PLE_DIM), _resident(ple_w.shape), _resident(gate_w.shape), _resident((1, d)),
                  _resident((1, d))],
        out_specs=tok(d),
        out_shape=jax.ShapeDtypeStruct((t, d), F32),
        compiler_params=_cparams("parallel"),
        name="ple_ln3",
    )(x, p, ple_w, gate_w, g.reshape(1, d), b.reshape(1, d))


def _lb_kernel(x_ref, o_ref):
    x = x_ref[...]
    depth = x.shape[0]
    m = jnp.max(x, axis=0, keepdims=True)
    e = jnp.exp(x - m)
    pr = e / jnp.sum(e, axis=0, keepdims=True)
    run = jnp.zeros_like(pr[0:1])
    for i in range(depth):
        o_ref[i:i + 1, :] = run
        run = run + pr[i:i + 1]


def _lb_call(logits):
    return pl.pallas_call(
        _lb_kernel,
        out_shape=jax.ShapeDtypeStruct(logits.shape, F32),
        name="hgrn_lb",
    )(logits)


def _pad_rows(x, t):
    pad = t - x.shape[1]
    if pad == 0:
        return x
    return jnp.pad(x, ((0, 0), (0, pad)) + ((0, 0),) * (x.ndim - 2))


def _layer(x, p, st, lw, alpha):
    b, l, d = x.shape
    t = b * l
    hgrn_s, rwkv_s, rwkv_prev, lat_past, kr_past, fk_past, fv_past, flf_past = st
    past = 0 if lat_past is None else lat_past.shape[1]
    t_valid = past + l
    t_pad = -(-t_valid // LANES) * LANES if past else t_valid

    cols = _in_call(x.reshape(t, d), lw["w_in"])
    ca, cb, qlat, kvlat, kr, fq, fk, fv, ff = [c.reshape(b, l, -1) for c in cols]

    st0 = jnp.zeros((b, BRANCH_W, BRANCH_W), F32) if hgrn_s is None else _hgrn_state_to_blockdiag(hgrn_s)
    o_a, hgrn_bd = _hgrn_call(ca, lw["lb"], lw["hgrn_norm_g"], st0)
    hgrn_new = _hgrn_state_from_blockdiag(hgrn_bd)

    prev_row = jnp.zeros((b, B_COLS), F32) if rwkv_prev is None else rwkv_prev
    s0 = jnp.zeros((b, N_HEADS, HEAD_DIM, HEAD_DIM), F32) if rwkv_s is None else rwkv_s
    prep = _rwkv_prep_call(cb, prev_row, lw["rwkv_mu"], lw["rwkv_w0"], lw["rwkv_w2"], lw["rwkv_a0"], lw["rwkv_a2"],
                           lw["rwkv_g2"], lw["rwkv_kk"], lw["rwkv_ka"], lw["rwkv_rk"])
    o_b, rwkv_new = _rwkv_chunk_call(prep, lw["rwkv_lnx_g"], lw["rwkv_lnx_b"], s0)
    shift_new = cb[:, l - 1]

    cos, sin = _rope_tables(past, l)
    q_c, lat_new, kr_new = _mla_q_call(qlat, kvlat, kr, cos, sin, lw["mla_qn_g"], lw["mla_w_uq"], lw["mla_kvn_g"])
    if past:
        c_all = _pad_rows(jnp.concatenate([lat_past, lat_new], axis=1), t_pad)
        kr_all = _pad_rows(jnp.concatenate([kr_past, kr_new], axis=1), t_pad)
    else:
        c_all, kr_all = lat_new, kr_new
    k_c, v_c = _mla_kv_call(c_all, kr_all, lw["mla_w_k"], lw["mla_w_v"])
    o_c = _flash_call(q_c, k_c, v_c, past=past, t_valid=t_valid, chunk_mask=True, name="mla_attn")

    lf_new, lf_wide = _fox_lf_call(ff, lw["fox_bf"])
    if past:
        flf_wide = jnp.pad(flf_past, ((0, 0), (0, 0), (0, LANES - N_HEADS)))
        lf_all = _pad_rows(jnp.concatenate([flf_wide, lf_wide], axis=1), t_pad)
        k_all = _pad_rows(jnp.concatenate([fk_past.reshape(b, past, BRANCH_W), fk], axis=1), t_pad)
        v_all = _pad_rows(jnp.concatenate([fv_past.reshape(b, past, BRANCH_W), fv], axis=1), t_pad)
    else:
        lf_all, k_all, v_all = lf_wide, fk, fv
    c_all_f = _cumsum_call(lf_all)
    q_d = _fox_q_call(fq, c_all_f[:, past:past + l])
    k_d, v_d = _fox_kv_call(k_all, v_all, c_all_f)
    o_d = _flash_call(q_d, k_d, v_d, past=past, t_valid=t_valid, chunk_mask=False, name="fox_attn")

    flat = lambda o: o.reshape(t, BRANCH_W)
    x1 = _mix_call(x.reshape(t, d), flat(o_a), flat(o_b), flat(o_c), flat(o_d), lw["w_mg"], lw["w_br"], lw["w_o"],
                   lw["ln1_g"], lw["ln1_b"], alpha)
    x2 = _moe_call(x1, lw["router_w"], lw["router_b"], lw["w_gu"], lw["w_d"], lw["ln2_g"], lw["ln2_b"], alpha)
    x3 = _ple_call(x2, p.reshape(t, PLE_DIM), lw["ple_w"], lw["ple_gate_w"], lw["ln3_g"], lw["ln3_b"], alpha)
    new = (hgrn_new, rwkv_new, shift_new, lat_new, kr_new, fk.reshape(b, l, N_HEADS, HEAD_DIM),
           fv.reshape(b, l, N_HEADS, HEAD_DIM), lf_new)
    return x3.reshape(b, l, d), new


def _relayout_w_uq(w_uq):
    w = w_uq.reshape(Q_LORA, N_HEADS, NOPE + ROPE)
    w = jnp.pad(w, ((0, 256 - Q_LORA), (0, 0), (0, LANES - NOPE - ROPE)))
    return w.reshape(256, N_HEADS * LANES).astype(BF16)


def _relayout_w_ukv(w_ukv):
    w = w_ukv.reshape(KV_LORA, N_HEADS, NOPE + HEAD_DIM)
    pad = lambda x: jnp.pad(x, ((0, 0), (0, 0), (0, LANES - x.shape[-1]))).reshape(KV_LORA, N_HEADS * LANES)
    return pad(w[..., :NOPE]).astype(BF16), pad(w[..., NOPE:]).astype(BF16)


def kernel(x_prompt, x_sample, state_hgrn, state_rwkv, state_rwkv_shift, cache_mla_latent, cache_mla_krope, cache_fox_k, cache_fox_v, cache_fox_logf, p_prompt, p_sample, ln_in_g, ln_in_b, w_in, hgrn_lb_logits, hgrn_norm_g, rwkv_mu, rwkv_w0, rwkv_w2, rwkv_a0, rwkv_a2, rwkv_g2, rwkv_kk, rwkv_ka, rwkv_rk, rwkv_lnx_g, rwkv_lnx_b, mla_qnorm_g, mla_w_uq, mla_kvnorm_g, mla_w_ukv, fox_bf, w_br, w_mg, w_o, ln1_g, ln1_b, router_w, router_b, exp_w_gate, exp_w_up, exp_w_down, ln2_g, ln2_b, ple_w, ple_gate_w, ln3_g, ln3_b):
    depth = w_in.shape[0]
    alpha = (2 * depth) ** DEPTH_ALPHA_POW
    d = x_prompt.shape[-1]
    lb_all = _lb_call(hgrn_lb_logits)

    def ln_in(x):
        return _ln_call(x.reshape(-1, d), ln_in_g, ln_in_b).reshape(x.shape)

    xp, xs = ln_in(x_prompt), ln_in(x_sample)
    new_p, new_s = [], []
    for i in range(depth):
        w_k, w_v = _relayout_w_ukv(mla_w_ukv[i])
        lw = dict(
            w_in=_relayout_w_in(w_in[i]), lb=lb_all[i], hgrn_norm_g=hgrn_norm_g[i], rwkv_mu=rwkv_mu[i],
            rwkv_w0=rwkv_w0[i], rwkv_w2=rwkv_w2[i], rwkv_a0=rwkv_a0[i], rwkv_a2=rwkv_a2[i], rwkv_g2=rwkv_g2[i],
            rwkv_kk=rwkv_kk[i], rwkv_ka=rwkv_ka[i], rwkv_rk=rwkv_rk[i], rwkv_lnx_g=rwkv_lnx_g[i],
            rwkv_lnx_b=rwkv_lnx_b[i],
            mla_qn_g=jnp.pad(mla_qnorm_g[i], (0, 256 - Q_LORA)).reshape(1, 256), mla_w_uq=_relayout_w_uq(mla_w_uq[i]),
            mla_kvn_g=mla_kvnorm_g[i].reshape(1, KV_LORA), mla_w_k=w_k, mla_w_v=w_v, fox_bf=fox_bf[i],
            w_br=w_br[i].astype(BF16), w_mg=w_mg[i].astype(BF16), w_o=w_o[i].astype(BF16),
            ln1_g=ln1_g[i], ln1_b=ln1_b[i], router_w=router_w, router_b=router_b,
            w_gu=jnp.concatenate([exp_w_gate[i], exp_w_up[i]], axis=-1).astype(BF16), w_d=exp_w_down[i].astype(BF16),
            ln2_g=ln2_g[i], ln2_b=ln2_b[i], ple_w=ple_w[i].astype(BF16), ple_gate_w=ple_gate_w[i].astype(BF16),
            ln3_g=ln3_g[i], ln3_b=ln3_b[i])
        xp, st_p = _layer(xp, p_prompt[i], (None,) * 8, lw, alpha)
        new_p.append(st_p)
        st_in = (state_hgrn[i], state_rwkv[i], state_rwkv_shift[i], cache_mla_latent[i], cache_mla_krope[i],
                 cache_fox_k[i], cache_fox_v[i], cache_fox_logf[i])
        xs, st_s = _layer(xs, p_sample[i], st_in, lw, alpha)
        new_s.append(st_s)
    stack = lambda sts, j: jnp.stack([s[j] for s in sts], axis=0)
    outs_p = tuple(stack(new_p, j) for j in range(8))
    outs_s = tuple(stack(new_s, j) for j in range(8))
    return (xp, xs) + outs_p + outs_s
```

```python
import functools
import math

import numpy as np
import jax
import jax.numpy as jnp
from jax import lax
from jax.experimental import pallas as pl
from jax.experimental.pallas import tpu as pltpu

F32 = jnp.float32
BF16 = jnp.bfloat16

D_MODEL = 1024
N_HEADS = 4
HEAD_DIM = 64
BRANCH_W = N_HEADS * HEAD_DIM
CHUNK = 64
W_LORA, A_LORA, G_LORA = 32, 32, 64
NOPE, ROPE, Q_LORA, KV_LORA = 64, 32, 192, 128
ROPE_THETA = 10000.0
N_EXPERTS, N_GROUPS, EXPERT_FF = 16, 4, 256
GROUP_SIZE = N_EXPERTS // N_GROUPS
PLE_DIM = 256
A_COLS = 4 * BRANCH_W
B_COLS = 3 * BRANCH_W + W_LORA + A_LORA + G_LORA
C_COLS = Q_LORA + KV_LORA + ROPE
D_COLS = 3 * BRANCH_W + N_HEADS
DEPTH_ALPHA_POW = 0.25
LN_EPS = 1e-5
RMS_EPS = 1e-6
RWKV_GN_EPS = 64e-5
MASK_VALUE = -1e30
LOG2E = math.log2(math.e)

LANES = 128
SUBLANES = 8
VMEM_LIMIT_BYTES = 56 * 1024 * 1024

NN = ((1,), (0,))
NT = ((1,), (1,))
TN = ((0,), (0,))


def _dg(a, b, dims=NN):
    return lax.dot_general(a, b, (dims, ((), ())), preferred_element_type=F32)


def _bdot(a, b, dims=NN):
    return _dg(a.astype(BF16), b.astype(BF16), dims)


def _split(x, pieces):
    out = []
    r = x
    for i in range(pieces):
        p = r.astype(BF16)
        out.append(p)
        if i + 1 < pieces:
            r = r - p.astype(F32)
    return out


def _dot3(a, b, dims=NN):
    ah, al = _split(a, 2)
    bh, bl = _split(b, 2)
    return _dg(ah, bh, dims) + (_dg(ah, bl, dims) + _dg(al, bh, dims))


def _dotc(a, c, dims=NN, pieces=3):
    ps = _split(a, pieces)
    acc = _dg(ps[0], c, dims)
    for p in ps[1:]:
        acc = acc + _dg(p, c, dims)
    return acc


def _iota(shape, dim):
    return lax.broadcasted_iota(jnp.int32, shape, dim)


def _tri_incl(n, dtype=BF16):
    return (_iota((n, n), 0) >= _iota((n, n), 1)).astype(dtype)


def _head_indicator():
    r = _iota((BRANCH_W, BRANCH_W), 0) // HEAD_DIM
    c = _iota((BRANCH_W, BRANCH_W), 1) // HEAD_DIM
    return (r == c).astype(BF16)


def _layer_norm(x, g, b):
    mu = jnp.mean(x, axis=-1, keepdims=True)
    xc = x - mu
    var = jnp.mean(xc * xc, axis=-1, keepdims=True)
    return xc * lax.rsqrt(var + LN_EPS) * g + b


def _softplus(x):
    return jnp.maximum(x, 0.0) + jnp.log(1.0 + jnp.exp(-jnp.abs(x)))


def _silu(x):
    return x * jax.nn.sigmoid(x)


def _cparams(*sem):
    return pltpu.CompilerParams(dimension_semantics=sem, vmem_limit_bytes=VMEM_LIMIT_BYTES)


def _resident(shape):
    nd = len(shape)
    return pl.BlockSpec(shape, lambda *_: (0,) * nd)


TOKEN_TILE = 512
MOE_TOKEN_TILE = 1024
HGRN_SUB_BLOCK = 16
ATTN_Q_TILE = 512
ATTN_KV_TILE = 512


def _token_tile(t, largest=TOKEN_TILE):
    tm = largest
    while tm >= SUBLANES:
        if t % tm == 0:
            return tm
        tm //= 2
    raise ValueError(f"token count {t} not a multiple of 8")


def _ln_kernel(x_ref, g_ref, b_ref, o_ref):
    o_ref[...] = _layer_norm(x_ref[...], g_ref[...], b_ref[...])


def _ln_call(x, g, b):
    t, d = x.shape
    tm = _token_tile(t)
    return pl.pallas_call(
        _ln_kernel,
        grid=(t // tm,),
        in_specs=[pl.BlockSpec((tm, d), lambda i: (i, 0)), _resident((1, d)), _resident((1, d))],
        out_specs=pl.BlockSpec((tm, d), lambda i: (i, 0)),
        out_shape=jax.ShapeDtypeStruct((t, d), F32),
        compiler_params=_cparams("parallel"),
        name="ln_in",
    )(x, g.reshape(1, d), b.reshape(1, d))


_IN_SLOTS = (("a", A_COLS, A_COLS), ("b", B_COLS, B_COLS), ("qlat", Q_LORA, 256), ("kvlat", KV_LORA, 128),
             ("kr", ROPE, 128), ("fq", BRANCH_W, BRANCH_W), ("fk", BRANCH_W, BRANCH_W), ("fv", BRANCH_W, BRANCH_W),
             ("ff", N_HEADS, 128))


def _relayout_w_in(w_in):
    parts, off = [], 0
    for _, width, slot in _IN_SLOTS:
        w = w_in[:, off:off + width]
        if slot > width:
            w = jnp.pad(w, ((0, 0), (0, slot - width)))
        parts.append(w)
        off += width
    assert off == w_in.shape[1]
    return jnp.concatenate(parts, axis=1).astype(BF16)


def _in_kernel(x_ref, w_ref, *o_refs):
    xb = x_ref[...].astype(BF16)
    off = 0
    for o_ref, (_, _, slot) in zip(o_refs, _IN_SLOTS):
        o_ref[...] = _dg(xb, w_ref[:, off:off + slot])
        off += slot


def _in_call(x, w_in_p):
    t, d = x.shape
    tm = _token_tile(t)
    n = w_in_p.shape[1]
    return pl.pallas_call(
        _in_kernel,
        grid=(t // tm,),
        in_specs=[pl.BlockSpec((tm, d), lambda i: (i, 0)), _resident((d, n))],
        out_specs=[pl.BlockSpec((tm, slot), lambda i: (i, 0)) for _, _, slot in _IN_SLOTS],
        out_shape=[jax.ShapeDtypeStruct((t, slot), F32) for _, _, slot in _IN_SLOTS],
        compiler_params=_cparams("parallel"),
        name="in_proj",
    )(x, w_in_p)


def _hgrn_kernel(c_ref, lb_ref, ng_ref, s0_ref, o_ref, s_ref, st_scr, *, chunk, group):
    j = pl.program_id(1)
    c = chunk
    w = BRANCH_W
    hd = HEAD_DIM
    nh = N_HEADS
    sb = min(HGRN_SUB_BLOCK, c)
    seqs = range(group)

    @pl.when(j == 0)
    def _():
        st_scr[...] = s0_ref[...]

    tri = _tri_incl(c)
    ind = _head_indicator()
    st_keep = (_iota((w, w), 0) // hd) == (_iota((w, w), 1) // hd)
    hr_keep = (_iota((nh * sb, w), 0) // sb) == (_iota((nh * sb, w), 1) // hd)
    row8 = _iota((SUBLANES, w), 0)
    lb = lb_ref[...]

    sq = []
    for bi in seqs:
        cols = c_ref[bi]
        q, fz, iv, g = cols[:, :w], cols[:, w:2 * w], cols[:, 2 * w:3 * w], cols[:, 3 * w:]
        f = lb + (1.0 - lb) * jax.nn.sigmoid(fz)
        a = _dotc(tri, jnp.log(f) * LOG2E, pieces=3)
        sq.append(dict(q=q, k=1.0 - f, iv=iv, g=g, a=a, a_last=a[c - 1:c, :], outs=[]))

    for bi, s in zip(seqs, sq):
        st = st_scr[bi]
        s["o"] = _bdot(s["q"] * jnp.exp2(s["a"]), st, NT)
        kd = s["k"] * jnp.exp2(s["a_last"] - s["a"])
        st_scr[bi] = st * jnp.exp2(s["a_last"]) + jnp.where(st_keep, _bdot(s["iv"], kd, TN), 0.0)

    for tb in range(c // sb):
        lo = tb * sb
        for s in sq:
            q, k, iv, a = s["q"], s["k"], s["iv"], s["a"]
            parts = []
            for rg in range(sb // SUBLANES):
                r0 = lo + rg * SUBLANES
                q8, a8 = q[r0:r0 + SUBLANES], a[r0:r0 + SUBLANES]
                pieces = []
                for src in range(lo, r0 + SUBLANES):
                    d = a8 - a[src:src + 1]
                    if src >= r0:
                        d = jnp.where(row8 >= src - r0, d, MASK_VALUE)
                    pieces.append(q8 * jnp.exp2(d) * k[src:src + 1])
                e = _dotc(jnp.concatenate(pieces, axis=0), ind, pieces=1)
                acc = e[0:SUBLANES] * iv[lo:lo + 1]
                for i in range(1, r0 + SUBLANES - lo):
                    acc = acc + e[i * SUBLANES:(i + 1) * SUBLANES] * iv[lo + i:lo + i + 1]
                parts.append(acc)
            o_tb = jnp.concatenate(parts, axis=0)
            if tb > 0:
                a_lo = a[lo:lo + 1]
                qh = q[lo:lo + sb] * jnp.exp2(a[lo:lo + sb] - a_lo)
                kh = k[:lo] * jnp.exp2(a_lo - a[:lo])
                qh_rows = jnp.where(hr_keep, jnp.concatenate([qh] * nh, axis=0), 0.0)
                sc = _bdot(qh_rows, kh, NT)
                ov = jnp.where(hr_keep, _bdot(sc, iv[:lo]), 0.0)
                for h in range(nh):
                    o_tb = o_tb + ov[h * sb:(h + 1) * sb]
            s["outs"].append(o_tb)

    for bi, s in zip(seqs, sq):
        o = s["o"] + jnp.concatenate(s["outs"], axis=0)
        ms = _dotc(o * o, ind, pieces=2) * (1.0 / hd)
        o = o * lax.rsqrt(ms + RMS_EPS) * ng_ref[...] * _silu(s["g"])
        o_ref[bi] = o.astype(o_ref.dtype)

    @pl.when(j == pl.num_programs(1) - 1)
    def _():
        s_ref[...] = st_scr[...]


def _hgrn_call(cols_a, lb, norm_g, st0):
    b, l, _ = cols_a.shape
    c = CHUNK if l % CHUNK == 0 else l
    w = BRANCH_W
    bg = _batch_group(b)
    return pl.pallas_call(
        functools.partial(_hgrn_kernel, chunk=c, group=bg),
        grid=(b // bg, l // c),
        in_specs=[pl.BlockSpec((bg, c, A_COLS), lambda i, j: (i, j, 0)), _resident((1, w)), _resident((1, w)),
                  pl.BlockSpec((bg, w, w), lambda i, j: (i, 0, 0))],
        out_specs=[pl.BlockSpec((bg, c, w), lambda i, j: (i, j, 0)),
                   pl.BlockSpec((bg, w, w), lambda i, j: (i, 0, 0))],
        out_shape=[jax.ShapeDtypeStruct((b, l, w), BF16), jax.ShapeDtypeStruct((b, w, w), F32)],
        scratch_shapes=[pltpu.VMEM((bg, w, w), F32)],
        compiler_params=_cparams("parallel", "arbitrary"),
        name="hgrn",
    )(cols_a, lb.reshape(1, w), norm_g.reshape(1, w), st0)


def _hgrn_state_to_blockdiag(s):
    b = s.shape[0]
    out = jnp.zeros((b, N_HEADS, HEAD_DIM, N_HEADS, HEAD_DIM), F32)
    for h in range(N_HEADS):
        out = out.at[:, h, :, h, :].set(jnp.swapaxes(s[:, h], -1, -2))
    return out.reshape(b, BRANCH_W, BRANCH_W)


def _hgrn_state_from_blockdiag(st):
    hs = [st[:, h * HEAD_DIM:(h + 1) * HEAD_DIM, h * HEAD_DIM:(h + 1) * HEAD_DIM] for h in range(N_HEADS)]
    return jnp.swapaxes(jnp.stack(hs, axis=1), -1, -2)


def _rwkv_tdot(a, b, dims=NN):
    return _bdot(a, b, dims)


def _rwkv_sdot(a, b, dims=NN):
    return _dot3(a, b, dims)


def _rwkv_operands(cols, prev, mu_ref, w0_ref, lora_ref, a0_ref, kk_ref, ka_ref, rk_ref):
    shifted = jnp.where(_iota(cols.shape, 0) == 0, prev, pltpu.roll(cols, 1, 0))
    m = cols + (shifted - cols) * mu_ref[...]
    w = BRANCH_W
    r, k, v = m[:, :w], m[:, w:2 * w], m[:, 2 * w:3 * w]
    slab = m[:, 3 * w:]
    lane = _iota(slab.shape, 1)
    act = jnp.where(lane < W_LORA, jnp.tanh(slab),
                    jnp.where(lane < W_LORA + A_LORA, slab, jax.nn.sigmoid(slab)))
    lora = _bdot(act, lora_ref[...])
    w_log = -_softplus(-(w0_ref[...] + lora[:, :w])) - 0.5
    a_rate = jax.nn.sigmoid(a0_ref[...] + lora[:, w:2 * w])
    g = lora[:, 2 * w:]
    ind = _head_indicator()
    kk = k * kk_ref[...]
    norm = jnp.sqrt(_dotc(kk * kk, ind, pieces=3))
    kk = kk / jnp.maximum(norm, 1e-12)
    kh = k * (1.0 + (a_rate - 1.0) * ka_ref[...])
    return dict(r=r, lw=-jnp.exp(w_log), k=kh, v=v, a=-kk, b=kk * a_rate, g=g,
                bonus=_dotc(r * kh * rk_ref[...], ind, pieces=3) * v)


def _rwkv_chunk_kernel(c_ref, prow_ref, mu_ref, w0_ref, lora_ref, a0_ref, kk_ref, ka_ref, rk_ref, lg_ref, lb_ref,
                       s0_ref, o_ref, s_ref, st_scr, prev_scr, *, chunk, group):
    j = pl.program_id(1)
    c = chunk
    w = BRANCH_W
    hd = HEAD_DIM
    nh = N_HEADS
    n = nh * c
    seqs = range(group)

    @pl.when(j == 0)
    def _():
        st_scr[...] = jnp.zeros(st_scr.shape, F32)
        prev_scr[...] = prow_ref[...]
        for bi in seqs:
            for h in range(nh):
                st_scr[bi, h * hd:(h + 1) * hd, h * hd:(h + 1) * hd] = s0_ref[bi, h]

    tri = _tri_incl(c)
    ind = _head_indicator()
    t_w = _iota((c, n), 0)
    s_w = _iota((c, n), 1) % c
    strict_w = t_w > s_w
    incl_w = t_w >= s_w
    eye_w = (t_w == s_w).astype(F32)
    bd_keep = (_iota((n, n), 0) // c) == (_iota((n, n), 1) // c)
    hr_keep = (_iota((n, w), 0) // c) == (_iota((n, w), 1) // hd)
    st_keep = (_iota((w, w), 0) // hd) == (_iota((w, w), 1) // hd)

    def bd(x):
        return jnp.where(bd_keep, jnp.concatenate([x] * nh, axis=0), 0.0)

    def head_rows(x):
        return jnp.where(hr_keep, jnp.concatenate([x] * nh, axis=0), 0.0)

    sq = []
    for bi in seqs:
        cols = c_ref[bi]
        ops = _rwkv_operands(cols, prev_scr[bi], mu_ref, w0_ref, lora_ref, a0_ref, kk_ref, ka_ref, rk_ref)
        prev_scr[bi] = cols[c - 1:c]
        r, lw, k, v, a, b = (ops[name] for name in ("r", "lw", "k", "v", "a", "b"))
        cum = _dotc(tri, lw, pieces=3)
        last = cum[c - 1:c, :]
        e_neg = jnp.exp(-cum)
        e_last = jnp.exp(last - cum)
        sq.append(dict(v=v, last=last, g=ops["g"], bonus=ops["bonus"],
                       lhs=jnp.concatenate([a * jnp.exp(cum - lw), r * jnp.exp(cum)], axis=0),
                       bt=b * e_neg, kt=k * e_neg, hat=jnp.concatenate([b * e_last, k * e_last], axis=0)))

    for q in sq:
        pb = _rwkv_tdot(q["lhs"], head_rows(q["bt"]), NT)
        pk = _rwkv_tdot(q["lhs"], head_rows(q["kt"]), NT)
        q["l_ab"] = jnp.where(strict_w, pb[:c], 0.0)
        q["a_rb"] = jnp.where(incl_w, pb[c:], 0.0)
        q["lk"] = jnp.concatenate([jnp.where(strict_w, pk[:c], 0.0), jnp.where(incl_w, pk[c:], 0.0)], axis=0)
        q["tinv"] = eye_w + q["l_ab"]
        q["p"] = q["l_ab"]

    for _ in range(int(math.log2(c)) - 1):
        for q in sq:
            q["p"] = _rwkv_tdot(q["p"], bd(q["p"]))
        for q in sq:
            q["tinv"] = q["tinv"] + _rwkv_tdot(q["tinv"], bd(q["p"]))

    for bi, q in zip(seqs, sq):
        q["st"] = st_scr[bi]
        q["sh"] = _rwkv_sdot(q["lhs"], q["st"], NT)
        q["lv"] = _rwkv_tdot(q["lk"], head_rows(q["v"]))
    for q in sq:
        q["u"] = _rwkv_tdot(q["tinv"], head_rows(q["sh"][:c] + q["lv"][:c]))
    for bi, q in zip(seqs, sq):
        upd = _rwkv_sdot(jnp.concatenate([q["u"], q["v"]], axis=0), q["hat"], TN)
        st_scr[bi] = q["st"] * jnp.exp(q["last"]) + jnp.where(st_keep, upd, 0.0)
    for q in sq:
        q["y"] = q["sh"][c:] + _rwkv_tdot(q["a_rb"], head_rows(q["u"])) + q["lv"][c:]

    for bi, q in zip(seqs, sq):
        y = q["y"]
        mu = _dotc(y, ind, pieces=3) * (1.0 / hd)
        yc = y - mu
        var = _dotc(yc * yc, ind, pieces=2) * (1.0 / hd)
        yn = yc * lax.rsqrt(var + RWKV_GN_EPS) * lg_ref[...] + lb_ref[...]
        o_ref[bi] = ((yn + q["bonus"]) * q["g"]).astype(o_ref.dtype)

    @pl.when(j == pl.num_programs(1) - 1)
    def _():
        for bi in seqs:
            for h in range(nh):
                s_ref[bi, h] = st_scr[bi, h * hd:(h + 1) * hd, h * hd:(h + 1) * hd]


def _batch_group(b):
    return 4 if b % 4 == 0 else 1


def _rwkv_call(cols_b, prev_row, s0, mu, w0, w2, a0, a2, g2, kk_s, ka, rk, lnx_g, lnx_b):
    b, l, nb = cols_b.shape
    w = BRANCH_W
    c = CHUNK if l % CHUNK == 0 else l
    bg = _batch_group(b)
    row = lambda x: x.reshape(1, -1)
    lora_w = jnp.zeros((LANES, 3 * w), F32)
    lora_w = lora_w.at[:W_LORA, :w].set(w2).at[W_LORA:W_LORA + A_LORA, w:2 * w].set(a2)
    lora_w = lora_w.at[W_LORA + A_LORA:, 2 * w:].set(g2).astype(BF16)
    st_spec = pl.BlockSpec((bg, N_HEADS, HEAD_DIM, HEAD_DIM), lambda i, j: (i, 0, 0, 0))
    return pl.pallas_call(
        functools.partial(_rwkv_chunk_kernel, chunk=c, group=bg),
        grid=(b // bg, l // c),
        in_specs=[pl.BlockSpec((bg, c, nb), lambda i, j: (i, j, 0)), pl.BlockSpec((bg, 1, nb), lambda i, j: (i, 0, 0)),
                  _resident((1, nb)), _resident((1, w)), _resident((LANES, 3 * w))] + [_resident((1, w))] * 6
                 + [st_spec],
        out_specs=[pl.BlockSpec((bg, c, w), lambda i, j: (i, j, 0)), st_spec],
        out_shape=[jax.ShapeDtypeStruct((b, l, w), BF16),
                   jax.ShapeDtypeStruct((b, N_HEADS, HEAD_DIM, HEAD_DIM), F32)],
        scratch_shapes=[pltpu.VMEM((bg, w, w), F32), pltpu.VMEM((bg, 1, nb), F32)],
        compiler_params=_cparams("parallel", "arbitrary"),
        name="rwkv",
    )(cols_b, prev_row.reshape(b, 1, nb), row(mu), row(w0), lora_w, row(a0), row(kk_s), row(ka), row(rk),
      row(lnx_g), row(lnx_b), s0)


def _rope_tables(p, l):
    half = ROPE // 2
    inv = 1.0 / (ROPE_THETA ** (jnp.arange(half, dtype=F32) / half))
    ang = (p + jnp.arange(l, dtype=jnp.int32)).astype(F32)[:, None] * inv[None, :]
    cos, sin = jnp.cos(ang), jnp.sin(ang)
    pad = jnp.zeros((l, LANES - ROPE), F32)
    return jnp.concatenate([cos, cos, pad], axis=1), jnp.concatenate([-sin, sin, pad], axis=1)


def _swap_halves(x, base):
    half = ROPE // 2
    n = x.shape[-1]
    lane = _iota(x.shape, x.ndim - 1) % LANES
    up = pltpu.roll(x, n - half, x.ndim - 1)
    down = pltpu.roll(x, half, x.ndim - 1)
    return jnp.where((lane >= base) & (lane < base + half), up,
                     jnp.where((lane >= base + half) & (lane < base + ROPE), down, 0.0))


def _rms_norm(x, g, width):
    ms = jnp.sum(x * x, axis=-1, keepdims=True) * (1.0 / width)
    return x * lax.rsqrt(ms + RMS_EPS) * g


def _mla_q_kernel(ql_ref, kvl_ref, kr_ref, cos_ref, sin_ref, qg_ref, wuq_ref, kvg_ref, q_ref, c_ref, krn_ref):
    scale = (NOPE + ROPE) ** -0.5 * LOG2E
    qn = _rms_norm(ql_ref[0], qg_ref[...], Q_LORA)
    q = _bdot(qn, wuq_ref[...])
    cos, sin = cos_ref[...], sin_ref[...]
    lane = _iota(cos.shape, 1)
    cos_q = jnp.where(lane < NOPE, 1.0, pltpu.roll(cos, NOPE, 1))
    sin_q = pltpu.roll(sin, NOPE, 1)
    for h in range(N_HEADS):
        qh = q[:, h * LANES:(h + 1) * LANES]
        qh = qh * cos_q + _swap_halves(qh, NOPE) * sin_q
        q_ref[0, h] = (qh * scale).astype(q_ref.dtype)
    c_ref[0] = _rms_norm(kvl_ref[0], kvg_ref[...], KV_LORA)
    kr = kr_ref[0]
    krn = kr * cos + _swap_halves(kr, 0) * sin
    krn_ref[0] = krn[:, :ROPE]


def _mla_q_call(qlat, kvlat, kr, cos, sin, qn_g, w_uq_p, kvn_g):
    b, l, _ = qlat.shape
    tm = _token_tile(l)
    tok = lambda n: pl.BlockSpec((1, tm, n), lambda i, j: (i, j, 0))
    tab = pl.BlockSpec((tm, LANES), lambda i, j: (j, 0))
    return pl.pallas_call(
        _mla_q_kernel,
        grid=(b, l // tm),
        in_specs=[tok(256), tok(KV_LORA), tok(LANES), tab, tab, _resident((1, 256)),
                  _resident((256, N_HEADS * LANES)), _resident((1, KV_LORA))],
        out_specs=[pl.BlockSpec((1, N_HEADS, tm, LANES), lambda i, j: (i, 0, j, 0)), tok(KV_LORA), tok(ROPE)],
        out_shape=[jax.ShapeDtypeStruct((b, N_HEADS, l, LANES), BF16), jax.ShapeDtypeStruct((b, l, KV_LORA), F32),
                   jax.ShapeDtypeStruct((b, l, ROPE), F32)],
        compiler_params=_cparams("parallel", "parallel"),
        name="mla_q",
    )(qlat, kvlat, kr, cos, sin, qn_g, w_uq_p, kvn_g)


def _mla_kv_kernel(c_ref, kr_ref, wk_ref, wv_ref, k_ref, v_ref):
    cb = c_ref[0].astype(BF16)
    kall = _dg(cb, wk_ref[...])
    vall = _dg(cb, wv_ref[...])
    tm = cb.shape[0]
    place = (_iota((ROPE, LANES), 0) + NOPE == _iota((ROPE, LANES), 1)).astype(BF16)
    kr_slot = _dg(kr_ref[0].astype(BF16), place)
    one_slot = (_iota((tm, LANES), 1) == HEAD_DIM).astype(F32)
    for h in range(N_HEADS):
        k_ref[0, h] = (kall[:, h * LANES:(h + 1) * LANES] + kr_slot).astype(k_ref.dtype)
        v_ref[0, h, 0] = (vall[:, h * LANES:(h + 1) * LANES] + one_slot).T.astype(v_ref.dtype)


def _kv_specs(b, t, tm):
    kspec = pl.BlockSpec((1, N_HEADS, tm, LANES), lambda i, j: (i, 0, j, 0))
    vspec = pl.BlockSpec((1, N_HEADS, 1, LANES, tm), lambda i, j: (i, 0, j, 0, 0))
    shapes = [jax.ShapeDtypeStruct((b, N_HEADS, t, LANES), BF16),
              jax.ShapeDtypeStruct((b, N_HEADS, t // tm, LANES, tm), BF16)]
    return [kspec, vspec], shapes


def _mla_kv_call(c_all, kr_all, w_k_p, w_v_p):
    b, t, _ = c_all.shape
    tm = _token_tile(t, ATTN_KV_TILE)
    tok = lambda n: pl.BlockSpec((1, tm, n), lambda i, j: (i, j, 0))
    out_specs, out_shape = _kv_specs(b, t, tm)
    return pl.pallas_call(
        _mla_kv_kernel,
        grid=(b, t // tm),
        in_specs=[tok(KV_LORA), tok(ROPE), _resident((KV_LORA, N_HEADS * LANES)),
                  _resident((KV_LORA, N_HEADS * LANES))],
        out_specs=out_specs,
        out_shape=out_shape,
        compiler_params=_cparams("parallel", "parallel"),
        name="mla_kv",
    )(c_all, kr_all, w_k_p, w_v_p)


def _fox_lf_kernel(ff_ref, bf_ref, lf_ref, lfw_ref):
    z = ff_ref[0] + bf_ref[...]
    lf = jnp.where(_iota(z.shape, 1) < N_HEADS, -_softplus(-z), 0.0)
    lfw_ref[0] = lf
    lf_ref[0] = lf[:, :N_HEADS]


def _fox_lf_call(ff, bf):
    b, l, _ = ff.shape
    tm = _token_tile(l)
    return pl.pallas_call(
        _fox_lf_kernel,
        grid=(b, l // tm),
        in_specs=[pl.BlockSpec((1, tm, LANES), lambda i, j: (i, j, 0)), _resident((1, LANES))],
        out_specs=[pl.BlockSpec((1, tm, N_HEADS), lambda i, j: (i, j, 0)),
                   pl.BlockSpec((1, tm, LANES), lambda i, j: (i, j, 0))],
        out_shape=[jax.ShapeDtypeStruct((b, l, N_HEADS), F32), jax.ShapeDtypeStruct((b, l, LANES), F32)],
        compiler_params=_cparams("parallel", "parallel"),
        name="fox_lf",
    )(ff, jnp.pad(bf, (0, LANES - N_HEADS)).reshape(1, LANES))


def _cumsum_kernel(x_ref, o_ref, carry):
    @pl.when(pl.program_id(1) == 0)
    def _():
        carry[...] = jnp.zeros_like(carry)

    x = x_ref[0]
    c = _dotc(_tri_incl(x.shape[0]), x, pieces=3) + carry[...]
    o_ref[0] = c
    carry[...] = c[x.shape[0] - 1:, :]


def _cumsum_call(x):
    b, t, n = x.shape
    tm = _token_tile(t)
    spec = pl.BlockSpec((1, tm, n), lambda i, j: (i, j, 0))
    return pl.pallas_call(
        _cumsum_kernel,
        grid=(b, t // tm),
        in_specs=[spec],
        out_specs=spec,
        out_shape=jax.ShapeDtypeStruct((b, t, n), F32),
        scratch_shapes=[pltpu.VMEM((1, n), F32)],
        compiler_params=_cparams("parallel", "arbitrary"),
        name="fox_cumsum",
    )(x)


def _head_slot(x, h):
    pair = x[:, (h // 2) * LANES:(h // 2 + 1) * LANES]
    return pair if h % 2 == 0 else pltpu.roll(pair, HEAD_DIM, 1)


def _bias_lanes(c_col, first, ones_first):
    tm = c_col.shape[0]
    lane = _iota((tm, LANES), 1)
    p0 = c_col.astype(BF16).astype(F32)
    r1 = c_col - p0
    p1 = r1.astype(BF16).astype(F32)
    p2 = r1 - p1
    out = jnp.where(lane == first, p0, jnp.where(lane == first + 1, p1, jnp.where(lane == first + 2, p2, 0.0)))
    return jnp.where((lane >= ones_first) & (lane < ones_first + 3), 1.0, out)


def _fox_q_kernel(q_ref, c_ref, o_ref):
    scale = HEAD_DIM ** -0.5 * LOG2E
    q = q_ref[0]
    c = c_ref[0] * LOG2E
    lane = _iota((q.shape[0], LANES), 1)
    for h in range(N_HEADS):
        bias = _bias_lanes(c[:, h:h + 1], HEAD_DIM, HEAD_DIM + 3)
        o_ref[0, h] = jnp.where(lane < HEAD_DIM, _head_slot(q, h) * scale, bias).astype(o_ref.dtype)


def _fox_kv_kernel(k_ref, v_ref, c_ref, ko_ref, vo_ref):
    k, v, c = k_ref[0], v_ref[0], c_ref[0] * LOG2E
    lane = _iota((k.shape[0], LANES), 1)
    for h in range(N_HEADS):
        bias = _bias_lanes(-c[:, h:h + 1], HEAD_DIM + 3, HEAD_DIM)
        ko_ref[0, h] = jnp.where(lane < HEAD_DIM, _head_slot(k, h), bias).astype(ko_ref.dtype)
        vo_ref[0, h, 0] = jnp.where(lane < HEAD_DIM, _head_slot(v, h),
                                    (lane == HEAD_DIM).astype(F32)).T.astype(vo_ref.dtype)


def _fox_q_call(q, c_new):
    b, l, w = q.shape
    tm = _token_tile(l)
    return pl.pallas_call(
        _fox_q_kernel,
        grid=(b, l // tm),
        in_specs=[pl.BlockSpec((1, tm, w), lambda i, j: (i, j, 0)),
                  pl.BlockSpec((1, tm, LANES), lambda i, j: (i, j, 0))],
        out_specs=pl.BlockSpec((1, N_HEADS, tm, LANES), lambda i, j: (i, 0, j, 0)),
        out_shape=jax.ShapeDtypeStruct((b, N_HEADS, l, LANES), BF16),
        compiler_params=_cparams("parallel", "parallel"),
        name="fox_q",
    )(q, c_new)


def _fox_kv_call(k_all, v_all, c_all):
    b, t, w = k_all.shape
    tm = _token_tile(t, ATTN_KV_TILE)
    tok = pl.BlockSpec((1, tm, w), lambda i, j: (i, j, 0))
    out_specs, out_shape = _kv_specs(b, t, tm)
    return pl.pallas_call(
        _fox_kv_kernel,
        grid=(b, t // tm),
        in_specs=[tok, tok, pl.BlockSpec((1, tm, LANES), lambda i, j: (i, j, 0))],
        out_specs=out_specs,
        out_shape=out_shape,
        compiler_params=_cparams("parallel", "parallel"),
        name="fox_kv",
    )(k_all, v_all, c_all)


def _flash_kernel(q_ref, k_ref, v_ref, o_ref, m_scr, acc_scr, s_scr, *, tq, tk, past, t_valid, chunk_mask):
    qi = pl.program_id(1)
    q_start = past + qi * tq
    if chunk_mask:
        vis_end = jnp.minimum(((q_start + tq + CHUNK - 1) // CHUNK) * CHUNK, t_valid)
    else:
        vis_end = q_start + tq
    n_blocks = (vis_end + tk - 1) // tk
    n_full = q_start // tk

    m_scr[...] = jnp.full(m_scr.shape, MASK_VALUE, F32)
    acc_scr[...] = jnp.zeros(acc_scr.shape, F32)

    def logits(h, kb):
        ks = pl.multiple_of(kb * tk, tk)
        return _dg(k_ref[0, h, pl.ds(ks, tk), :], q_ref[0, h], NT)

    s_scr[...] = logits(0, 0)

    def block(kb, masked):
        if masked:
            kpos = kb * tk + _iota((tk, tq), 0)
            qpos = q_start + _iota((tk, tq), 1)
            if chunk_mask:
                keep = ((kpos // CHUNK) <= (qpos // CHUNK)) & (kpos < t_valid)
            else:
                keep = kpos <= qpos
        s_next = s_scr[...]
        for h in range(N_HEADS):
            s = jnp.where(keep, s_next, MASK_VALUE) if masked else s_next
            if h + 1 < N_HEADS:
                s_next = logits(h + 1, kb)
            else:
                s_next = logits(0, jnp.minimum(kb + 1, n_blocks - 1))
            m_old = m_scr[h]
            m_new = jnp.maximum(m_old, jnp.max(s, axis=0, keepdims=True))
            p = jnp.exp2(s - m_new[:1])
            pv = _dg(v_ref[0, h, kb], p.astype(BF16))
            acc_scr[h] = acc_scr[h] * jnp.exp2(m_old - m_new)[:1] + pv
            m_scr[h] = m_new
        s_scr[...] = s_next

    def full_body(kb, carry):
        block(kb, False)
        return carry

    def masked_body(kb, carry):
        block(kb, True)
        return carry

    lax.fori_loop(0, n_full, full_body, 0)
    lax.fori_loop(n_full, n_blocks, masked_body, 0)
    for h in range(N_HEADS):
        acc = acc_scr[h].T
        o = acc[:, :HEAD_DIM] / acc[:, HEAD_DIM:HEAD_DIM + 1]
        o_ref[0, :, h * HEAD_DIM:(h + 1) * HEAD_DIM] = o.astype(o_ref.dtype)


def _flash_call(q, k, v, *, past, t_valid, chunk_mask, name):
    b, nh, l, _ = q.shape
    t = k.shape[2]
    tk = v.shape[-1]
    lq = max(l, LANES)
    if lq != l:
        q = jnp.pad(q, ((0, 0), (0, 0), (0, lq - l), (0, 0)))
    tq = _token_tile(lq, ATTN_Q_TILE)
    out = pl.pallas_call(
        functools.partial(_flash_kernel, tq=tq, tk=tk, past=past, t_valid=t_valid, chunk_mask=chunk_mask),
        grid=(b, lq // tq),
        in_specs=[pl.BlockSpec((1, nh, tq, LANES), lambda i, j: (i, 0, j, 0)),
                  pl.BlockSpec((1, nh, t, LANES), lambda i, j: (i, 0, 0, 0)),
                  pl.BlockSpec((1, nh, t // tk, LANES, tk), lambda i, j: (i, 0, 0, 0, 0))],
        out_specs=pl.BlockSpec((1, tq, BRANCH_W), lambda i, j: (i, j, 0)),
        out_shape=jax.ShapeDtypeStruct((b, lq, BRANCH_W), BF16),
        scratch_shapes=[pltpu.VMEM((nh, SUBLANES, tq), F32), pltpu.VMEM((nh, LANES, tq), F32),
                        pltpu.VMEM((tk, tq), F32)],
        compiler_params=_cparams("parallel", "arbitrary"),
        name=name,
    )(q, k, v)
    return out[:, :l]


def _mix_kernel(x_ref, oa_ref, ob_ref, oc_ref, od_ref, wmg_ref, wbr_ref, wo_ref, g_ref, b_ref, o_ref, *, alpha):
    x = x_ref[...]
    xb = x.astype(BF16)
    d = D_MODEL
    acc = None
    for n, br_ref in enumerate((oa_ref, ob_ref, oc_ref, od_ref)):
        gate = jax.nn.sigmoid(_dg(xb, wmg_ref[:, n * d:(n + 1) * d]))
        term = gate * _dg(br_ref[...], wbr_ref[n])
        acc = term if acc is None else acc + term
    mix = _dg(acc.astype(BF16), wo_ref[...])
    o_ref[...] = _layer_norm(alpha * x + mix, g_ref[...], b_ref[...])


def _mix_call(x, oa, ob, oc, od, w_mg, w_br, w_o, g, b, alpha):
    t, d = x.shape
    tm = _token_tile(t)
    tok = lambda n: pl.BlockSpec((tm, n), lambda i: (i, 0))
    return pl.pallas_call(
        functools.partial(_mix_kernel, alpha=alpha),
        grid=(t // tm,),
        in_specs=[tok(d)] + [tok(BRANCH_W)] * 4 + [_resident(w_mg.shape), _resident(w_br.shape),
                                                   _resident(w_o.shape), _resident((1, d)), _resident((1, d))],
        out_specs=tok(d),
        out_shape=jax.ShapeDtypeStruct((t, d), F32),
        compiler_params=_cparams("parallel"),
        name="mix_ln1",
    )(x, oa, ob, oc, od, w_mg, w_br, w_o, g.reshape(1, d), b.reshape(1, d))


def _route(x, rw_ref, rb_ref):
    scores = jax.nn.sigmoid(_dot3(rw_ref[...], x, NT)[:N_EXPERTS])
    biased = scores + rb_ref[...][:N_EXPERTS]
    col = [biased[e:e + 1] for e in range(N_EXPERTS)]
    gs = []
    for g in range(N_GROUPS):
        v = col[g * GROUP_SIZE:(g + 1) * GROUP_SIZE]
        best = None
        for i in range(GROUP_SIZE):
            for j in range(i + 1, GROUP_SIZE):
                s = v[i] + v[j]
                best = s if best is None else jnp.maximum(best, s)
        gs.append(best)
    gates = []
    for g in range(N_GROUPS):
        sel = None
        for o in range(N_GROUPS):
            if o == g:
                continue
            cond = (gs[g] > gs[o]) if o < g else (gs[g] >= gs[o])
            sel = cond if sel is None else sel & cond
        v = col[g * GROUP_SIZE:(g + 1) * GROUP_SIZE]
        for i in range(GROUP_SIZE):
            rank = None
            for j in range(GROUP_SIZE):
                if j == i:
                    continue
                ahead = (v[j] >= v[i]) if j < i else (v[j] > v[i])
                ahead = ahead.astype(F32)
                rank = ahead if rank is None else rank + ahead
            e = g * GROUP_SIZE + i
            gates.append(jnp.where(sel & (rank < 2.0), scores[e:e + 1], 0.0))
    total = gates[0]
    for gt in gates[1:]:
        total = total + gt
    return [gt / total for gt in gates]


def _moe_kernel(x_ref, rw_ref, rb_ref, wgu_ref, wd_ref, g_ref, b_ref, o_ref, acc_scr, gate_scr, *, alpha):
    e = pl.program_id(1)

    @pl.when(e == 0)
    def _():
        gates = _route(x_ref[...], rw_ref, rb_ref)
        tm = gate_scr.shape[0]
        gt = jnp.concatenate(gates + [jnp.zeros((LANES - N_EXPERTS, tm), F32)], axis=0)
        gate_scr[...] = gt.T
        acc_scr[...] = jnp.zeros(acc_scr.shape, F32)

    xb = x_ref[...].astype(BF16)
    hgu = _dg(xb, wgu_ref[0])
    gm = gate_scr[...]
    gate = jnp.sum(jnp.where(_iota(gm.shape, 1) == e, gm, 0.0), axis=-1, keepdims=True)
    h = _silu(hgu[:, :EXPERT_FF]) * hgu[:, EXPERT_FF:]
    acc_scr[...] += gate * _dg(h.astype(BF16), wd_ref[0])

    @pl.when(e == pl.num_programs(1) - 1)
    def _():
        o_ref[...] = _layer_norm(alpha * x_ref[...] + acc_scr[...], g_ref[...], b_ref[...])


def _moe_call(x, router_w, router_b, w_gu, w_d, g, b, alpha):
    t, d = x.shape
    tm = _token_tile(t, MOE_TOKEN_TILE)
    tok = pl.BlockSpec((tm, d), lambda i, e: (i, 0))
    return pl.pallas_call(
        functools.partial(_moe_kernel, alpha=alpha),
        grid=(t // tm, N_EXPERTS),
        in_specs=[tok, _resident((LANES, d)), _resident((LANES, 1)),
                  pl.BlockSpec((1, d, 2 * EXPERT_FF), lambda i, e: (e, 0, 0)),
                  pl.BlockSpec((1, EXPERT_FF, d), lambda i, e: (e, 0, 0)), _resident((1, d)), _resident((1, d))],
        out_specs=tok,
        out_shape=jax.ShapeDtypeStruct((t, d), F32),
        scratch_shapes=[pltpu.VMEM((tm, d), F32), pltpu.VMEM((tm, LANES), F32)],
        compiler_params=_cparams("parallel", "arbitrary"),
        name="moe_ln2",
    )(x, jnp.pad(router_w.T, ((0, LANES - N_EXPERTS), (0, 0))),
      jnp.pad(router_b, (0, LANES - N_EXPERTS)).reshape(LANES, 1), w_gu, w_d, g.reshape(1, d), b.reshape(1, d))


def _ple_kernel(x_ref, p_ref, pw_ref, gw_ref, g_ref, b_ref, o_ref, *, alpha):
    x = x_ref[...]
    ple = _bdot(p_ref[...], pw_ref[...]) * jax.nn.sigmoid(_bdot(x, gw_ref[...]))
    o_ref[...] = _layer_norm(alpha * x + ple, g_ref[...], b_ref[...])


def _ple_call(x, p, ple_w, gate_w, g, b, alpha):
    t, d = x.shape
    tm = _token_tile(t)
    tok = lambda n: pl.BlockSpec((tm, n), lambda i: (i, 0))
    return pl.pallas_call(
        functools.partial(_ple_kernel, alpha=alpha),
        grid=(t // tm,),
        in_specs=[tok(d), tok(PLE_DIM), _resident(ple_w.shape), _resident(gate_w.shape), _resident((1, d)),
                  _resident((1, d))],
        out_specs=tok(d),
        out_shape=jax.ShapeDtypeStruct((t, d), F32),
        compiler_params=_cparams("parallel"),
        name="ple_ln3",
    )(x, p, ple_w, gate_w, g.reshape(1, d), b.reshape(1, d))


def _lb_kernel(x_ref, o_ref):
    x = x_ref[...]
    depth = x.shape[0]
    m = jnp.max(x, axis=0, keepdims=True)
    e = jnp.exp(x - m)
    pr = e / jnp.sum(e, axis=0, keepdims=True)
    run = jnp.zeros_like(pr[0:1])
    for i in range(depth):
        o_ref[i:i + 1, :] = run
        run = run + pr[i:i + 1]


def _lb_call(logits):
    return pl.pallas_call(
        _lb_kernel,
        out_shape=jax.ShapeDtypeStruct(logits.shape, F32),
        name="hgrn_lb",
    )(logits)


def _pad_rows(x, t):
    pad = t - x.shape[1]
    if pad == 0:
        return x
    return jnp.pad(x, ((0, 0), (0, pad)) + ((0, 0),) * (x.ndim - 2))


def _layer(x, p, st, lw, alpha):
    b, l, d = x.shape
    t = b * l
    hgrn_s, rwkv_s, rwkv_prev, lat_past, kr_past, fk_past, fv_past, flf_past = st
    past = 0 if lat_past is None else lat_past.shape[1]
    t_valid = past + l
    t_pad = -(-t_valid // LANES) * LANES if past else t_valid

    cols = _in_call(x.reshape(t, d), lw["w_in"])
    ca, cb, qlat, kvlat, kr, fq, fk, fv, ff = [c.reshape(b, l, -1) for c in cols]

    st0 = jnp.zeros((b, BRANCH_W, BRANCH_W), F32) if hgrn_s is None else _hgrn_state_to_blockdiag(hgrn_s)
    o_a, hgrn_bd = _hgrn_call(ca, lw["lb"], lw["hgrn_norm_g"], st0)
    hgrn_new = _hgrn_state_from_blockdiag(hgrn_bd)

    prev_row = jnp.zeros((b, B_COLS), F32) if rwkv_prev is None else rwkv_prev
    s0 = jnp.zeros((b, N_HEADS, HEAD_DIM, HEAD_DIM), F32) if rwkv_s is None else rwkv_s
    o_b, rwkv_new = _rwkv_call(cb, prev_row, s0, lw["rwkv_mu"], lw["rwkv_w0"], lw["rwkv_w2"], lw["rwkv_a0"],
                               lw["rwkv_a2"], lw["rwkv_g2"], lw["rwkv_kk"], lw["rwkv_ka"], lw["rwkv_rk"],
                               lw["rwkv_lnx_g"], lw["rwkv_lnx_b"])
    shift_new = cb[:, l - 1]

    cos, sin = _rope_tables(past, l)
    q_c, lat_new, kr_new = _mla_q_call(qlat, kvlat, kr, cos, sin, lw["mla_qn_g"], lw["mla_w_uq"], lw["mla_kvn_g"])
    if past:
        c_all = _pad_rows(jnp.concatenate([lat_past, lat_new], axis=1), t_pad)
        kr_all = _pad_rows(jnp.concatenate([kr_past, kr_new], axis=1), t_pad)
    else:
        c_all, kr_all = lat_new, kr_new
    k_c, v_c = _mla_kv_call(c_all, kr_all, lw["mla_w_k"], lw["mla_w_v"])
    o_c = _flash_call(q_c, k_c, v_c, past=past, t_valid=t_valid, chunk_mask=True, name="mla_attn")

    lf_new, lf_wide = _fox_lf_call(ff, lw["fox_bf"])
    if past:
        flf_wide = jnp.pad(flf_past, ((0, 0), (0, 0), (0, LANES - N_HEADS)))
        lf_all = _pad_rows(jnp.concatenate([flf_wide, lf_wide], axis=1), t_pad)
        k_all = _pad_rows(jnp.concatenate([fk_past.reshape(b, past, BRANCH_W), fk], axis=1), t_pad)
        v_all = _pad_rows(jnp.concatenate([fv_past.reshape(b, past, BRANCH_W), fv], axis=1), t_pad)
    else:
        lf_all, k_all, v_all = lf_wide, fk, fv
    c_all_f = _cumsum_call(lf_all)
    q_d = _fox_q_call(fq, c_all_f[:, past:past + l])
    k_d, v_d = _fox_kv_call(k_all, v_all, c_all_f)
    o_d = _flash_call(q_d, k_d, v_d, past=past, t_valid=t_valid, chunk_mask=False, name="fox_attn")

    flat = lambda o: o.reshape(t, BRANCH_W)
    x1 = _mix_call(x.reshape(t, d), flat(o_a), flat(o_b), flat(o_c), flat(o_d), lw["w_mg"], lw["w_br"], lw["w_o"],
                   lw["ln1_g"], lw["ln1_b"], alpha)
    x2 = _moe_call(x1, lw["router_w"], lw["router_b"], lw["w_gu"], lw["w_d"], lw["ln2_g"], lw["ln2_b"], alpha)
    x3 = _ple_call(x2, p.reshape(t, PLE_DIM), lw["ple_w"], lw["ple_gate_w"], lw["ln3_g"], lw["ln3_b"], alpha)
    new = (hgrn_new, rwkv_new, shift_new, lat_new, kr_new, fk.reshape(b, l, N_HEADS, HEAD_DIM),
           fv.reshape(b, l, N_HEADS, HEAD_DIM), lf_new)
    return x3.reshape(b, l, d), new


def _relayout_w_uq(w_uq):
    w = w_uq.reshape(Q_LORA, N_HEADS, NOPE + ROPE)
    w = jnp.pad(w, ((0, 256 - Q_LORA), (0, 0), (0, LANES - NOPE - ROPE)))
    return w.reshape(256, N_HEADS * LANES).astype(BF16)


def _relayout_w_ukv(w_ukv):
    w = w_ukv.reshape(KV_LORA, N_HEADS, NOPE + HEAD_DIM)
    pad = lambda x: jnp.pad(x, ((0, 0), (0, 0), (0, LANES - x.shape[-1]))).reshape(KV_LORA, N_HEADS * LANES)
    return pad(w[..., :NOPE]).astype(BF16), pad(w[..., NOPE:]).astype(BF16)


def kernel(x_prompt, x_sample, state_hgrn, state_rwkv, state_rwkv_shift, cache_mla_latent, cache_mla_krope, cache_fox_k, cache_fox_v, cache_fox_logf, p_prompt, p_sample, ln_in_g, ln_in_b, w_in, hgrn_lb_logits, hgrn_norm_g, rwkv_mu, rwkv_w0, rwkv_w2, rwkv_a0, rwkv_a2, rwkv_g2, rwkv_kk, rwkv_ka, rwkv_rk, rwkv_lnx_g, rwkv_lnx_b, mla_qnorm_g, mla_w_uq, mla_kvnorm_g, mla_w_ukv, fox_bf, w_br, w_mg, w_o, ln1_g, ln1_b, router_w, router_b, exp_w_gate, exp_w_up, exp_w_down, ln2_g, ln2_b, ple_w, ple_gate_w, ln3_g, ln3_b):
    depth = w_in.shape[0]
    alpha = (2 * depth) ** DEPTH_ALPHA_POW
    d = x_prompt.shape[-1]
    lb_all = _lb_call(hgrn_lb_logits)

    def ln_in(x):
        return _ln_call(x.reshape(-1, d), ln_in_g, ln_in_b).reshape(x.shape)

    xp, xs = ln_in(x_prompt), ln_in(x_sample)
    new_p, new_s = [], []
    for i in range(depth):
        w_k, w_v = _relayout_w_ukv(mla_w_ukv[i])
        lw = dict(
            w_in=_relayout_w_in(w_in[i]), lb=lb_all[i], hgrn_norm_g=hgrn_norm_g[i], rwkv_mu=rwkv_mu[i],
            rwkv_w0=rwkv_w0[i], rwkv_w2=rwkv_w2[i], rwkv_a0=rwkv_a0[i], rwkv_a2=rwkv_a2[i], rwkv_g2=rwkv_g2[i],
            rwkv_kk=rwkv_kk[i], rwkv_ka=rwkv_ka[i], rwkv_rk=rwkv_rk[i], rwkv_lnx_g=rwkv_lnx_g[i],
            rwkv_lnx_b=rwkv_lnx_b[i],
            mla_qn_g=jnp.pad(mla_qnorm_g[i], (0, 256 - Q_LORA)).reshape(1, 256), mla_w_uq=_relayout_w_uq(mla_w_uq[i]),
            mla_kvn_g=mla_kvnorm_g[i].reshape(1, KV_LORA), mla_w_k=w_k, mla_w_v=w_v, fox_bf=fox_bf[i],
            w_br=w_br[i].astype(BF16), w_mg=w_mg[i].astype(BF16), w_o=w_o[i].astype(BF16),
            ln1_g=ln1_g[i], ln1_b=ln1_b[i], router_w=router_w, router_b=router_b,
            w_gu=jnp.concatenate([exp_w_gate[i], exp_w_up[i]], axis=-1).astype(BF16), w_d=exp_w_down[i].astype(BF16),
            ln2_g=ln2_g[i], ln2_b=ln2_b[i], ple_w=ple_w[i].astype(BF16), ple_gate_w=ple_gate_w[i].astype(BF16),
            ln3_g=ln3_g[i], ln3_b=ln3_b[i])
        xp, st_p = _layer(xp, p_prompt[i], (None,) * 8, lw, alpha)
        new_p.append(st_p)
        st_in = (state_hgrn[i], state_rwkv[i], state_rwkv_shift[i], cache_mla_latent[i], cache_mla_krope[i],
                 cache_fox_k[i], cache_fox_v[i], cache_fox_logf[i])
        xs, st_s = _layer(xs, p_sample[i], st_in, lw, alpha)
        new_s.append(st_s)
    stack = lambda sts, j: jnp.stack([s[j] for s in sts], axis=0)
    outs_p = tuple(stack(new_p, j) for j in range(8))
    outs_s = tuple(stack(new_s, j) for j in range(8))
    return (xp, xs) + outs_p + outs_s
```

```python
import functools
import math

import numpy as np
import jax
import jax.numpy as jnp
from jax import lax
from jax.experimental import pallas as pl
from jax.experimental.pallas import tpu as pltpu

F32 = jnp.float32
BF16 = jnp.bfloat16

D_MODEL = 1024
N_HEADS = 4
HEAD_DIM = 64
BRANCH_W = N_HEADS * HEAD_DIM
CHUNK = 64
W_LORA, A_LORA, G_LORA = 32, 32, 64
NOPE, ROPE, Q_LORA, KV_LORA = 64, 32, 192, 128
ROPE_THETA = 10000.0
N_EXPERTS, N_GROUPS, EXPERT_FF = 16, 4, 256
GROUP_SIZE = N_EXPERTS // N_GROUPS
PLE_DIM = 256
A_COLS = 4 * BRANCH_W
B_COLS = 3 * BRANCH_W + W_LORA + A_LORA + G_LORA
C_COLS = Q_LORA + KV_LORA + ROPE
D_COLS = 3 * BRANCH_W + N_HEADS
DEPTH_ALPHA_POW = 0.25
LN_EPS = 1e-5
RMS_EPS = 1e-6
RWKV_GN_EPS = 64e-5
MASK_VALUE = -1e30
LOG2E = math.log2(math.e)

LANES = 128
SUBLANES = 8
VMEM_LIMIT_BYTES = 56 * 1024 * 1024

NN = ((1,), (0,))
NT = ((1,), (1,))
TN = ((0,), (0,))


def _dg(a, b, dims=NN):
    return lax.dot_general(a, b, (dims, ((), ())), preferred_element_type=F32)


def _bdot(a, b, dims=NN):
    return _dg(a.astype(BF16), b.astype(BF16), dims)


def _split(x, pieces):
    out = []
    r = x
    for i in range(pieces):
        p = r.astype(BF16)
        out.append(p)
        if i + 1 < pieces:
            r = r - p.astype(F32)
    return out


def _dot3(a, b, dims=NN):
    ah, al = _split(a, 2)
    bh, bl = _split(b, 2)
    return _dg(ah, bh, dims) + (_dg(ah, bl, dims) + _dg(al, bh, dims))


def _dotc(a, c, dims=NN, pieces=3):
    ps = _split(a, pieces)
    acc = _dg(ps[0], c, dims)
    for p in ps[1:]:
        acc = acc + _dg(p, c, dims)
    return acc


def _iota(shape, dim):
    return lax.broadcasted_iota(jnp.int32, shape, dim)


def _tri_incl(n, dtype=BF16):
    return (_iota((n, n), 0) >= _iota((n, n), 1)).astype(dtype)


def _head_indicator():
    r = _iota((BRANCH_W, BRANCH_W), 0) // HEAD_DIM
    c = _iota((BRANCH_W, BRANCH_W), 1) // HEAD_DIM
    return (r == c).astype(BF16)


def _layer_norm(x, g, b):
    mu = jnp.mean(x, axis=-1, keepdims=True)
    xc = x - mu
    var = jnp.mean(xc * xc, axis=-1, keepdims=True)
    return xc * lax.rsqrt(var + LN_EPS) * g + b


def _softplus(x):
    return jnp.maximum(x, 0.0) + jnp.log(1.0 + jnp.exp(-jnp.abs(x)))


def _silu(x):
    return x * jax.nn.sigmoid(x)


def _cparams(*sem):
    return pltpu.CompilerParams(dimension_semantics=sem, vmem_limit_bytes=VMEM_LIMIT_BYTES)


def _resident(shape):
    nd = len(shape)
    return pl.BlockSpec(shape, lambda *_: (0,) * nd)


TOKEN_TILE = 512
MOE_TOKEN_TILE = 1024
MOE_BLOCK = 256
HGRN_SUB_BLOCK = 16
ATTN_Q_TILE = 512
ATTN_KV_TILE = 512


def _token_tile(t, largest=TOKEN_TILE):
    tm = largest
    while tm >= SUBLANES:
        if t % tm == 0:
            return tm
        tm //= 2
    raise ValueError(f"token count {t} not a multiple of 8")


def _ln_kernel(x_ref, g_ref, b_ref, o_ref):
    o_ref[...] = _layer_norm(x_ref[...], g_ref[...], b_ref[...])


def _ln_call(x, g, b):
    t, d = x.shape
    tm = _token_tile(t)
    return pl.pallas_call(
        _ln_kernel,
        grid=(t // tm,),
        in_specs=[pl.BlockSpec((tm, d), lambda i: (i, 0)), _resident((1, d)), _resident((1, d))],
        out_specs=pl.BlockSpec((tm, d), lambda i: (i, 0)),
        out_shape=jax.ShapeDtypeStruct((t, d), F32),
        compiler_params=_cparams("parallel"),
        name="ln_in",
    )(x, g.reshape(1, d), b.reshape(1, d))


_IN_SLOTS = (("a", A_COLS, A_COLS), ("b", B_COLS, B_COLS), ("qlat", Q_LORA, 256), ("kvlat", KV_LORA, 128),
             ("kr", ROPE, 128), ("fq", BRANCH_W, BRANCH_W), ("fk", BRANCH_W, BRANCH_W), ("fv", BRANCH_W, BRANCH_W),
             ("ff", N_HEADS, 128))


def _relayout_w_in(w_in):
    parts, off = [], 0
    for _, width, slot in _IN_SLOTS:
        w = w_in[:, off:off + width]
        if slot > width:
            w = jnp.pad(w, ((0, 0), (0, slot - width)))
        parts.append(w)
        off += width
    assert off == w_in.shape[1]
    return jnp.concatenate(parts, axis=1).astype(BF16)


def _in_kernel(x_ref, w_ref, *o_refs):
    xb = x_ref[...].astype(BF16)
    off = 0
    for o_ref, (_, _, slot) in zip(o_refs, _IN_SLOTS):
        o_ref[...] = _dg(xb, w_ref[:, off:off + slot])
        off += slot


def _in_call(x, w_in_p):
    t, d = x.shape
    tm = _token_tile(t)
    n = w_in_p.shape[1]
    return pl.pallas_call(
        _in_kernel,
        grid=(t // tm,),
        in_specs=[pl.BlockSpec((tm, d), lambda i: (i, 0)), _resident((d, n))],
        out_specs=[pl.BlockSpec((tm, slot), lambda i: (i, 0)) for _, _, slot in _IN_SLOTS],
        out_shape=[jax.ShapeDtypeStruct((t, slot), F32) for _, _, slot in _IN_SLOTS],
        compiler_params=_cparams("parallel"),
        name="in_proj",
    )(x, w_in_p)


def _hgrn_kernel(c_ref, lb_ref, ng_ref, s0_ref, o_ref, s_ref, st_scr, *, chunk, group):
    j = pl.program_id(1)
    c = chunk
    w = BRANCH_W
    hd = HEAD_DIM
    nh = N_HEADS
    sb = min(HGRN_SUB_BLOCK, c)
    seqs = range(group)

    @pl.when(j == 0)
    def _():
        st_scr[...] = s0_ref[...]

    tri = _tri_incl(c)
    ind = _head_indicator()
    st_keep = (_iota((w, w), 0) // hd) == (_iota((w, w), 1) // hd)
    hr_keep = (_iota((nh * sb, w), 0) // sb) == (_iota((nh * sb, w), 1) // hd)
    row8 = _iota((SUBLANES, w), 0)
    lb = lb_ref[...]

    sq = []
    for bi in seqs:
        cols = c_ref[bi]
        q, fz, iv, g = cols[:, :w], cols[:, w:2 * w], cols[:, 2 * w:3 * w], cols[:, 3 * w:]
        f = lb + (1.0 - lb) * jax.nn.sigmoid(fz)
        a = _dotc(tri, jnp.log(f) * LOG2E, pieces=3)
        sq.append(dict(q=q, k=1.0 - f, iv=iv, g=g, a=a, a_last=a[c - 1:c, :], outs=[]))

    for bi, s in zip(seqs, sq):
        st = st_scr[bi]
        s["o"] = _bdot(s["q"] * jnp.exp2(s["a"]), st, NT)
        kd = s["k"] * jnp.exp2(s["a_last"] - s["a"])
        st_scr[bi] = st * jnp.exp2(s["a_last"]) + jnp.where(st_keep, _bdot(s["iv"], kd, TN), 0.0)

    for tb in range(c // sb):
        lo = tb * sb
        for s in sq:
            q, k, iv, a = s["q"], s["k"], s["iv"], s["a"]
            parts = []
            for rg in range(sb // SUBLANES):
                r0 = lo + rg * SUBLANES
                q8, a8 = q[r0:r0 + SUBLANES], a[r0:r0 + SUBLANES]
                pieces = []
                for src in range(lo, r0 + SUBLANES):
                    d = a8 - a[src:src + 1]
                    if src >= r0:
                        d = jnp.where(row8 >= src - r0, d, MASK_VALUE)
                    pieces.append(q8 * jnp.exp2(d) * k[src:src + 1])
                e = _dotc(jnp.concatenate(pieces, axis=0), ind, pieces=1)
                acc = e[0:SUBLANES] * iv[lo:lo + 1]
                for i in range(1, r0 + SUBLANES - lo):
                    acc = acc + e[i * SUBLANES:(i + 1) * SUBLANES] * iv[lo + i:lo + i + 1]
                parts.append(acc)
            o_tb = jnp.concatenate(parts, axis=0)
            if tb > 0:
                a_lo = a[lo:lo + 1]
                qh = q[lo:lo + sb] * jnp.exp2(a[lo:lo + sb] - a_lo)
                kh = k[:lo] * jnp.exp2(a_lo - a[:lo])
                qh_rows = jnp.where(hr_keep, jnp.concatenate([qh] * nh, axis=0), 0.0)
                sc = _bdot(qh_rows, kh, NT)
                ov = jnp.where(hr_keep, _bdot(sc, iv[:lo]), 0.0)
                for h in range(nh):
                    o_tb = o_tb + ov[h * sb:(h + 1) * sb]
            s["outs"].append(o_tb)

    for bi, s in zip(seqs, sq):
        o = s["o"] + jnp.concatenate(s["outs"], axis=0)
        ms = _dotc(o * o, ind, pieces=2) * (1.0 / hd)
        o = o * lax.rsqrt(ms + RMS_EPS) * ng_ref[...] * _silu(s["g"])
        o_ref[bi] = o.astype(o_ref.dtype)

    @pl.when(j == pl.num_programs(1) - 1)
    def _():
        s_ref[...] = st_scr[...]


def _hgrn_call(cols_a, lb, norm_g, st0):
    b, l, _ = cols_a.shape
    c = CHUNK if l % CHUNK == 0 else l
    w = BRANCH_W
    bg = _batch_group(b)
    return pl.pallas_call(
        functools.partial(_hgrn_kernel, chunk=c, group=bg),
        grid=(b // bg, l // c),
        in_specs=[pl.BlockSpec((bg, c, A_COLS), lambda i, j: (i, j, 0)), _resident((1, w)), _resident((1, w)),
                  pl.BlockSpec((bg, w, w), lambda i, j: (i, 0, 0))],
        out_specs=[pl.BlockSpec((bg, c, w), lambda i, j: (i, j, 0)),
                   pl.BlockSpec((bg, w, w), lambda i, j: (i, 0, 0))],
        out_shape=[jax.ShapeDtypeStruct((b, l, w), BF16), jax.ShapeDtypeStruct((b, w, w), F32)],
        scratch_shapes=[pltpu.VMEM((bg, w, w), F32)],
        compiler_params=_cparams("parallel", "arbitrary"),
        name="hgrn",
    )(cols_a, lb.reshape(1, w), norm_g.reshape(1, w), st0)


def _hgrn_state_to_blockdiag(s):
    b = s.shape[0]
    out = jnp.zeros((b, N_HEADS, HEAD_DIM, N_HEADS, HEAD_DIM), F32)
    for h in range(N_HEADS):
        out = out.at[:, h, :, h, :].set(jnp.swapaxes(s[:, h], -1, -2))
    return out.reshape(b, BRANCH_W, BRANCH_W)


def _hgrn_state_from_blockdiag(st):
    hs = [st[:, h * HEAD_DIM:(h + 1) * HEAD_DIM, h * HEAD_DIM:(h + 1) * HEAD_DIM] for h in range(N_HEADS)]
    return jnp.swapaxes(jnp.stack(hs, axis=1), -1, -2)


def _rwkv_tdot(a, b, dims=NN):
    return _bdot(a, b, dims)


def _rwkv_sdot(a, b, dims=NN):
    return _dot3(a, b, dims)


def _rwkv_operands(cols, prev, mu_ref, w0_ref, lora_ref, a0_ref, kk_ref, ka_ref, rk_ref):
    shifted = jnp.where(_iota(cols.shape, 0) == 0, prev, pltpu.roll(cols, 1, 0))
    m = cols + (shifted - cols) * mu_ref[...]
    w = BRANCH_W
    r, k, v = m[:, :w], m[:, w:2 * w], m[:, 2 * w:3 * w]
    slab = m[:, 3 * w:]
    lane = _iota(slab.shape, 1)
    act = jnp.where(lane < W_LORA, jnp.tanh(slab),
                    jnp.where(lane < W_LORA + A_LORA, slab, jax.nn.sigmoid(slab)))
    lora = _bdot(act, lora_ref[...])
    w_log = -_softplus(-(w0_ref[...] + lora[:, :w])) - 0.5
    a_rate = jax.nn.sigmoid(a0_ref[...] + lora[:, w:2 * w])
    g = lora[:, 2 * w:]
    ind = _head_indicator()
    kk = k * kk_ref[...]
    norm = jnp.sqrt(_dotc(kk * kk, ind, pieces=3))
    kk = kk / jnp.maximum(norm, 1e-12)
    kh = k * (1.0 + (a_rate - 1.0) * ka_ref[...])
    return dict(r=r, lw=-jnp.exp(w_log), k=kh, v=v, a=-kk, b=kk * a_rate, g=g,
                bonus=_dotc(r * kh * rk_ref[...], ind, pieces=3) * v)


def _rwkv_chunk_kernel(c_ref, prow_ref, mu_ref, w0_ref, lora_ref, a0_ref, kk_ref, ka_ref, rk_ref, lg_ref, lb_ref,
                       s0_ref, o_ref, s_ref, st_scr, prev_scr, *, chunk, group):
    j = pl.program_id(1)
    c = chunk
    w = BRANCH_W
    hd = HEAD_DIM
    nh = N_HEADS
    n = nh * c
    seqs = range(group)

    @pl.when(j == 0)
    def _():
        st_scr[...] = jnp.zeros(st_scr.shape, F32)
        prev_scr[...] = prow_ref[...]
        for bi in seqs:
            for h in range(nh):
                st_scr[bi, h * hd:(h + 1) * hd, h * hd:(h + 1) * hd] = s0_ref[bi, h]

    tri = _tri_incl(c)
    ind = _head_indicator()
    t_w = _iota((c, n), 0)
    s_w = _iota((c, n), 1) % c
    strict_w = t_w > s_w
    incl_w = t_w >= s_w
    eye_w = (t_w == s_w).astype(F32)
    bd_keep = (_iota((n, n), 0) // c) == (_iota((n, n), 1) // c)
    hr_keep = (_iota((n, w), 0) // c) == (_iota((n, w), 1) // hd)
    st_keep = (_iota((w, w), 0) // hd) == (_iota((w, w), 1) // hd)

    def bd(x):
        return jnp.where(bd_keep, jnp.concatenate([x] * nh, axis=0), 0.0)

    def head_rows(x):
        return jnp.where(hr_keep, jnp.concatenate([x] * nh, axis=0), 0.0)

    sq = []
    for bi in seqs:
        cols = c_ref[bi]
        ops = _rwkv_operands(cols, prev_scr[bi], mu_ref, w0_ref, lora_ref, a0_ref, kk_ref, ka_ref, rk_ref)
        prev_scr[bi] = cols[c - 1:c]
        r, lw, k, v, a, b = (ops[name] for name in ("r", "lw", "k", "v", "a", "b"))
        cum = _dotc(tri, lw, pieces=3)
        last = cum[c - 1:c, :]
        e_neg = jnp.exp(-cum)
        e_last = jnp.exp(last - cum)
        sq.append(dict(v=v, last=last, g=ops["g"], bonus=ops["bonus"],
                       lhs=jnp.concatenate([a * jnp.exp(cum - lw), r * jnp.exp(cum)], axis=0),
                       bt=b * e_neg, kt=k * e_neg, hat=jnp.concatenate([b * e_last, k * e_last], axis=0)))

    for q in sq:
        pb = _rwkv_tdot(q["lhs"], head_rows(q["bt"]), NT)
        pk = _rwkv_tdot(q["lhs"], head_rows(q["kt"]), NT)
        q["l_ab"] = jnp.where(strict_w, pb[:c], 0.0)
        q["a_rb"] = jnp.where(incl_w, pb[c:], 0.0)
        q["lk"] = jnp.concatenate([jnp.where(strict_w, pk[:c], 0.0), jnp.where(incl_w, pk[c:], 0.0)], axis=0)
        q["tinv"] = eye_w + q["l_ab"]
        q["p"] = q["l_ab"]

    for _ in range(int(math.log2(c)) - 1):
        for q in sq:
            q["p"] = _rwkv_tdot(q["p"], bd(q["p"]))
        for q in sq:
            q["tinv"] = q["tinv"] + _rwkv_tdot(q["tinv"], bd(q["p"]))

    for bi, q in zip(seqs, sq):
        q["st"] = st_scr[bi]
        q["sh"] = _rwkv_sdot(q["lhs"], q["st"], NT)
        q["lv"] = _rwkv_tdot(q["lk"], head_rows(q["v"]))
    for q in sq:
        q["u"] = _rwkv_tdot(q["tinv"], head_rows(q["sh"][:c] + q["lv"][:c]))
    for bi, q in zip(seqs, sq):
        upd = _rwkv_sdot(jnp.concatenate([q["u"], q["v"]], axis=0), q["hat"], TN)
        st_scr[bi] = q["st"] * jnp.exp(q["last"]) + jnp.where(st_keep, upd, 0.0)
    for q in sq:
        q["y"] = q["sh"][c:] + _rwkv_tdot(q["a_rb"], head_rows(q["u"])) + q["lv"][c:]

    for bi, q in zip(seqs, sq):
        y = q["y"]
        mu = _dotc(y, ind, pieces=3) * (1.0 / hd)
        yc = y - mu
        var = _dotc(yc * yc, ind, pieces=2) * (1.0 / hd)
        yn = yc * lax.rsqrt(var + RWKV_GN_EPS) * lg_ref[...] + lb_ref[...]
        o_ref[bi] = ((yn + q["bonus"]) * q["g"]).astype(o_ref.dtype)

    @pl.when(j == pl.num_programs(1) - 1)
    def _():
        for bi in seqs:
            for h in range(nh):
                s_ref[bi, h] = st_scr[bi, h * hd:(h + 1) * hd, h * hd:(h + 1) * hd]


def _batch_group(b):
    return 4 if b % 4 == 0 else 1


def _rwkv_call(cols_b, prev_row, s0, mu, w0, w2, a0, a2, g2, kk_s, ka, rk, lnx_g, lnx_b):
    b, l, nb = cols_b.shape
    w = BRANCH_W
    c = CHUNK if l % CHUNK == 0 else l
    bg = _batch_group(b)
    row = lambda x: x.reshape(1, -1)
    lora_w = jnp.zeros((LANES, 3 * w), F32)
    lora_w = lora_w.at[:W_LORA, :w].set(w2).at[W_LORA:W_LORA + A_LORA, w:2 * w].set(a2)
    lora_w = lora_w.at[W_LORA + A_LORA:, 2 * w:].set(g2).astype(BF16)
    st_spec = pl.BlockSpec((bg, N_HEADS, HEAD_DIM, HEAD_DIM), lambda i, j: (i, 0, 0, 0))
    return pl.pallas_call(
        functools.partial(_rwkv_chunk_kernel, chunk=c, group=bg),
        grid=(b // bg, l // c),
        in_specs=[pl.BlockSpec((bg, c, nb), lambda i, j: (i, j, 0)), pl.BlockSpec((bg, 1, nb), lambda i, j: (i, 0, 0)),
                  _resident((1, nb)), _resident((1, w)), _resident((LANES, 3 * w))] + [_resident((1, w))] * 6
                 + [st_spec],
        out_specs=[pl.BlockSpec((bg, c, w), lambda i, j: (i, j, 0)), st_spec],
        out_shape=[jax.ShapeDtypeStruct((b, l, w), BF16),
                   jax.ShapeDtypeStruct((b, N_HEADS, HEAD_DIM, HEAD_DIM), F32)],
        scratch_shapes=[pltpu.VMEM((bg, w, w), F32), pltpu.VMEM((bg, 1, nb), F32)],
        compiler_params=_cparams("parallel", "arbitrary"),
        name="rwkv",
    )(cols_b, prev_row.reshape(b, 1, nb), row(mu), row(w0), lora_w, row(a0), row(kk_s), row(ka), row(rk),
      row(lnx_g), row(lnx_b), s0)


def _rope_tables(p, l):
    half = ROPE // 2
    inv = 1.0 / (ROPE_THETA ** (jnp.arange(half, dtype=F32) / half))
    ang = (p + jnp.arange(l, dtype=jnp.int32)).astype(F32)[:, None] * inv[None, :]
    cos, sin = jnp.cos(ang), jnp.sin(ang)
    pad = jnp.zeros((l, LANES - ROPE), F32)
    return jnp.concatenate([cos, cos, pad], axis=1), jnp.concatenate([-sin, sin, pad], axis=1)


def _swap_halves(x, base):
    half = ROPE // 2
    n = x.shape[-1]
    lane = _iota(x.shape, x.ndim - 1) % LANES
    up = pltpu.roll(x, n - half, x.ndim - 1)
    down = pltpu.roll(x, half, x.ndim - 1)
    return jnp.where((lane >= base) & (lane < base + half), up,
                     jnp.where((lane >= base + half) & (lane < base + ROPE), down, 0.0))


def _rms_norm(x, g, width):
    ms = jnp.sum(x * x, axis=-1, keepdims=True) * (1.0 / width)
    return x * lax.rsqrt(ms + RMS_EPS) * g


def _mla_q_kernel(ql_ref, kvl_ref, kr_ref, cos_ref, sin_ref, qg_ref, wuq_ref, kvg_ref, q_ref, c_ref, krn_ref):
    scale = (NOPE + ROPE) ** -0.5 * LOG2E
    qn = _rms_norm(ql_ref[0], qg_ref[...], Q_LORA)
    q = _bdot(qn, wuq_ref[...])
    cos, sin = cos_ref[...], sin_ref[...]
    lane = _iota(cos.shape, 1)
    cos_q = jnp.where(lane < NOPE, 1.0, pltpu.roll(cos, NOPE, 1))
    sin_q = pltpu.roll(sin, NOPE, 1)
    for h in range(N_HEADS):
        qh = q[:, h * LANES:(h + 1) * LANES]
        qh = qh * cos_q + _swap_halves(qh, NOPE) * sin_q
        q_ref[0, h] = (qh * scale).astype(q_ref.dtype)
    c_ref[0] = _rms_norm(kvl_ref[0], kvg_ref[...], KV_LORA)
    kr = kr_ref[0]
    krn = kr * cos + _swap_halves(kr, 0) * sin
    krn_ref[0] = krn[:, :ROPE]


def _mla_q_call(qlat, kvlat, kr, cos, sin, qn_g, w_uq_p, kvn_g):
    b, l, _ = qlat.shape
    tm = _token_tile(l)
    tok = lambda n: pl.BlockSpec((1, tm, n), lambda i, j: (i, j, 0))
    tab = pl.BlockSpec((tm, LANES), lambda i, j: (j, 0))
    return pl.pallas_call(
        _mla_q_kernel,
        grid=(b, l // tm),
        in_specs=[tok(256), tok(KV_LORA), tok(LANES), tab, tab, _resident((1, 256)),
                  _resident((256, N_HEADS * LANES)), _resident((1, KV_LORA))],
        out_specs=[pl.BlockSpec((1, N_HEADS, tm, LANES), lambda i, j: (i, 0, j, 0)), tok(KV_LORA), tok(ROPE)],
        out_shape=[jax.ShapeDtypeStruct((b, N_HEADS, l, LANES), BF16), jax.ShapeDtypeStruct((b, l, KV_LORA), F32),
                   jax.ShapeDtypeStruct((b, l, ROPE), F32)],
        compiler_params=_cparams("parallel", "parallel"),
        name="mla_q",
    )(qlat, kvlat, kr, cos, sin, qn_g, w_uq_p, kvn_g)


def _mla_kv_kernel(c_ref, kr_ref, wk_ref, wv_ref, k_ref, v_ref):
    cb = c_ref[0].astype(BF16)
    kall = _dg(cb, wk_ref[...])
    vall = _dg(cb, wv_ref[...])
    tm = cb.shape[0]
    place = (_iota((ROPE, LANES), 0) + NOPE == _iota((ROPE, LANES), 1)).astype(BF16)
    kr_slot = _dg(kr_ref[0].astype(BF16), place)
    one_slot = (_iota((tm, LANES), 1) == HEAD_DIM).astype(F32)
    for h in range(N_HEADS):
        k_ref[0, h] = (kall[:, h * LANES:(h + 1) * LANES] + kr_slot).astype(k_ref.dtype)
        v_ref[0, h, 0] = (vall[:, h * LANES:(h + 1) * LANES] + one_slot).T.astype(v_ref.dtype)


def _kv_specs(b, t, tm):
    kspec = pl.BlockSpec((1, N_HEADS, tm, LANES), lambda i, j: (i, 0, j, 0))
    vspec = pl.BlockSpec((1, N_HEADS, 1, LANES, tm), lambda i, j: (i, 0, j, 0, 0))
    shapes = [jax.ShapeDtypeStruct((b, N_HEADS, t, LANES), BF16),
              jax.ShapeDtypeStruct((b, N_HEADS, t // tm, LANES, tm), BF16)]
    return [kspec, vspec], shapes


def _mla_kv_call(c_all, kr_all, w_k_p, w_v_p):
    b, t, _ = c_all.shape
    tm = _token_tile(t, ATTN_KV_TILE)
    tok = lambda n: pl.BlockSpec((1, tm, n), lambda i, j: (i, j, 0))
    out_specs, out_shape = _kv_specs(b, t, tm)
    return pl.pallas_call(
        _mla_kv_kernel,
        grid=(b, t // tm),
        in_specs=[tok(KV_LORA), tok(ROPE), _resident((KV_LORA, N_HEADS * LANES)),
                  _resident((KV_LORA, N_HEADS * LANES))],
        out_specs=out_specs,
        out_shape=out_shape,
        compiler_params=_cparams("parallel", "parallel"),
        name="mla_kv",
    )(c_all, kr_all, w_k_p, w_v_p)


def _fox_lf_kernel(ff_ref, bf_ref, lf_ref, lfw_ref):
    z = ff_ref[0] + bf_ref[...]
    lf = jnp.where(_iota(z.shape, 1) < N_HEADS, -_softplus(-z), 0.0)
    lfw_ref[0] = lf
    lf_ref[0] = lf[:, :N_HEADS]


def _fox_lf_call(ff, bf):
    b, l, _ = ff.shape
    tm = _token_tile(l)
    return pl.pallas_call(
        _fox_lf_kernel,
        grid=(b, l // tm),
        in_specs=[pl.BlockSpec((1, tm, LANES), lambda i, j: (i, j, 0)), _resident((1, LANES))],
        out_specs=[pl.BlockSpec((1, tm, N_HEADS), lambda i, j: (i, j, 0)),
                   pl.BlockSpec((1, tm, LANES), lambda i, j: (i, j, 0))],
        out_shape=[jax.ShapeDtypeStruct((b, l, N_HEADS), F32), jax.ShapeDtypeStruct((b, l, LANES), F32)],
        compiler_params=_cparams("parallel", "parallel"),
        name="fox_lf",
    )(ff, jnp.pad(bf, (0, LANES - N_HEADS)).reshape(1, LANES))


def _head_slot(x, h):
    pair = x[:, (h // 2) * LANES:(h // 2 + 1) * LANES]
    return pair if h % 2 == 0 else pltpu.roll(pair, HEAD_DIM, 1)


def _bias_lanes(c_col, first, ones_first):
    tm = c_col.shape[0]
    lane = _iota((tm, LANES), 1)
    p0 = c_col.astype(BF16).astype(F32)
    r1 = c_col - p0
    p1 = r1.astype(BF16).astype(F32)
    p2 = r1 - p1
    out = jnp.where(lane == first, p0, jnp.where(lane == first + 1, p1, jnp.where(lane == first + 2, p2, 0.0)))
    return jnp.where((lane >= ones_first) & (lane < ones_first + 3), 1.0, out)


def _fox_q_kernel(q_ref, c_ref, o_ref):
    scale = HEAD_DIM ** -0.5 * LOG2E
    q = q_ref[0]
    c = c_ref[0] * LOG2E
    lane = _iota((q.shape[0], LANES), 1)
    for h in range(N_HEADS):
        bias = _bias_lanes(c[:, h:h + 1], HEAD_DIM, HEAD_DIM + 3)
        o_ref[0, h] = jnp.where(lane < HEAD_DIM, _head_slot(q, h) * scale, bias).astype(o_ref.dtype)


def _fox_kv_kernel(k_ref, v_ref, lf_ref, ko_ref, vo_ref, c_ref, carry):
    @pl.when(pl.program_id(1) == 0)
    def _():
        carry[...] = jnp.zeros_like(carry)

    k, v, lf = k_ref[0], v_ref[0], lf_ref[0]
    tm = k.shape[0]
    c = _dotc(_tri_incl(tm), lf, pieces=3) + carry[...]
    c_ref[0] = c
    carry[...] = c[tm - 1:, :]
    c = c * LOG2E
    lane = _iota((tm, LANES), 1)
    for h in range(N_HEADS):
        bias = _bias_lanes(-c[:, h:h + 1], HEAD_DIM + 3, HEAD_DIM)
        ko_ref[0, h] = jnp.where(lane < HEAD_DIM, _head_slot(k, h), bias).astype(ko_ref.dtype)
        vo_ref[0, h, 0] = jnp.where(lane < HEAD_DIM, _head_slot(v, h),
                                    (lane == HEAD_DIM).astype(F32)).T.astype(vo_ref.dtype)


def _fox_q_call(q, c_new):
    b, l, w = q.shape
    tm = _token_tile(l)
    return pl.pallas_call(
        _fox_q_kernel,
        grid=(b, l // tm),
        in_specs=[pl.BlockSpec((1, tm, w), lambda i, j: (i, j, 0)),
                  pl.BlockSpec((1, tm, LANES), lambda i, j: (i, j, 0))],
        out_specs=pl.BlockSpec((1, N_HEADS, tm, LANES), lambda i, j: (i, 0, j, 0)),
        out_shape=jax.ShapeDtypeStruct((b, N_HEADS, l, LANES), BF16),
        compiler_params=_cparams("parallel", "parallel"),
        name="fox_q",
    )(q, c_new)


def _fox_kv_call(k_all, v_all, lf_all):
    b, t, w = k_all.shape
    tm = _token_tile(t, ATTN_KV_TILE)
    tok = pl.BlockSpec((1, tm, w), lambda i, j: (i, j, 0))
    wide = pl.BlockSpec((1, tm, LANES), lambda i, j: (i, j, 0))
    out_specs, out_shape = _kv_specs(b, t, tm)
    return pl.pallas_call(
        _fox_kv_kernel,
        grid=(b, t // tm),
        in_specs=[tok, tok, wide],
        out_specs=out_specs + [wide],
        out_shape=out_shape + [jax.ShapeDtypeStruct((b, t, LANES), F32)],
        scratch_shapes=[pltpu.VMEM((1, LANES), F32)],
        compiler_params=_cparams("parallel", "arbitrary"),
        name="fox_kv",
    )(k_all, v_all, lf_all)


def _flash_kernel(q_ref, k_ref, v_ref, o_ref, m_scr, acc_scr, s_scr, *, tq, tk, past, t_valid, chunk_mask):
    qi = pl.program_id(1)
    q_start = past + qi * tq
    if chunk_mask:
        vis_end = jnp.minimum(((q_start + tq + CHUNK - 1) // CHUNK) * CHUNK, t_valid)
    else:
        vis_end = q_start + tq
    n_blocks = (vis_end + tk - 1) // tk
    n_full = q_start // tk

    m_scr[...] = jnp.full(m_scr.shape, MASK_VALUE, F32)
    acc_scr[...] = jnp.zeros(acc_scr.shape, F32)

    def logits(h, kb):
        ks = pl.multiple_of(kb * tk, tk)
        return _dg(k_ref[0, h, pl.ds(ks, tk), :], q_ref[0, h], NT)

    s_scr[...] = logits(0, 0)

    def block(kb, masked, s_next):
        if masked:
            kpos = kb * tk + _iota((tk, tq), 0)
            qpos = q_start + _iota((tk, tq), 1)
            if chunk_mask:
                keep = ((kpos // CHUNK) <= (qpos // CHUNK)) & (kpos < t_valid)
            else:
                keep = kpos <= qpos
        for h in range(N_HEADS):
            s = jnp.where(keep, s_next, MASK_VALUE) if masked else s_next
            if h + 1 < N_HEADS:
                s_next = logits(h + 1, kb)
            else:
                s_next = logits(0, jnp.minimum(kb + 1, n_blocks - 1))
            m_old = m_scr[h]
            m_new = jnp.maximum(m_old, jnp.max(s, axis=0, keepdims=True))
            p = jnp.exp2(s - m_new[:1])
            pv = _dg(v_ref[0, h, kb], p.astype(BF16))
            acc_scr[h] = acc_scr[h] * jnp.exp2(m_old - m_new)[:1] + pv
            m_scr[h] = m_new
        return s_next

    def full_body(kb, carry):
        s_scr[...] = block(kb, False, s_scr[...])
        return carry

    def masked_body(kb, carry):
        s_scr[...] = block(kb, True, s_scr[...])
        return carry

    lax.fori_loop(0, n_full, full_body, 0)
    lax.fori_loop(n_full, n_blocks, masked_body, 0)
    for h in range(N_HEADS):
        acc = acc_scr[h].T
        o = acc[:, :HEAD_DIM] / acc[:, HEAD_DIM:HEAD_DIM + 1]
        o_ref[0, :, h * HEAD_DIM:(h + 1) * HEAD_DIM] = o.astype(o_ref.dtype)


def _flash_call(q, k, v, *, past, t_valid, chunk_mask, name):
    b, nh, l, _ = q.shape
    t = k.shape[2]
    tk = v.shape[-1]
    lq = max(l, LANES)
    if lq != l:
        q = jnp.pad(q, ((0, 0), (0, 0), (0, lq - l), (0, 0)))
    tq = _token_tile(lq, ATTN_Q_TILE)
    out = pl.pallas_call(
        functools.partial(_flash_kernel, tq=tq, tk=tk, past=past, t_valid=t_valid, chunk_mask=chunk_mask),
        grid=(b, lq // tq),
        in_specs=[pl.BlockSpec((1, nh, tq, LANES), lambda i, j: (i, 0, j, 0)),
                  pl.BlockSpec((1, nh, t, LANES), lambda i, j: (i, 0, 0, 0)),
                  pl.BlockSpec((1, nh, t // tk, LANES, tk), lambda i, j: (i, 0, 0, 0, 0))],
        out_specs=pl.BlockSpec((1, tq, BRANCH_W), lambda i, j: (i, j, 0)),
        out_shape=jax.ShapeDtypeStruct((b, lq, BRANCH_W), BF16),
        scratch_shapes=[pltpu.VMEM((nh, SUBLANES, tq), F32), pltpu.VMEM((nh, LANES, tq), F32),
                        pltpu.VMEM((tk, tq), F32)],
        compiler_params=_cparams("parallel", "arbitrary"),
        name=name,
    )(q, k, v)
    return out[:, :l]


def _mix_kernel(x_ref, oa_ref, ob_ref, oc_ref, od_ref, wmg_ref, wbr_ref, wo_ref, g_ref, b_ref, o_ref, *, alpha):
    x = x_ref[...]
    xb = x.astype(BF16)
    d = D_MODEL
    acc = None
    for n, br_ref in enumerate((oa_ref, ob_ref, oc_ref, od_ref)):
        gate = jax.nn.sigmoid(_dg(xb, wmg_ref[:, n * d:(n + 1) * d]))
        term = gate * _dg(br_ref[...], wbr_ref[n])
        acc = term if acc is None else acc + term
    mix = _dg(acc.astype(BF16), wo_ref[...])
    o_ref[...] = _layer_norm(alpha * x + mix, g_ref[...], b_ref[...])


def _mix_call(x, oa, ob, oc, od, w_mg, w_br, w_o, g, b, alpha):
    t, d = x.shape
    tm = _token_tile(t)
    tok = lambda n: pl.BlockSpec((tm, n), lambda i: (i, 0))
    return pl.pallas_call(
        functools.partial(_mix_kernel, alpha=alpha),
        grid=(t // tm,),
        in_specs=[tok(d)] + [tok(BRANCH_W)] * 4 + [_resident(w_mg.shape), _resident(w_br.shape),
                                                   _resident(w_o.shape), _resident((1, d)), _resident((1, d))],
        out_specs=tok(d),
        out_shape=jax.ShapeDtypeStruct((t, d), F32),
        compiler_params=_cparams("parallel"),
        name="mix_ln1",
    )(x, oa, ob, oc, od, w_mg, w_br, w_o, g.reshape(1, d), b.reshape(1, d))


def _route(x, rw_ref, rb_ref):
    scores = jax.nn.sigmoid(_dot3(rw_ref[...], x, NT)[:N_EXPERTS])
    biased = scores + rb_ref[...][:N_EXPERTS]
    col = [biased[e:e + 1] for e in range(N_EXPERTS)]
    gs = []
    for g in range(N_GROUPS):
        v = col[g * GROUP_SIZE:(g + 1) * GROUP_SIZE]
        best = None
        for i in range(GROUP_SIZE):
            for j in range(i + 1, GROUP_SIZE):
                s = v[i] + v[j]
                best = s if best is None else jnp.maximum(best, s)
        gs.append(best)
    gates, sels = [], []
    for g in range(N_GROUPS):
        sel = None
        for o in range(N_GROUPS):
            if o == g:
                continue
            cond = (gs[g] > gs[o]) if o < g else (gs[g] >= gs[o])
            sel = cond if sel is None else sel & cond
        sels.append(sel)
        v = col[g * GROUP_SIZE:(g + 1) * GROUP_SIZE]
        for i in range(GROUP_SIZE):
            rank = None
            for j in range(GROUP_SIZE):
                if j == i:
                    continue
                ahead = (v[j] >= v[i]) if j < i else (v[j] > v[i])
                ahead = ahead.astype(F32)
                rank = ahead if rank is None else rank + ahead
            e = g * GROUP_SIZE + i
            gates.append(jnp.where(sel & (rank < 2.0), scores[e:e + 1], 0.0))
    total = gates[0]
    for gt in gates[1:]:
        total = total + gt
    return [gt / total for gt in gates], sels


MOE_POS_LANE = N_EXPERTS
MOE_ROW_FORM = 24
MOE_META_GROUP = 32
MOE_META_COUNT = 64


def _moe_route_kernel(x_ref, rw_ref, rb_ref, gp_ref, posr_ref, meta_ref, *, rows):
    tm = x_ref.shape[0]
    gates, sels = _route(x_ref[...], rw_ref, rb_ref)
    gsel = jnp.concatenate([s.astype(F32) for s in sels] + [jnp.zeros((SUBLANES - N_GROUPS, tm), F32)], axis=0)
    before = (_iota((tm, tm), 0) < _iota((tm, tm), 1)).astype(BF16)
    rank = _dg(gsel.astype(BF16), before)
    count = jnp.sum(gsel, axis=1, keepdims=True)
    offs = [jnp.zeros((1, 1), F32)]
    for g in range(1, N_GROUPS):
        offs.append(offs[-1] + count[g - 1:g])
    pos = gsel[0:1] * rank[0:1]
    for g in range(1, N_GROUPS):
        pos = pos + gsel[g:g + 1] * (offs[g] + rank[g:g + 1])
    gt = jnp.concatenate(gates + [pos, jnp.zeros((LANES - N_EXPERTS - 1, tm), F32)], axis=0)
    posr_ref[0] = gt[:MOE_ROW_FORM]
    gp_ref[...] = gt.T

    lane = _iota((1, LANES), 1)
    lane_g = lane - MOE_META_GROUP
    shift = int(math.log2(rows))
    meta = jnp.zeros((1, LANES), jnp.int32)
    start = jnp.zeros((1, 1), jnp.int32)
    for g in range(N_GROUPS):
        n_g = count[g:g + 1].astype(jnp.int32)
        off_g = offs[g].astype(jnp.int32)
        first = lax.shift_right_arithmetic(off_g, shift)
        last = lax.shift_right_arithmetic(off_g + n_g - 1, shift)
        k_g = jnp.where(n_g > 0, last - first + 1, 0)
        meta = jnp.where((lane >= start) & (lane < start + k_g), first + (lane - start), meta)
        meta = jnp.where((lane_g >= start) & (lane_g < start + k_g), g, meta)
        start = start + k_g
    meta_ref[0] = jnp.where(lane == MOE_META_COUNT, start, meta)


def _moe_sparse_kernel(meta_ref, x_ref, gp_ref, posr_ref, wg_ref, wu_ref, wd_ref, g_ref, b_ref, o_ref,
                       xb_scr, acc_scr, xs_scr, gs_scr, y_scr, back_scr, *, alpha, rows):
    i = pl.program_id(0)
    w = pl.program_id(1)
    tm = x_ref.shape[0]
    blk = meta_ref[i, w]
    grp = meta_ref[i, MOE_META_GROUP + w]
    n_items = meta_ref[i, MOE_META_COUNT]
    valid = w < n_items
    first_of_blk = valid & ((w == 0) | (meta_ref[i, jnp.maximum(w - 1, 0)] != blk))
    last_of_blk = valid & ((w + 1 >= n_items) | (meta_ref[i, w + 1] != blk))
    base = (blk * rows).astype(F32)

    @pl.when(w == 0)
    def _():
        xb_scr[...] = x_ref[...].astype(BF16)
        acc_scr[...] = jnp.zeros(acc_scr.shape, F32)

    @pl.when(first_of_blk)
    def _():
        rowform = posr_ref[0]
        pos_row = rowform[MOE_POS_LANE:MOE_POS_LANE + 1]
        perm = (pos_row == base + _iota((rows, tm), 0).astype(F32)).astype(BF16)
        back = (gp_ref[:, MOE_POS_LANE:MOE_POS_LANE + 1]
                == base + _iota((tm, rows), 1).astype(F32)).astype(BF16)
        back_scr[...] = back
        xs_scr[...] = _dg(perm, xb_scr[...]).astype(BF16)
        gst = _dotc(rowform[:N_EXPERTS], back, pieces=3)
        gst = jnp.concatenate([gst, jnp.zeros((LANES - N_EXPERTS, rows), F32)], axis=0)
        gs_scr[...] = gst.T
        y_scr[...] = jnp.zeros(y_scr.shape, F32)

    @pl.when(valid)
    def _():
        xs = xs_scr[...]
        gs = gs_scr[...]
        lane = _iota(gs.shape, 1)
        y = y_scr[...]
        for e in range(GROUP_SIZE):
            idx = grp * GROUP_SIZE + e
            gate = jnp.sum(jnp.where(lane == idx, gs, 0.0), axis=-1, keepdims=True)
            h = _silu(_dg(xs, wg_ref[idx])) * _dg(xs, wu_ref[idx]) * gate
            y = y + _dg(h.astype(BF16), wd_ref[idx])
        y_scr[...] = y

    @pl.when(last_of_blk)
    def _():
        acc_scr[...] += _dg(back_scr[...], y_scr[...].astype(BF16))

    @pl.when(w == pl.num_programs(1) - 1)
    def _():
        o_ref[...] = _layer_norm(alpha * x_ref[...] + acc_scr[...], g_ref[...], b_ref[...])


def _moe_call(x, router_w, router_b, w_g, w_u, w_d, g, b, alpha):
    t, d = x.shape
    tm = _token_tile(t, MOE_TOKEN_TILE)
    nt = t // tm
    rows = min(MOE_BLOCK, tm)
    tok = pl.BlockSpec((tm, d), lambda i: (i, 0))
    gp, posr, meta = pl.pallas_call(
        functools.partial(_moe_route_kernel, rows=rows),
        grid=(nt,),
        in_specs=[tok, _resident((LANES, d)), _resident((LANES, 1))],
        out_specs=[pl.BlockSpec((tm, LANES), lambda i: (i, 0)), pl.BlockSpec((1, MOE_ROW_FORM, tm), lambda i: (i, 0, 0)),
                   pl.BlockSpec((1, 1, LANES), lambda i: (i, 0, 0))],
        out_shape=[jax.ShapeDtypeStruct((t, LANES), F32), jax.ShapeDtypeStruct((nt, MOE_ROW_FORM, tm), F32),
                   jax.ShapeDtypeStruct((nt, 1, LANES), jnp.int32)],
        compiler_params=_cparams("parallel"),
        name="moe_route",
    )(x, jnp.pad(router_w.T, ((0, LANES - N_EXPERTS), (0, 0))),
      jnp.pad(router_b, (0, LANES - N_EXPERTS)).reshape(LANES, 1))

    n_items = tm // rows + N_GROUPS - 1
    once = lambda shape: pl.BlockSpec(shape, lambda i, w, m: (0,) * len(shape), pipeline_mode=pl.Buffered(1))
    tok2 = lambda n: pl.BlockSpec((tm, n), lambda i, w, m: (i, 0))
    return pl.pallas_call(
        functools.partial(_moe_sparse_kernel, alpha=alpha, rows=rows),
        grid_spec=pltpu.PrefetchScalarGridSpec(
            num_scalar_prefetch=1,
            grid=(nt, n_items),
            in_specs=[tok2(d), tok2(LANES), pl.BlockSpec((1, MOE_ROW_FORM, tm), lambda i, w, m: (i, 0, 0)),
                      once(w_g.shape), once(w_u.shape), once(w_d.shape), once((1, d)), once((1, d))],
            out_specs=tok2(d),
            scratch_shapes=[pltpu.VMEM((tm, d), BF16), pltpu.VMEM((tm, d), F32), pltpu.VMEM((rows, d), BF16),
                            pltpu.VMEM((rows, LANES), F32), pltpu.VMEM((rows, d), F32),
                            pltpu.VMEM((tm, rows), BF16)]),
        out_shape=jax.ShapeDtypeStruct((t, d), F32),
        compiler_params=_cparams("parallel", "arbitrary"),
        name="moe_ln2",
    )(meta.reshape(nt, LANES), x, gp, posr, w_g, w_u, w_d, g.reshape(1, d), b.reshape(1, d))


def _ple_kernel(x_ref, p_ref, pw_ref, gw_ref, g_ref, b_ref, o_ref, *, alpha):
    x = x_ref[...]
    ple = _bdot(p_ref[...], pw_ref[...]) * jax.nn.sigmoid(_bdot(x, gw_ref[...]))
    o_ref[...] = _layer_norm(alpha * x + ple, g_ref[...], b_ref[...])


def _ple_call(x, p, ple_w, gate_w, g, b, alpha):
    t, d = x.shape
    tm = _token_tile(t)
    tok = lambda n: pl.BlockSpec((tm, n), lambda i: (i, 0))
    return pl.pallas_call(
        functools.partial(_ple_kernel, alpha=alpha),
        grid=(t // tm,),
        in_specs=[tok(d), tok(PLE_DIM), _resident(ple_w.shape), _resident(gate_w.shape), _resident((1, d)),
                  _resident((1, d))],
        out_specs=tok(d),
        out_shape=jax.ShapeDtypeStruct((t, d), F32),
        compiler_params=_cparams("parallel"),
        name="ple_ln3",
    )(x, p, ple_w, gate_w, g.reshape(1, d), b.reshape(1, d))


def _lb_kernel(x_ref, o_ref):
    x = x_ref[...]
    depth = x.shape[0]
    m = jnp.max(x, axis=0, keepdims=True)
    e = jnp.exp(x - m)
    pr = e / jnp.sum(e, axis=0, keepdims=True)
    run = jnp.zeros_like(pr[0:1])
    for i in range(depth):
        o_ref[i:i + 1, :] = run
        run = run + pr[i:i + 1]


def _lb_call(logits):
    return pl.pallas_call(
        _lb_kernel,
        out_shape=jax.ShapeDtypeStruct(logits.shape, F32),
        name="hgrn_lb",
    )(logits)


def _pad_rows(x, t):
    pad = t - x.shape[1]
    if pad == 0:
        return x
    return jnp.pad(x, ((0, 0), (0, pad)) + ((0, 0),) * (x.ndim - 2))


def _layer(x, p, st, lw, alpha):
    b, l, d = x.shape
    t = b * l
    hgrn_s, rwkv_s, rwkv_prev, lat_past, kr_past, fk_past, fv_past, flf_past = st
    past = 0 if lat_past is None else lat_past.shape[1]
    t_valid = past + l
    t_pad = -(-t_valid // LANES) * LANES if past else t_valid

    cols = _in_call(x.reshape(t, d), lw["w_in"])
    ca, cb, qlat, kvlat, kr, fq, fk, fv, ff = [c.reshape(b, l, -1) for c in cols]

    st0 = jnp.zeros((b, BRANCH_W, BRANCH_W), F32) if hgrn_s is None else _hgrn_state_to_blockdiag(hgrn_s)
    o_a, hgrn_bd = _hgrn_call(ca, lw["lb"], lw["hgrn_norm_g"], st0)
    hgrn_new = _hgrn_state_from_blockdiag(hgrn_bd)

    prev_row = jnp.zeros((b, B_COLS), F32) if rwkv_prev is None else rwkv_prev
    s0 = jnp.zeros((b, N_HEADS, HEAD_DIM, HEAD_DIM), F32) if rwkv_s is None else rwkv_s
    o_b, rwkv_new = _rwkv_call(cb, prev_row, s0, lw["rwkv_mu"], lw["rwkv_w0"], lw["rwkv_w2"], lw["rwkv_a0"],
                               lw["rwkv_a2"], lw["rwkv_g2"], lw["rwkv_kk"], lw["rwkv_ka"], lw["rwkv_rk"],
                               lw["rwkv_lnx_g"], lw["rwkv_lnx_b"])
    shift_new = cb[:, l - 1]

    cos, sin = _rope_tables(past, l)
    q_c, lat_new, kr_new = _mla_q_call(qlat, kvlat, kr, cos, sin, lw["mla_qn_g"], lw["mla_w_uq"], lw["mla_kvn_g"])
    if past:
        c_all = _pad_rows(jnp.concatenate([lat_past, lat_new], axis=1), t_pad)
        kr_all = _pad_rows(jnp.concatenate([kr_past, kr_new], axis=1), t_pad)
    else:
        c_all, kr_all = lat_new, kr_new
    k_c, v_c = _mla_kv_call(c_all, kr_all, lw["mla_w_k"], lw["mla_w_v"])
    o_c = _flash_call(q_c, k_c, v_c, past=past, t_valid=t_valid, chunk_mask=True, name="mla_attn")

    lf_new, lf_wide = _fox_lf_call(ff, lw["fox_bf"])
    if past:
        flf_wide = jnp.pad(flf_past, ((0, 0), (0, 0), (0, LANES - N_HEADS)))
        lf_all = _pad_rows(jnp.concatenate([flf_wide, lf_wide], axis=1), t_pad)
        k_all = _pad_rows(jnp.concatenate([fk_past.reshape(b, past, BRANCH_W), fk], axis=1), t_pad)
        v_all = _pad_rows(jnp.concatenate([fv_past.reshape(b, past, BRANCH_W), fv], axis=1), t_pad)
    else:
        lf_all, k_all, v_all = lf_wide, fk, fv
    k_d, v_d, c_all_f = _fox_kv_call(k_all, v_all, lf_all)
    q_d = _fox_q_call(fq, c_all_f[:, past:past + l])
    o_d = _flash_call(q_d, k_d, v_d, past=past, t_valid=t_valid, chunk_mask=False, name="fox_attn")

    flat = lambda o: o.reshape(t, BRANCH_W)
    x1 = _mix_call(x.reshape(t, d), flat(o_a), flat(o_b), flat(o_c), flat(o_d), lw["w_mg"], lw["w_br"], lw["w_o"],
                   lw["ln1_g"], lw["ln1_b"], alpha)
    x2 = _moe_call(x1, lw["router_w"], lw["router_b"], lw["w_g"], lw["w_u"], lw["w_d"], lw["ln2_g"], lw["ln2_b"],
                   alpha)
    x3 = _ple_call(x2, p.reshape(t, PLE_DIM), lw["ple_w"], lw["ple_gate_w"], lw["ln3_g"], lw["ln3_b"], alpha)
    new = (hgrn_new, rwkv_new, shift_new, lat_new, kr_new, fk.reshape(b, l, N_HEADS, HEAD_DIM),
           fv.reshape(b, l, N_HEADS, HEAD_DIM), lf_new)
    return x3.reshape(b, l, d), new


def _relayout_w_uq(w_uq):
    w = w_uq.reshape(Q_LORA, N_HEADS, NOPE + ROPE)
    w = jnp.pad(w, ((0, 256 - Q_LORA), (0, 0), (0, LANES - NOPE - ROPE)))
    return w.reshape(256, N_HEADS * LANES).astype(BF16)


def _relayout_w_ukv(w_ukv):
    w = w_ukv.reshape(KV_LORA, N_HEADS, NOPE + HEAD_DIM)
    pad = lambda x: jnp.pad(x, ((0, 0), (0, 0), (0, LANES - x.shape[-1]))).reshape(KV_LORA, N_HEADS * LANES)
    return pad(w[..., :NOPE]).astype(BF16), pad(w[..., NOPE:]).astype(BF16)


def kernel(x_prompt, x_sample, state_hgrn, state_rwkv, state_rwkv_shift, cache_mla_latent, cache_mla_krope, cache_fox_k, cache_fox_v, cache_fox_logf, p_prompt, p_sample, ln_in_g, ln_in_b, w_in, hgrn_lb_logits, hgrn_norm_g, rwkv_mu, rwkv_w0, rwkv_w2, rwkv_a0, rwkv_a2, rwkv_g2, rwkv_kk, rwkv_ka, rwkv_rk, rwkv_lnx_g, rwkv_lnx_b, mla_qnorm_g, mla_w_uq, mla_kvnorm_g, mla_w_ukv, fox_bf, w_br, w_mg, w_o, ln1_g, ln1_b, router_w, router_b, exp_w_gate, exp_w_up, exp_w_down, ln2_g, ln2_b, ple_w, ple_gate_w, ln3_g, ln3_b):
    depth = w_in.shape[0]
    alpha = (2 * depth) ** DEPTH_ALPHA_POW
    d = x_prompt.shape[-1]
    lb_all = _lb_call(hgrn_lb_logits)

    def ln_in(x):
        return _ln_call(x.reshape(-1, d), ln_in_g, ln_in_b).reshape(x.shape)

    xp, xs = ln_in(x_prompt), ln_in(x_sample)
    new_p, new_s = [], []
    for i in range(depth):
        w_k, w_v = _relayout_w_ukv(mla_w_ukv[i])
        lw = dict(
            w_in=_relayout_w_in(w_in[i]), lb=lb_all[i], hgrn_norm_g=hgrn_norm_g[i], rwkv_mu=rwkv_mu[i],
            rwkv_w0=rwkv_w0[i], rwkv_w2=rwkv_w2[i], rwkv_a0=rwkv_a0[i], rwkv_a2=rwkv_a2[i], rwkv_g2=rwkv_g2[i],
            rwkv_kk=rwkv_kk[i], rwkv_ka=rwkv_ka[i], rwkv_rk=rwkv_rk[i], rwkv_lnx_g=rwkv_lnx_g[i],
            rwkv_lnx_b=rwkv_lnx_b[i],
            mla_qn_g=jnp.pad(mla_qnorm_g[i], (0, 256 - Q_LORA)).reshape(1, 256), mla_w_uq=_relayout_w_uq(mla_w_uq[i]),
            mla_kvn_g=mla_kvnorm_g[i].reshape(1, KV_LORA), mla_w_k=w_k, mla_w_v=w_v, fox_bf=fox_bf[i],
            w_br=w_br[i].astype(BF16), w_mg=w_mg[i].astype(BF16), w_o=w_o[i].astype(BF16),
            ln1_g=ln1_g[i], ln1_b=ln1_b[i], router_w=router_w, router_b=router_b,
            w_g=exp_w_gate[i].astype(BF16), w_u=exp_w_up[i].astype(BF16), w_d=exp_w_down[i].astype(BF16),
            ln2_g=ln2_g[i], ln2_b=ln2_b[i], ple_w=ple_w[i].astype(BF16), ple_gate_w=ple_gate_w[i].astype(BF16),
            ln3_g=ln3_g[i], ln3_b=ln3_b[i])
        xp, st_p = _layer(xp, p_prompt[i], (None,) * 8, lw, alpha)
        new_p.append(st_p)
        st_in = (state_hgrn[i], state_rwkv[i], state_rwkv_shift[i], cache_mla_latent[i], cache_mla_krope[i],
                 cache_fox_k[i], cache_fox_v[i], cache_fox_logf[i])
        xs, st_s = _layer(xs, p_sample[i], st_in, lw, alpha)
        new_s.append(st_s)
    stack = lambda sts, j: jnp.stack([s[j] for s in sts], axis=0)
    outs_p = tuple(stack(new_p, j) for j in range(8))
    outs_s = tuple(stack(new_s, j) for j in range(8))
    return (xp, xs) + outs_p + outs_s
```

```python
import functools
import math

import numpy as np
import jax
import jax.numpy as jnp
from jax import lax
from jax.experimental import pallas as pl
from jax.experimental.pallas import tpu as pltpu

F32 = jnp.float32
BF16 = jnp.bfloat16

D_MODEL = 1024
N_HEADS = 4
HEAD_DIM = 64
BRANCH_W = N_HEADS * HEAD_DIM
CHUNK = 64
W_LORA, A_LORA, G_LORA = 32, 32, 64
NOPE, ROPE, Q_LORA, KV_LORA = 64, 32, 192, 128
ROPE_THETA = 10000.0
N_EXPERTS, N_GROUPS, EXPERT_FF = 16, 4, 256
GROUP_SIZE = N_EXPERTS // N_GROUPS
PLE_DIM = 256
A_COLS = 4 * BRANCH_W
B_COLS = 3 * BRANCH_W + W_LORA + A_LORA + G_LORA
C_COLS = Q_LORA + KV_LORA + ROPE
D_COLS = 3 * BRANCH_W + N_HEADS
DEPTH_ALPHA_POW = 0.25
LN_EPS = 1e-5
RMS_EPS = 1e-6
RWKV_GN_EPS = 64e-5
MASK_VALUE = -1e30
LOG2E = math.log2(math.e)

LANES = 128
SUBLANES = 8
VMEM_LIMIT_BYTES = 56 * 1024 * 1024

NN = ((1,), (0,))
NT = ((1,), (1,))
TN = ((0,), (0,))


def _dg(a, b, dims=NN):
    return lax.dot_general(a, b, (dims, ((), ())), preferred_element_type=F32)


def _bdot(a, b, dims=NN):
    return _dg(a.astype(BF16), b.astype(BF16), dims)


def _split(x, pieces):
    out = []
    r = x
    for i in range(pieces):
        p = r.astype(BF16)
        out.append(p)
        if i + 1 < pieces:
            r = r - p.astype(F32)
    return out


def _dot3(a, b, dims=NN):
    ah, al = _split(a, 2)
    bh, bl = _split(b, 2)
    return _dg(ah, bh, dims) + (_dg(ah, bl, dims) + _dg(al, bh, dims))


def _dotc(a, c, dims=NN, pieces=3):
    ps = _split(a, pieces)
    acc = _dg(ps[0], c, dims)
    for p in ps[1:]:
        acc = acc + _dg(p, c, dims)
    return acc


def _tree_sum(xs):
    while len(xs) > 1:
        xs = [xs[i] + xs[i + 1] for i in range(0, len(xs) - 1, 2)] + ([xs[-1]] if len(xs) % 2 else [])
    return xs[0]


def _iota(shape, dim):
    return lax.broadcasted_iota(jnp.int32, shape, dim)


def _tri_incl(n, dtype=BF16):
    return (_iota((n, n), 0) >= _iota((n, n), 1)).astype(dtype)


def _head_indicator():
    r = _iota((BRANCH_W, BRANCH_W), 0) // HEAD_DIM
    c = _iota((BRANCH_W, BRANCH_W), 1) // HEAD_DIM
    return (r == c).astype(BF16)


def _layer_norm(x, g, b):
    mu = jnp.mean(x, axis=-1, keepdims=True)
    xc = x - mu
    var = jnp.mean(xc * xc, axis=-1, keepdims=True)
    return xc * lax.rsqrt(var + LN_EPS) * g + b


def _softplus(x):
    return jnp.maximum(x, 0.0) + jnp.log(1.0 + jnp.exp(-jnp.abs(x)))


def _silu(x):
    return x * jax.nn.sigmoid(x)


def _cparams(*sem):
    return pltpu.CompilerParams(dimension_semantics=sem, vmem_limit_bytes=VMEM_LIMIT_BYTES)


def _resident(shape):
    nd = len(shape)
    return pl.BlockSpec(shape, lambda *_: (0,) * nd)


TOKEN_TILE = 512
MOE_TOKEN_TILE = 1024
MOE_BLOCK = 256
HGRN_SUB_BLOCK = 16
ATTN_Q_TILE = 512
ATTN_KV_TILE = 512
ATTN_V_ROWS = 80


def _token_tile(t, largest=TOKEN_TILE):
    tm = largest
    while tm >= SUBLANES:
        if t % tm == 0:
            return tm
        tm //= 2
    raise ValueError(f"token count {t} not a multiple of 8")


def _ln_kernel(x_ref, g_ref, b_ref, o_ref):
    o_ref[...] = _layer_norm(x_ref[...], g_ref[...], b_ref[...])


def _ln_call(x, g, b):
    t, d = x.shape
    tm = _token_tile(t)
    return pl.pallas_call(
        _ln_kernel,
        grid=(t // tm,),
        in_specs=[pl.BlockSpec((tm, d), lambda i: (i, 0)), _resident((1, d)), _resident((1, d))],
        out_specs=pl.BlockSpec((tm, d), lambda i: (i, 0)),
        out_shape=jax.ShapeDtypeStruct((t, d), F32),
        compiler_params=_cparams("parallel"),
        name="ln_in",
    )(x, g.reshape(1, d), b.reshape(1, d))


_IN_SLOTS = (("a", A_COLS, A_COLS), ("b", B_COLS, B_COLS), ("qlat", Q_LORA, 256), ("kvlat", KV_LORA, 128),
             ("kr", ROPE, 128), ("fq", BRANCH_W, BRANCH_W), ("fk", BRANCH_W, BRANCH_W), ("fv", BRANCH_W, BRANCH_W),
             ("ff", N_HEADS, 128))


def _relayout_w_in(w_in):
    parts, off = [], 0
    for _, width, slot in _IN_SLOTS:
        w = w_in[:, off:off + width]
        if slot > width:
            w = jnp.pad(w, ((0, 0), (0, slot - width)))
        parts.append(w)
        off += width
    assert off == w_in.shape[1]
    return jnp.concatenate(parts, axis=1).astype(BF16)


def _in_kernel(x_ref, w_ref, *o_refs):
    xb = x_ref[...].astype(BF16)
    off = 0
    for o_ref, (_, _, slot) in zip(o_refs, _IN_SLOTS):
        o_ref[...] = _dg(xb, w_ref[:, off:off + slot])
        off += slot


def _in_call(x, w_in_p):
    t, d = x.shape
    tm = _token_tile(t)
    n = w_in_p.shape[1]
    return pl.pallas_call(
        _in_kernel,
        grid=(t // tm,),
        in_specs=[pl.BlockSpec((tm, d), lambda i: (i, 0)), _resident((d, n))],
        out_specs=[pl.BlockSpec((tm, slot), lambda i: (i, 0)) for _, _, slot in _IN_SLOTS],
        out_shape=[jax.ShapeDtypeStruct((t, slot), F32) for _, _, slot in _IN_SLOTS],
        compiler_params=_cparams("parallel"),
        name="in_proj",
    )(x, w_in_p)


def _hgrn_parts(c_ref, lb_ref, ng_ref, s0_ref, o_ref, s_ref, st_scr, *, chunk, group):
    def init():
        st_scr[...] = s0_ref[...]

    def finish():
        s_ref[...] = st_scr[...]

    return init, _hgrn_stages(c_ref, lb_ref, ng_ref, o_ref, st_scr, chunk, group), finish


def _hgrn_stages(c_ref, lb_ref, ng_ref, o_ref, st_scr, chunk, group):
    c = chunk
    w = BRANCH_W
    hd = HEAD_DIM
    nh = N_HEADS
    sb = min(HGRN_SUB_BLOCK, c)
    seqs = range(group)
    tri = _tri_incl(c)
    ind = _head_indicator()
    st_keep = (_iota((w, w), 0) // hd) == (_iota((w, w), 1) // hd)
    hr_keep = (_iota((nh * sb, w), 0) // sb) == (_iota((nh * sb, w), 1) // hd)
    row8 = _iota((SUBLANES, w), 0)
    lb = lb_ref[...]

    sq = []
    for bi in seqs:
        cols = c_ref[bi]
        q, fz, iv, g = cols[:, :w], cols[:, w:2 * w], cols[:, 2 * w:3 * w], cols[:, 3 * w:]
        f = lb + (1.0 - lb) * jax.nn.sigmoid(fz)
        a = _dotc(tri, jnp.log(f) * LOG2E, pieces=3)
        sq.append(dict(q=q, k=1.0 - f, iv=iv, g=g, a=a, a_last=a[c - 1:c, :], outs=[]))
    yield

    for bi, s in zip(seqs, sq):
        st = st_scr[bi]
        s["o"] = _bdot(s["q"] * jnp.exp2(s["a"]), st, NT)
        kd = s["k"] * jnp.exp2(s["a_last"] - s["a"])
        st_scr[bi] = st * jnp.exp2(s["a_last"]) + jnp.where(st_keep, _bdot(s["iv"], kd, TN), 0.0)
    yield

    for tb in range(c // sb):
        lo = tb * sb
        for s in sq:
            q, k, iv, a = s["q"], s["k"], s["iv"], s["a"]
            parts = []
            for rg in range(sb // SUBLANES):
                r0 = lo + rg * SUBLANES
                q8, a8 = q[r0:r0 + SUBLANES], a[r0:r0 + SUBLANES]
                pieces = []
                for src in range(lo, r0 + SUBLANES):
                    d = a8 - a[src:src + 1]
                    if src >= r0:
                        d = jnp.where(row8 >= src - r0, d, MASK_VALUE)
                    pieces.append(q8 * jnp.exp2(d) * k[src:src + 1])
                e = _dotc(jnp.concatenate(pieces, axis=0), ind, pieces=1)
                parts.append(_tree_sum([e[i * SUBLANES:(i + 1) * SUBLANES] * iv[lo + i:lo + i + 1]
                                        for i in range(r0 + SUBLANES - lo)]))
            o_tb = jnp.concatenate(parts, axis=0)
            if tb > 0:
                a_lo = a[lo:lo + 1]
                qh = q[lo:lo + sb] * jnp.exp2(a[lo:lo + sb] - a_lo)
                kh = k[:lo] * jnp.exp2(a_lo - a[:lo])
                qh_rows = jnp.where(hr_keep, jnp.concatenate([qh] * nh, axis=0), 0.0)
                sc = _bdot(qh_rows, kh, NT)
                ov = jnp.where(hr_keep, _bdot(sc, iv[:lo]), 0.0)
                for h in range(nh):
                    o_tb = o_tb + ov[h * sb:(h + 1) * sb]
            s["outs"].append(o_tb)
        yield

    for bi, s in zip(seqs, sq):
        o = s["o"] + jnp.concatenate(s["outs"], axis=0)
        ms = _dotc(o * o, ind, pieces=2) * (1.0 / hd)
        o = o * lax.rsqrt(ms + RMS_EPS) * ng_ref[...] * _silu(s["g"])
        o_ref[bi] = o.astype(o_ref.dtype)


def _hgrn_state_to_blockdiag(s):
    b = s.shape[0]
    out = jnp.zeros((b, N_HEADS, HEAD_DIM, N_HEADS, HEAD_DIM), F32)
    for h in range(N_HEADS):
        out = out.at[:, h, :, h, :].set(jnp.swapaxes(s[:, h], -1, -2))
    return out.reshape(b, BRANCH_W, BRANCH_W)


def _hgrn_state_from_blockdiag(st):
    hs = [st[:, h * HEAD_DIM:(h + 1) * HEAD_DIM, h * HEAD_DIM:(h + 1) * HEAD_DIM] for h in range(N_HEADS)]
    return jnp.swapaxes(jnp.stack(hs, axis=1), -1, -2)


def _rwkv_tdot(a, b, dims=NN):
    return _bdot(a, b, dims)


def _rwkv_sdot(a, b, dims=NN):
    return _dot3(a, b, dims)


def _rwkv_operands(cols, prev, mu_ref, w0_ref, lora_ref, a0_ref, kk_ref, ka_ref, rk_ref):
    shifted = jnp.where(_iota(cols.shape, 0) == 0, prev, pltpu.roll(cols, 1, 0))
    m = cols + (shifted - cols) * mu_ref[...]
    w = BRANCH_W
    r, k, v = m[:, :w], m[:, w:2 * w], m[:, 2 * w:3 * w]
    slab = m[:, 3 * w:]
    lane = _iota(slab.shape, 1)
    act = jnp.where(lane < W_LORA, jnp.tanh(slab),
                    jnp.where(lane < W_LORA + A_LORA, slab, jax.nn.sigmoid(slab)))
    lora = _bdot(act, lora_ref[...])
    w_log = -_softplus(-(w0_ref[...] + lora[:, :w])) - 0.5
    a_rate = jax.nn.sigmoid(a0_ref[...] + lora[:, w:2 * w])
    g = lora[:, 2 * w:]
    ind = _head_indicator()
    kk = k * kk_ref[...]
    norm = jnp.sqrt(_dotc(kk * kk, ind, pieces=3))
    kk = kk / jnp.maximum(norm, 1e-12)
    kh = k * (1.0 + (a_rate - 1.0) * ka_ref[...])
    return dict(r=r, lw=-jnp.exp(w_log), k=kh, v=v, a=-kk, b=kk * a_rate, g=g,
                bonus=_dotc(r * kh * rk_ref[...], ind, pieces=3) * v)


def _rwkv_parts(c_ref, prow_ref, mu_ref, w0_ref, lora_ref, a0_ref, kk_ref, ka_ref, rk_ref, lg_ref, lb_ref,
                s0_ref, o_ref, s_ref, st_scr, prev_scr, *, chunk, group):
    hd = HEAD_DIM
    blocks = [(bi, slice(h * hd, (h + 1) * hd)) for bi in range(group) for h in range(N_HEADS)]

    def init():
        st_scr[...] = jnp.zeros(st_scr.shape, F32)
        prev_scr[...] = prow_ref[...]
        for bi, sl in blocks:
            st_scr[bi, sl, sl] = s0_ref[bi, sl.start // hd]

    def finish():
        for bi, sl in blocks:
            s_ref[bi, sl.start // hd] = st_scr[bi, sl, sl]

    stages = _rwkv_stages(c_ref, mu_ref, w0_ref, lora_ref, a0_ref, kk_ref, ka_ref, rk_ref, lg_ref, lb_ref, o_ref,
                          st_scr, prev_scr, chunk, group)
    return init, stages, finish


def _rwkv_stages(c_ref, mu_ref, w0_ref, lora_ref, a0_ref, kk_ref, ka_ref, rk_ref, lg_ref, lb_ref, o_ref, st_scr,
                 prev_scr, chunk, group):
    c = chunk
    w = BRANCH_W
    hd = HEAD_DIM
    nh = N_HEADS
    n = nh * c
    seqs = range(group)
    tri = _tri_incl(c)
    ind = _head_indicator()
    t_w = _iota((c, n), 0)
    s_w = _iota((c, n), 1) % c
    strict_w = t_w > s_w
    incl_w = t_w >= s_w
    eye_w = (t_w == s_w).astype(F32)
    bd_keep = (_iota((n, n), 0) // c) == (_iota((n, n), 1) // c)
    hr_keep = (_iota((n, w), 0) // c) == (_iota((n, w), 1) // hd)
    st_keep = (_iota((w, w), 0) // hd) == (_iota((w, w), 1) // hd)

    def bd(x):
        return jnp.where(bd_keep, jnp.concatenate([x] * nh, axis=0), 0.0)

    def head_rows(x):
        return jnp.where(hr_keep, jnp.concatenate([x] * nh, axis=0), 0.0)

    sq = []
    for bi in seqs:
        cols = c_ref[bi]
        ops = _rwkv_operands(cols, prev_scr[bi], mu_ref, w0_ref, lora_ref, a0_ref, kk_ref, ka_ref, rk_ref)
        prev_scr[bi] = cols[c - 1:c]
        r, lw, k, v, a, b = (ops[name] for name in ("r", "lw", "k", "v", "a", "b"))
        cum = _dotc(tri, lw, pieces=3)
        last = cum[c - 1:c, :]
        e_neg = jnp.exp(-cum)
        e_last = jnp.exp(last - cum)
        sq.append(dict(v=v, last=last, g=ops["g"], bonus=ops["bonus"],
                       lhs=jnp.concatenate([a * jnp.exp(cum - lw), r * jnp.exp(cum)], axis=0),
                       bt=b * e_neg, kt=k * e_neg, hat=jnp.concatenate([b * e_last, k * e_last], axis=0)))
    yield

    for q in sq:
        pb = _rwkv_tdot(q["lhs"], head_rows(q["bt"]), NT)
        pk = _rwkv_tdot(q["lhs"], head_rows(q["kt"]), NT)
        q["l_ab"] = jnp.where(strict_w, pb[:c], 0.0)
        q["a_rb"] = jnp.where(incl_w, pb[c:], 0.0)
        q["lk"] = jnp.concatenate([jnp.where(strict_w, pk[:c], 0.0), jnp.where(incl_w, pk[c:], 0.0)], axis=0)
        q["tinv"] = eye_w + q["l_ab"]
        q["p"] = q["l_ab"]
    yield

    for _ in range(int(math.log2(c)) - 1):
        for q in sq:
            q["p"] = _rwkv_tdot(q["p"], bd(q["p"]))
        yield
        for q in sq:
            q["tinv"] = q["tinv"] + _rwkv_tdot(q["tinv"], bd(q["p"]))
        yield

    for bi, q in zip(seqs, sq):
        q["st"] = st_scr[bi]
        q["sh"] = _rwkv_sdot(q["lhs"], q["st"], NT)
        q["lv"] = _rwkv_tdot(q["lk"], head_rows(q["v"]))
    yield
    for q in sq:
        q["u"] = _rwkv_tdot(q["tinv"], head_rows(q["sh"][:c] + q["lv"][:c]))
    yield
    for bi, q in zip(seqs, sq):
        upd = _rwkv_sdot(jnp.concatenate([q["u"], q["v"]], axis=0), q["hat"], TN)
        st_scr[bi] = q["st"] * jnp.exp(q["last"]) + jnp.where(st_keep, upd, 0.0)
    yield
    for q in sq:
        q["y"] = q["sh"][c:] + _rwkv_tdot(q["a_rb"], head_rows(q["u"])) + q["lv"][c:]
    yield

    for bi, q in zip(seqs, sq):
        y = q["y"]
        mu = _dotc(y, ind, pieces=3) * (1.0 / hd)
        yc = y - mu
        var = _dotc(yc * yc, ind, pieces=2) * (1.0 / hd)
        yn = yc * lax.rsqrt(var + RWKV_GN_EPS) * lg_ref[...] + lb_ref[...]
        o_ref[bi] = ((yn + q["bonus"]) * q["g"]).astype(o_ref.dtype)


def _batch_group(b):
    return 4 if b % 4 == 0 else 1


N_HGRN_IN, N_RWKV_IN = 4, 12
RWKV_STAGES_PER_HGRN_STAGE = 1


def _recurrent_kernel(*refs, chunk, group):
    h_in, refs = refs[:N_HGRN_IN], refs[N_HGRN_IN:]
    r_in, refs = refs[:N_RWKV_IN], refs[N_RWKV_IN:]
    h_o, h_s, r_o, r_s, h_st, r_st, r_prev = refs
    h_init, h_stages, h_finish = _hgrn_parts(*h_in, h_o, h_s, h_st, chunk=chunk, group=group)
    r_init, r_stages, r_finish = _rwkv_parts(*r_in, r_o, r_s, r_st, r_prev, chunk=chunk, group=group)
    j = pl.program_id(1)

    @pl.when(j == 0)
    def _():
        h_init()
        r_init()

    live = [r_stages] * RWKV_STAGES_PER_HGRN_STAGE + [h_stages]
    while live:
        for stages in list(live):
            if stages in live and next(stages, "done") == "done":
                live = [s for s in live if s is not stages]

    @pl.when(j == pl.num_programs(1) - 1)
    def _():
        h_finish()
        r_finish()


def _recurrent_call(cols_a, lb, norm_g, hgrn_st0, cols_b, prev_row, rwkv_s0, mu, w0, w2, a0, a2, g2, kk_s, ka, rk,
                    lnx_g, lnx_b):
    b, l, nb = cols_b.shape
    w = BRANCH_W
    c = CHUNK if l % CHUNK == 0 else l
    bg = _batch_group(b)
    row = lambda x: x.reshape(1, -1)
    lora_w = jnp.zeros((LANES, 3 * w), F32)
    lora_w = lora_w.at[:W_LORA, :w].set(w2).at[W_LORA:W_LORA + A_LORA, w:2 * w].set(a2)
    lora_w = lora_w.at[W_LORA + A_LORA:, 2 * w:].set(g2).astype(BF16)
    tok = lambda n: pl.BlockSpec((bg, c, n), lambda i, j: (i, j, 0))
    hst_spec = pl.BlockSpec((bg, w, w), lambda i, j: (i, 0, 0))
    rst_spec = pl.BlockSpec((bg, N_HEADS, HEAD_DIM, HEAD_DIM), lambda i, j: (i, 0, 0, 0))
    hgrn_in = [tok(A_COLS), _resident((1, w)), _resident((1, w)), hst_spec]
    rwkv_in = ([tok(nb), pl.BlockSpec((bg, 1, nb), lambda i, j: (i, 0, 0)), _resident((1, nb)), _resident((1, w)),
                _resident((LANES, 3 * w))] + [_resident((1, w))] * 6 + [rst_spec])
    assert len(hgrn_in) == N_HGRN_IN and len(rwkv_in) == N_RWKV_IN
    return pl.pallas_call(
        functools.partial(_recurrent_kernel, chunk=c, group=bg),
        grid=(b // bg, l // c),
        in_specs=hgrn_in + rwkv_in,
        out_specs=[tok(w), hst_spec, tok(w), rst_spec],
        out_shape=[jax.ShapeDtypeStruct((b, l, w), BF16), jax.ShapeDtypeStruct((b, w, w), F32),
                   jax.ShapeDtypeStruct((b, l, w), BF16),
                   jax.ShapeDtypeStruct((b, N_HEADS, HEAD_DIM, HEAD_DIM), F32)],
        scratch_shapes=[pltpu.VMEM((bg, w, w), F32), pltpu.VMEM((bg, w, w), F32), pltpu.VMEM((bg, 1, nb), F32)],
        compiler_params=_cparams("parallel", "arbitrary"),
        name="hgrn_rwkv",
    )(cols_a, row(lb), row(norm_g), hgrn_st0,
      cols_b, prev_row.reshape(b, 1, nb), row(mu), row(w0), lora_w, row(a0), row(kk_s), row(ka), row(rk),
      row(lnx_g), row(lnx_b), rwkv_s0)


def _rope_tables(p, l):
    half = ROPE // 2
    inv = 1.0 / (ROPE_THETA ** (jnp.arange(half, dtype=F32) / half))
    ang = (p + jnp.arange(l, dtype=jnp.int32)).astype(F32)[:, None] * inv[None, :]
    cos, sin = jnp.cos(ang), jnp.sin(ang)
    pad = jnp.zeros((l, LANES - ROPE), F32)
    return jnp.concatenate([cos, cos, pad], axis=1), jnp.concatenate([-sin, sin, pad], axis=1)


def _swap_halves(x, base):
    half = ROPE // 2
    n = x.shape[-1]
    lane = _iota(x.shape, x.ndim - 1) % LANES
    up = pltpu.roll(x, n - half, x.ndim - 1)
    down = pltpu.roll(x, half, x.ndim - 1)
    return jnp.where((lane >= base) & (lane < base + half), up,
                     jnp.where((lane >= base + half) & (lane < base + ROPE), down, 0.0))


def _rms_norm(x, g, width):
    ms = jnp.sum(x * x, axis=-1, keepdims=True) * (1.0 / width)
    return x * lax.rsqrt(ms + RMS_EPS) * g


def _mla_q_kernel(ql_ref, kvl_ref, kr_ref, cos_ref, sin_ref, qg_ref, wuq_ref, kvg_ref, q_ref, c_ref, krn_ref):
    scale = (NOPE + ROPE) ** -0.5 * LOG2E
    qn = _rms_norm(ql_ref[0], qg_ref[...], Q_LORA)
    q = _bdot(qn, wuq_ref[...])
    cos, sin = cos_ref[...], sin_ref[...]
    lane = _iota(cos.shape, 1)
    cos_q = jnp.where(lane < NOPE, 1.0, pltpu.roll(cos, NOPE, 1))
    sin_q = pltpu.roll(sin, NOPE, 1)
    for h in range(N_HEADS):
        qh = q[:, h * LANES:(h + 1) * LANES]
        qh = qh * cos_q + _swap_halves(qh, NOPE) * sin_q
        q_ref[0, h] = (qh * scale).astype(q_ref.dtype)
    c_ref[0] = _rms_norm(kvl_ref[0], kvg_ref[...], KV_LORA)
    kr = kr_ref[0]
    krn = kr * cos + _swap_halves(kr, 0) * sin
    krn_ref[0] = krn[:, :ROPE]


def _mla_q_call(qlat, kvlat, kr, cos, sin, qn_g, w_uq_p, kvn_g):
    b, l, _ = qlat.shape
    tm = _token_tile(l)
    tok = lambda n: pl.BlockSpec((1, tm, n), lambda i, j: (i, j, 0))
    tab = pl.BlockSpec((tm, LANES), lambda i, j: (j, 0))
    return pl.pallas_call(
        _mla_q_kernel,
        grid=(b, l // tm),
        in_specs=[tok(256), tok(KV_LORA), tok(LANES), tab, tab, _resident((1, 256)),
                  _resident((256, N_HEADS * LANES)), _resident((1, KV_LORA))],
        out_specs=[pl.BlockSpec((1, N_HEADS, tm, LANES), lambda i, j: (i, 0, j, 0)), tok(KV_LORA), tok(ROPE)],
        out_shape=[jax.ShapeDtypeStruct((b, N_HEADS, l, LANES), BF16), jax.ShapeDtypeStruct((b, l, KV_LORA), F32),
                   jax.ShapeDtypeStruct((b, l, ROPE), F32)],
        compiler_params=_cparams("parallel", "parallel"),
        name="mla_q",
    )(qlat, kvlat, kr, cos, sin, qn_g, w_uq_p, kvn_g)


def _mla_kv_kernel(c_ref, kr_ref, wk_ref, wv_ref, k_ref, v_ref):
    cb = c_ref[0].astype(BF16)
    kall = _dg(cb, wk_ref[...])
    vall = _dg(cb, wv_ref[...])
    tm = cb.shape[0]
    place = (_iota((ROPE, LANES), 0) + NOPE == _iota((ROPE, LANES), 1)).astype(BF16)
    kr_slot = _dg(kr_ref[0].astype(BF16), place)
    one_slot = (_iota((tm, LANES), 1) == HEAD_DIM).astype(F32)
    for h in range(N_HEADS):
        k_ref[0, h] = (kall[:, h * LANES:(h + 1) * LANES] + kr_slot).astype(k_ref.dtype)
        v_ref[0, h, 0] = (vall[:, h * LANES:(h + 1) * LANES] + one_slot).T[:ATTN_V_ROWS].astype(v_ref.dtype)


def _kv_specs(b, t, tm):
    kspec = pl.BlockSpec((1, N_HEADS, tm, LANES), lambda i, j: (i, 0, j, 0))
    vspec = pl.BlockSpec((1, N_HEADS, 1, ATTN_V_ROWS, tm), lambda i, j: (i, 0, j, 0, 0))
    shapes = [jax.ShapeDtypeStruct((b, N_HEADS, t, LANES), BF16),
              jax.ShapeDtypeStruct((b, N_HEADS, t // tm, ATTN_V_ROWS, tm), BF16)]
    return [kspec, vspec], shapes


def _mla_kv_call(c_all, kr_all, w_k_p, w_v_p):
    b, t, _ = c_all.shape
    tm = _token_tile(t, ATTN_KV_TILE)
    tok = lambda n: pl.BlockSpec((1, tm, n), lambda i, j: (i, j, 0))
    out_specs, out_shape = _kv_specs(b, t, tm)
    return pl.pallas_call(
        _mla_kv_kernel,
        grid=(b, t // tm),
        in_specs=[tok(KV_LORA), tok(ROPE), _resident((KV_LORA, N_HEADS * LANES)),
                  _resident((KV_LORA, N_HEADS * LANES))],
        out_specs=out_specs,
        out_shape=out_shape,
        compiler_params=_cparams("parallel", "parallel"),
        name="mla_kv",
    )(c_all, kr_all, w_k_p, w_v_p)


def _fox_lf_kernel(ff_ref, bf_ref, lf_ref, lfw_ref):
    z = ff_ref[0] + bf_ref[...]
    lf = jnp.where(_iota(z.shape, 1) < N_HEADS, -_softplus(-z), 0.0)
    lfw_ref[0] = lf
    lf_ref[0] = lf[:, :N_HEADS]


def _fox_lf_call(ff, bf):
    b, l, _ = ff.shape
    tm = _token_tile(l)
    return pl.pallas_call(
        _fox_lf_kernel,
        grid=(b, l // tm),
        in_specs=[pl.BlockSpec((1, tm, LANES), lambda i, j: (i, j, 0)), _resident((1, LANES))],
        out_specs=[pl.BlockSpec((1, tm, N_HEADS), lambda i, j: (i, j, 0)),
                   pl.BlockSpec((1, tm, LANES), lambda i, j: (i, j, 0))],
        out_shape=[jax.ShapeDtypeStruct((b, l, N_HEADS), F32), jax.ShapeDtypeStruct((b, l, LANES), F32)],
        compiler_params=_cparams("parallel", "parallel"),
        name="fox_lf",
    )(ff, jnp.pad(bf, (0, LANES - N_HEADS)).reshape(1, LANES))


def _head_slot(x, h):
    pair = x[:, (h // 2) * LANES:(h // 2 + 1) * LANES]
    return pair if h % 2 == 0 else pltpu.roll(pair, HEAD_DIM, 1)


def _bias_lanes(c_col, first, ones_first):
    tm = c_col.shape[0]
    lane = _iota((tm, LANES), 1)
    p0 = c_col.astype(BF16).astype(F32)
    r1 = c_col - p0
    p1 = r1.astype(BF16).astype(F32)
    p2 = r1 - p1
    out = jnp.where(lane == first, p0, jnp.where(lane == first + 1, p1, jnp.where(lane == first + 2, p2, 0.0)))
    return jnp.where((lane >= ones_first) & (lane < ones_first + 3), 1.0, out)


def _fox_q_kernel(q_ref, c_ref, o_ref):
    scale = HEAD_DIM ** -0.5 * LOG2E
    q = q_ref[0]
    c = c_ref[0] * LOG2E
    lane = _iota((q.shape[0], LANES), 1)
    for h in range(N_HEADS):
        bias = _bias_lanes(c[:, h:h + 1], HEAD_DIM, HEAD_DIM + 3)
        o_ref[0, h] = jnp.where(lane < HEAD_DIM, _head_slot(q, h) * scale, bias).astype(o_ref.dtype)


def _fox_kv_kernel(k_ref, v_ref, lf_ref, ko_ref, vo_ref, c_ref, carry):
    @pl.when(pl.program_id(1) == 0)
    def _():
        carry[...] = jnp.zeros_like(carry)

    k, v, lf = k_ref[0], v_ref[0], lf_ref[0]
    tm = k.shape[0]
    c = _dotc(_tri_incl(tm), lf, pieces=3) + carry[...]
    c_ref[0] = c
    carry[...] = c[tm - 1:, :]
    c = c * LOG2E
    lane = _iota((tm, LANES), 1)
    for h in range(N_HEADS):
        bias = _bias_lanes(-c[:, h:h + 1], HEAD_DIM + 3, HEAD_DIM)
        ko_ref[0, h] = jnp.where(lane < HEAD_DIM, _head_slot(k, h), bias).astype(ko_ref.dtype)
        vo_ref[0, h, 0] = jnp.where(lane < HEAD_DIM, _head_slot(v, h),
                                    (lane == HEAD_DIM).astype(F32)).T[:ATTN_V_ROWS].astype(vo_ref.dtype)


def _fox_q_call(q, c_new):
    b, l, w = q.shape
    tm = _token_tile(l)
    return pl.pallas_call(
        _fox_q_kernel,
        grid=(b, l // tm),
        in_specs=[pl.BlockSpec((1, tm, w), lambda i, j: (i, j, 0)),
                  pl.BlockSpec((1, tm, LANES), lambda i, j: (i, j, 0))],
        out_specs=pl.BlockSpec((1, N_HEADS, tm, LANES), lambda i, j: (i, 0, j, 0)),
        out_shape=jax.ShapeDtypeStruct((b, N_HEADS, l, LANES), BF16),
        compiler_params=_cparams("parallel", "parallel"),
        name="fox_q",
    )(q, c_new)


def _fox_kv_call(k_all, v_all, lf_all):
    b, t, w = k_all.shape
    tm = _token_tile(t, ATTN_KV_TILE)
    tok = pl.BlockSpec((1, tm, w), lambda i, j: (i, j, 0))
    wide = pl.BlockSpec((1, tm, LANES), lambda i, j: (i, j, 0))
    out_specs, out_shape = _kv_specs(b, t, tm)
    return pl.pallas_call(
        _fox_kv_kernel,
        grid=(b, t // tm),
        in_specs=[tok, tok, wide],
        out_specs=out_specs + [wide],
        out_shape=out_shape + [jax.ShapeDtypeStruct((b, t, LANES), F32)],
        scratch_shapes=[pltpu.VMEM((1, LANES), F32)],
        compiler_params=_cparams("parallel", "arbitrary"),
        name="fox_kv",
    )(k_all, v_all, lf_all)


def _flash_kernel(q_ref, k_ref, v_ref, o_ref, m_scr, acc_scr, s_scr, *, tq, tk, past, t_valid, chunk_mask):
    qi = pl.program_id(1)
    q_start = past + qi * tq
    if chunk_mask:
        vis_end = jnp.minimum(((q_start + tq + CHUNK - 1) // CHUNK) * CHUNK, t_valid)
    else:
        vis_end = q_start + tq
    n_blocks = (vis_end + tk - 1) // tk
    n_full = q_start // tk

    m_scr[...] = jnp.full(m_scr.shape, MASK_VALUE, F32)
    acc_scr[...] = jnp.zeros(acc_scr.shape, F32)

    def logits(h, kb):
        ks = pl.multiple_of(kb * tk, tk)
        return _dg(k_ref[0, h, pl.ds(ks, tk), :], q_ref[0, h], NT)

    s_scr[...] = logits(0, 0)

    def block(kb, masked, s_next):
        if masked:
            kpos = kb * tk + _iota((tk, tq), 0)
            qpos = q_start + _iota((tk, tq), 1)
            if chunk_mask:
                keep = ((kpos // CHUNK) <= (qpos // CHUNK)) & (kpos < t_valid)
            else:
                keep = kpos <= qpos
        for h in range(N_HEADS):
            s = jnp.where(keep, s_next, MASK_VALUE) if masked else s_next
            if h + 1 < N_HEADS:
                s_next = logits(h + 1, kb)
            else:
                s_next = logits(0, jnp.minimum(kb + 1, n_blocks - 1))
            m_old = m_scr[h]
            m_new = jnp.maximum(m_old, jnp.max(s, axis=0, keepdims=True))
            p = jnp.exp2(s - m_new[:1])
            pv = _dg(v_ref[0, h, kb], p.astype(BF16))
            acc_scr[h] = acc_scr[h] * jnp.exp2(m_old - m_new)[:1] + pv
            m_scr[h] = m_new
        return s_next

    def full_body(kb, carry):
        s_scr[...] = block(kb, False, s_scr[...])
        return carry

    def masked_body(kb, carry):
        s_scr[...] = block(kb, True, s_scr[...])
        return carry

    lax.fori_loop(0, n_full, full_body, 0)
    lax.fori_loop(n_full, n_blocks, masked_body, 0)
    for h in range(N_HEADS):
        acc = acc_scr[h]
        acc = jnp.concatenate([acc, jnp.zeros((LANES - ATTN_V_ROWS, tq), F32)], axis=0).T
        o = acc[:, :HEAD_DIM] / acc[:, HEAD_DIM:HEAD_DIM + 1]
        o_ref[0, :, h * HEAD_DIM:(h + 1) * HEAD_DIM] = o.astype(o_ref.dtype)


def _flash_call(q, k, v, *, past, t_valid, chunk_mask, name):
    b, nh, l, _ = q.shape
    t = k.shape[2]
    tk = v.shape[-1]
    lq = max(l, LANES)
    if lq != l:
        q = jnp.pad(q, ((0, 0), (0, 0), (0, lq - l), (0, 0)))
    tq = _token_tile(lq, ATTN_Q_TILE)
    out = pl.pallas_call(
        functools.partial(_flash_kernel, tq=tq, tk=tk, past=past, t_valid=t_valid, chunk_mask=chunk_mask),
        grid=(b, lq // tq),
        in_specs=[pl.BlockSpec((1, nh, tq, LANES), lambda i, j: (i, 0, j, 0)),
                  pl.BlockSpec((1, nh, t, LANES), lambda i, j: (i, 0, 0, 0)),
                  pl.BlockSpec((1, nh, t // tk, ATTN_V_ROWS, tk), lambda i, j: (i, 0, 0, 0, 0))],
        out_specs=pl.BlockSpec((1, tq, BRANCH_W), lambda i, j: (i, j, 0)),
        out_shape=jax.ShapeDtypeStruct((b, lq, BRANCH_W), BF16),
        scratch_shapes=[pltpu.VMEM((nh, SUBLANES, tq), F32), pltpu.VMEM((nh, ATTN_V_ROWS, tq), F32),
                        pltpu.VMEM((tk, tq), F32)],
        compiler_params=_cparams("parallel", "arbitrary"),
        name=name,
    )(q, k, v)
    return out[:, :l]


def _mix_kernel(x_ref, oa_ref, ob_ref, oc_ref, od_ref, wmg_ref, wbr_ref, wo_ref, g_ref, b_ref, o_ref, *, alpha):
    x = x_ref[...]
    xb = x.astype(BF16)
    d = D_MODEL
    acc = None
    for n, br_ref in enumerate((oa_ref, ob_ref, oc_ref, od_ref)):
        gate = jax.nn.sigmoid(_dg(xb, wmg_ref[:, n * d:(n + 1) * d]))
        term = gate * _dg(br_ref[...], wbr_ref[n])
        acc = term if acc is None else acc + term
    mix = _dg(acc.astype(BF16), wo_ref[...])
    o_ref[...] = _layer_norm(alpha * x + mix, g_ref[...], b_ref[...])


def _mix_call(x, oa, ob, oc, od, w_mg, w_br, w_o, g, b, alpha):
    t, d = x.shape
    tm = _token_tile(t)
    tok = lambda n: pl.BlockSpec((tm, n), lambda i: (i, 0))
    return pl.pallas_call(
        functools.partial(_mix_kernel, alpha=alpha),
        grid=(t // tm,),
        in_specs=[tok(d)] + [tok(BRANCH_W)] * 4 + [_resident(w_mg.shape), _resident(w_br.shape),
                                                   _resident(w_o.shape), _resident((1, d)), _resident((1, d))],
        out_specs=tok(d),
        out_shape=jax.ShapeDtypeStruct((t, d), F32),
        compiler_params=_cparams("parallel"),
        name="mix_ln1",
    )(x, oa, ob, oc, od, w_mg, w_br, w_o, g.reshape(1, d), b.reshape(1, d))


def _route(x, rw_ref, rb_ref):
    scores = jax.nn.sigmoid(_dot3(rw_ref[...], x, NT)[:N_EXPERTS])
    biased = scores + rb_ref[...][:N_EXPERTS]
    col = [biased[e:e + 1] for e in range(N_EXPERTS)]
    gs = []
    for g in range(N_GROUPS):
        v = col[g * GROUP_SIZE:(g + 1) * GROUP_SIZE]
        best = None
        for i in range(GROUP_SIZE):
            for j in range(i + 1, GROUP_SIZE):
                s = v[i] + v[j]
                best = s if best is None else jnp.maximum(best, s)
        gs.append(best)
    gates, sels = [], []
    for g in range(N_GROUPS):
        sel = None
        for o in range(N_GROUPS):
            if o == g:
                continue
            cond = (gs[g] > gs[o]) if o < g else (gs[g] >= gs[o])
            sel = cond if sel is None else sel & cond
        sels.append(sel)
        v = col[g * GROUP_SIZE:(g + 1) * GROUP_SIZE]
        for i in range(GROUP_SIZE):
            rank = None
            for j in range(GROUP_SIZE):
                if j == i:
                    continue
                ahead = (v[j] >= v[i]) if j < i else (v[j] > v[i])
                ahead = ahead.astype(F32)
                rank = ahead if rank is None else rank + ahead
            e = g * GROUP_SIZE + i
            gates.append(jnp.where(sel & (rank < 2.0), scores[e:e + 1], 0.0))
    total = gates[0]
    for gt in gates[1:]:
        total = total + gt
    return [gt / total for gt in gates], sels


MOE_POS_LANE = N_EXPERTS
MOE_ROW_FORM = 24
MOE_META_GROUP = 32
MOE_META_COUNT = 64


def _moe_route_kernel(x_ref, rw_ref, rb_ref, gp_ref, posr_ref, meta_ref, *, rows):
    tm = x_ref.shape[0]
    gates, sels = _route(x_ref[...], rw_ref, rb_ref)
    gsel = jnp.concatenate([s.astype(F32) for s in sels] + [jnp.zeros((SUBLANES - N_GROUPS, tm), F32)], axis=0)
    before = (_iota((tm, tm), 0) < _iota((tm, tm), 1)).astype(BF16)
    rank = _dg(gsel.astype(BF16), before)
    count = jnp.sum(gsel, axis=1, keepdims=True)
    offs = [jnp.zeros((1, 1), F32)]
    for g in range(1, N_GROUPS):
        offs.append(offs[-1] + count[g - 1:g])
    pos = gsel[0:1] * rank[0:1]
    for g in range(1, N_GROUPS):
        pos = pos + gsel[g:g + 1] * (offs[g] + rank[g:g + 1])
    gt = jnp.concatenate(gates + [pos, jnp.zeros((LANES - N_EXPERTS - 1, tm), F32)], axis=0)
    posr_ref[0] = gt[:MOE_ROW_FORM]
    gp_ref[...] = gt.T

    lane = _iota((1, LANES), 1)
    lane_g = lane - MOE_META_GROUP
    shift = int(math.log2(rows))
    meta = jnp.zeros((1, LANES), jnp.int32)
    start = jnp.zeros((1, 1), jnp.int32)
    for g in range(N_GROUPS):
        n_g = count[g:g + 1].astype(jnp.int32)
        off_g = offs[g].astype(jnp.int32)
        first = lax.shift_right_arithmetic(off_g, shift)
        last = lax.shift_right_arithmetic(off_g + n_g - 1, shift)
        k_g = jnp.where(n_g > 0, last - first + 1, 0)
        meta = jnp.where((lane >= start) & (lane < start + k_g), first + (lane - start), meta)
        meta = jnp.where((lane_g >= start) & (lane_g < start + k_g), g, meta)
        start = start + k_g
    meta_ref[0] = jnp.where(lane == MOE_META_COUNT, start, meta)


def _moe_sparse_kernel(meta_ref, x_ref, gp_ref, posr_ref, wg_ref, wu_ref, wd_ref, g_ref, b_ref, o_ref,
                       xb_scr, acc_scr, xs_scr, gs_scr, y_scr, back_scr, *, alpha, rows):
    i = pl.program_id(0)
    w = pl.program_id(1)
    tm = x_ref.shape[0]
    blk = meta_ref[i, w]
    grp = meta_ref[i, MOE_META_GROUP + w]
    n_items = meta_ref[i, MOE_META_COUNT]
    valid = w < n_items
    first_of_blk = valid & ((w == 0) | (meta_ref[i, jnp.maximum(w - 1, 0)] != blk))
    last_of_blk = valid & ((w + 1 >= n_items) | (meta_ref[i, w + 1] != blk))
    base = (blk * rows).astype(F32)

    @pl.when(w == 0)
    def _():
        xb_scr[...] = x_ref[...].astype(BF16)
        acc_scr[...] = jnp.zeros(acc_scr.shape, F32)

    @pl.when(first_of_blk)
    def _():
        rowform = posr_ref[0]
        pos_row = rowform[MOE_POS_LANE:MOE_POS_LANE + 1]
        perm = (pos_row == base + _iota((rows, tm), 0).astype(F32)).astype(BF16)
        back = (gp_ref[:, MOE_POS_LANE:MOE_POS_LANE + 1]
                == base + _iota((tm, rows), 1).astype(F32)).astype(BF16)
        back_scr[...] = back
        xs_scr[...] = _dg(perm, xb_scr[...]).astype(BF16)
        gst = _dotc(rowform[:N_EXPERTS], back, pieces=3)
        gst = jnp.concatenate([gst, jnp.zeros((LANES - N_EXPERTS, rows), F32)], axis=0)
        gs_scr[...] = gst.T
        y_scr[...] = jnp.zeros(y_scr.shape, F32)

    @pl.when(valid)
    def _():
        xs = xs_scr[...]
        gs = gs_scr[...]
        lane = _iota(gs.shape, 1)
        y = y_scr[...]
        for e in range(GROUP_SIZE):
            idx = grp * GROUP_SIZE + e
            gate = jnp.sum(jnp.where(lane == idx, gs, 0.0), axis=-1, keepdims=True)
            h = _silu(_dg(xs, wg_ref[idx])) * _dg(xs, wu_ref[idx]) * gate
            y = y + _dg(h.astype(BF16), wd_ref[idx])
        y_scr[...] = y

    @pl.when(last_of_blk)
    def _():
        acc_scr[...] += _dg(back_scr[...], y_scr[...].astype(BF16))

    @pl.when(w == pl.num_programs(1) - 1)
    def _():
        o_ref[...] = _layer_norm(alpha * x_ref[...] + acc_scr[...], g_ref[...], b_ref[...])


def _moe_call(x, router_w, router_b, w_g, w_u, w_d, g, b, alpha):
    t, d = x.shape
    tm = _token_tile(t, MOE_TOKEN_TILE)
    nt = t // tm
    rows = min(MOE_BLOCK, tm)
    tok = pl.BlockSpec((tm, d), lambda i: (i, 0))
    gp, posr, meta = pl.pallas_call(
        functools.partial(_moe_route_kernel, rows=rows),
        grid=(nt,),
        in_specs=[tok, _resident((LANES, d)), _resident((LANES, 1))],
        out_specs=[pl.BlockSpec((tm, LANES), lambda i: (i, 0)), pl.BlockSpec((1, MOE_ROW_FORM, tm), lambda i: (i, 0, 0)),
                   pl.BlockSpec((1, 1, LANES), lambda i: (i, 0, 0))],
        out_shape=[jax.ShapeDtypeStruct((t, LANES), F32), jax.ShapeDtypeStruct((nt, MOE_ROW_FORM, tm), F32),
                   jax.ShapeDtypeStruct((nt, 1, LANES), jnp.int32)],
        compiler_params=_cparams("parallel"),
        name="moe_route",
    )(x, jnp.pad(router_w.T, ((0, LANES - N_EXPERTS), (0, 0))),
      jnp.pad(router_b, (0, LANES - N_EXPERTS)).reshape(LANES, 1))

    n_items = tm // rows + N_GROUPS - 1
    once = lambda shape: pl.BlockSpec(shape, lambda i, w, m: (0,) * len(shape), pipeline_mode=pl.Buffered(1))
    tok2 = lambda n: pl.BlockSpec((tm, n), lambda i, w, m: (i, 0))
    return pl.pallas_call(
        functools.partial(_moe_sparse_kernel, alpha=alpha, rows=rows),
        grid_spec=pltpu.PrefetchScalarGridSpec(
            num_scalar_prefetch=1,
            grid=(nt, n_items),
            in_specs=[tok2(d), tok2(LANES), pl.BlockSpec((1, MOE_ROW_FORM, tm), lambda i, w, m: (i, 0, 0)),
                      once(w_g.shape), once(w_u.shape), once(w_d.shape), once((1, d)), once((1, d))],
            out_specs=tok2(d),
            scratch_shapes=[pltpu.VMEM((tm, d), BF16), pltpu.VMEM((tm, d), F32), pltpu.VMEM((rows, d), BF16),
                            pltpu.VMEM((rows, LANES), F32), pltpu.VMEM((rows, d), F32),
                            pltpu.VMEM((tm, rows), BF16)]),
        out_shape=jax.ShapeDtypeStruct((t, d), F32),
        compiler_params=_cparams("parallel", "arbitrary"),
        name="moe_ln2",
    )(meta.reshape(nt, LANES), x, gp, posr, w_g, w_u, w_d, g.reshape(1, d), b.reshape(1, d))


def _ple_kernel(x_ref, p_ref, pw_ref, gw_ref, g_ref, b_ref, o_ref, *, alpha):
    x = x_ref[...]
    ple = _bdot(p_ref[...], pw_ref[...]) * jax.nn.sigmoid(_bdot(x, gw_ref[...]))
    o_ref[...] = _layer_norm(alpha * x + ple, g_ref[...], b_ref[...])


def _ple_call(x, p, ple_w, gate_w, g, b, alpha):
    t, d = x.shape
    tm = _token_tile(t)
    tok = lambda n: pl.BlockSpec((tm, n), lambda i: (i, 0))
    return pl.pallas_call(
        functools.partial(_ple_kernel, alpha=alpha),
        grid=(t // tm,),
        in_specs=[tok(d), tok(PLE_DIM), _resident(ple_w.shape), _resident(gate_w.shape), _resident((1, d)),
                  _resident((1, d))],
        out_specs=tok(d),
        out_shape=jax.ShapeDtypeStruct((t, d), F32),
        compiler_params=_cparams("parallel"),
        name="ple_ln3",
    )(x, p, ple_w, gate_w, g.reshape(1, d), b.reshape(1, d))


def _lb_kernel(x_ref, o_ref):
    x = x_ref[...]
    depth = x.shape[0]
    m = jnp.max(x, axis=0, keepdims=True)
    e = jnp.exp(x - m)
    pr = e / jnp.sum(e, axis=0, keepdims=True)
    run = jnp.zeros_like(pr[0:1])
    for i in range(depth):
        o_ref[i:i + 1, :] = run
        run = run + pr[i:i + 1]


def _lb_call(logits):
    return pl.pallas_call(
        _lb_kernel,
        out_shape=jax.ShapeDtypeStruct(logits.shape, F32),
        name="hgrn_lb",
    )(logits)


def _pad_rows(x, t):
    pad = t - x.shape[1]
    if pad == 0:
        return x
    return jnp.pad(x, ((0, 0), (0, pad)) + ((0, 0),) * (x.ndim - 2))


def _layer(x, p, st, lw, alpha):
    b, l, d = x.shape
    t = b * l
    hgrn_s, rwkv_s, rwkv_prev, lat_past, kr_past, fk_past, fv_past, flf_past = st
    past = 0 if lat_past is None else lat_past.shape[1]
    t_valid = past + l
    t_pad = -(-t_valid // LANES) * LANES if past else t_valid

    cols = _in_call(x.reshape(t, d), lw["w_in"])
    ca, cb, qlat, kvlat, kr, fq, fk, fv, ff = [c.reshape(b, l, -1) for c in cols]

    st0 = jnp.zeros((b, BRANCH_W, BRANCH_W), F32) if hgrn_s is None else _hgrn_state_to_blockdiag(hgrn_s)
    prev_row = jnp.zeros((b, B_COLS), F32) if rwkv_prev is None else rwkv_prev
    s0 = jnp.zeros((b, N_HEADS, HEAD_DIM, HEAD_DIM), F32) if rwkv_s is None else rwkv_s
    o_a, hgrn_bd, o_b, rwkv_new = _recurrent_call(
        ca, lw["lb"], lw["hgrn_norm_g"], st0, cb, prev_row, s0, lw["rwkv_mu"], lw["rwkv_w0"], lw["rwkv_w2"],
        lw["rwkv_a0"], lw["rwkv_a2"], lw["rwkv_g2"], lw["rwkv_kk"], lw["rwkv_ka"], lw["rwkv_rk"],
        lw["rwkv_lnx_g"], lw["rwkv_lnx_b"])
    hgrn_new = _hgrn_state_from_blockdiag(hgrn_bd)
    shift_new = cb[:, l - 1]

    cos, sin = _rope_tables(past, l)
    q_c, lat_new, kr_new = _mla_q_call(qlat, kvlat, kr, cos, sin, lw["mla_qn_g"], lw["mla_w_uq"], lw["mla_kvn_g"])
    if past:
        c_all = _pad_rows(jnp.concatenate([lat_past, lat_new], axis=1), t_pad)
        kr_all = _pad_rows(jnp.concatenate([kr_past, kr_new], axis=1), t_pad)
    else:
        c_all, kr_all = lat_new, kr_new
    k_c, v_c = _mla_kv_call(c_all, kr_all, lw["mla_w_k"], lw["mla_w_v"])
    o_c = _flash_call(q_c, k_c, v_c, past=past, t_valid=t_valid, chunk_mask=True, name="mla_attn")

    lf_new, lf_wide = _fox_lf_call(ff, lw["fox_bf"])
    if past:
        flf_wide = jnp.pad(flf_past, ((0, 0), (0, 0), (0, LANES - N_HEADS)))
        lf_all = _pad_rows(jnp.concatenate([flf_wide, lf_wide], axis=1), t_pad)
        k_all = _pad_rows(jnp.concatenate([fk_past.reshape(b, past, BRANCH_W), fk], axis=1), t_pad)
        v_all = _pad_rows(jnp.concatenate([fv_past.reshape(b, past, BRANCH_W), fv], axis=1), t_pad)
    else:
        lf_all, k_all, v_all = lf_wide, fk, fv
    k_d, v_d, c_all_f = _fox_kv_call(k_all, v_all, lf_all)
    q_d = _fox_q_call(fq, c_all_f[:, past:past + l])
    o_d = _flash_call(q_d, k_d, v_d, past=past, t_valid=t_valid, chunk_mask=False, name="fox_attn")

    flat = lambda o: o.reshape(t, BRANCH_W)
    x1 = _mix_call(x.reshape(t, d), flat(o_a), flat(o_b), flat(o_c), flat(o_d), lw["w_mg"], lw["w_br"], lw["w_o"],
                   lw["ln1_g"], lw["ln1_b"], alpha)
    x2 = _moe_call(x1, lw["router_w"], lw["router_b"], lw["w_g"], lw["w_u"], lw["w_d"], lw["ln2_g"], lw["ln2_b"],
                   alpha)
    x3 = _ple_call(x2, p.reshape(t, PLE_DIM), lw["ple_w"], lw["ple_gate_w"], lw["ln3_g"], lw["ln3_b"], alpha)
    new = (hgrn_new, rwkv_new, shift_new, lat_new, kr_new, fk.reshape(b, l, N_HEADS, HEAD_DIM),
           fv.reshape(b, l, N_HEADS, HEAD_DIM), lf_new)
    return x3.reshape(b, l, d), new


def _relayout_w_uq(w_uq):
    w = w_uq.reshape(Q_LORA, N_HEADS, NOPE + ROPE)
    w = jnp.pad(w, ((0, 256 - Q_LORA), (0, 0), (0, LANES - NOPE - ROPE)))
    return w.reshape(256, N_HEADS * LANES).astype(BF16)


def _relayout_w_ukv(w_ukv):
    w = w_ukv.reshape(KV_LORA, N_HEADS, NOPE + HEAD_DIM)
    pad = lambda x: jnp.pad(x, ((0, 0), (0, 0), (0, LANES - x.shape[-1]))).reshape(KV_LORA, N_HEADS * LANES)
    return pad(w[..., :NOPE]).astype(BF16), pad(w[..., NOPE:]).astype(BF16)


def kernel(x_prompt, x_sample, state_hgrn, state_rwkv, state_rwkv_shift, cache_mla_latent, cache_mla_krope, cache_fox_k, cache_fox_v, cache_fox_logf, p_prompt, p_sample, ln_in_g, ln_in_b, w_in, hgrn_lb_logits, hgrn_norm_g, rwkv_mu, rwkv_w0, rwkv_w2, rwkv_a0, rwkv_a2, rwkv_g2, rwkv_kk, rwkv_ka, rwkv_rk, rwkv_lnx_g, rwkv_lnx_b, mla_qnorm_g, mla_w_uq, mla_kvnorm_g, mla_w_ukv, fox_bf, w_br, w_mg, w_o, ln1_g, ln1_b, router_w, router_b, exp_w_gate, exp_w_up, exp_w_down, ln2_g, ln2_b, ple_w, ple_gate_w, ln3_g, ln3_b):
    depth = w_in.shape[0]
    alpha = (2 * depth) ** DEPTH_ALPHA_POW
    d = x_prompt.shape[-1]
    lb_all = _lb_call(hgrn_lb_logits)

    def ln_in(x):
        return _ln_call(x.reshape(-1, d), ln_in_g, ln_in_b).reshape(x.shape)

    xp, xs = ln_in(x_prompt), ln_in(x_sample)
    new_p, new_s = [], []
    for i in range(depth):
        w_k, w_v = _relayout_w_ukv(mla_w_ukv[i])
        lw = dict(
            w_in=_relayout_w_in(w_in[i]), lb=lb_all[i], hgrn_norm_g=hgrn_norm_g[i], rwkv_mu=rwkv_mu[i],
            rwkv_w0=rwkv_w0[i], rwkv_w2=rwkv_w2[i], rwkv_a0=rwkv_a0[i], rwkv_a2=rwkv_a2[i], rwkv_g2=rwkv_g2[i],
            rwkv_kk=rwkv_kk[i], rwkv_ka=rwkv_ka[i], rwkv_rk=rwkv_rk[i], rwkv_lnx_g=rwkv_lnx_g[i],
            rwkv_lnx_b=rwkv_lnx_b[i],
            mla_qn_g=jnp.pad(mla_qnorm_g[i], (0, 256 - Q_LORA)).reshape(1, 256), mla_w_uq=_relayout_w_uq(mla_w_uq[i]),
            mla_kvn_g=mla_kvnorm_g[i].reshape(1, KV_LORA), mla_w_k=w_k, mla_w_v=w_v, fox_bf=fox_bf[i],
            w_br=w_br[i].astype(BF16), w_mg=w_mg[i].astype(BF16), w_o=w_o[i].astype(BF16),
            ln1_g=ln1_g[i], ln1_b=ln1_b[i], router_w=router_w, router_b=router_b,
            w_g=exp_w_gate[i].astype(BF16), w_u=exp_w_up[i].astype(BF16), w_d=exp_w_down[i].astype(BF16),
            ln2_g=ln2_g[i], ln2_b=ln2_b[i], ple_w=ple_w[i].astype(BF16), ple_gate_w=ple_gate_w[i].astype(BF16),
            ln3_g=ln3_g[i], ln3_b=ln3_b[i])
        xp, st_p = _layer(xp, p_prompt[i], (None,) * 8, lw, alpha)
        new_p.append(st_p)
        st_in = (state_hgrn[i], state_rwkv[i], state_rwkv_shift[i], cache_mla_latent[i], cache_mla_krope[i],
                 cache_fox_k[i], cache_fox_v[i], cache_fox_logf[i])
        xs, st_s = _layer(xs, p_sample[i], st_in, lw, alpha)
        new_s.append(st_s)
    stack = lambda sts, j: jnp.stack([s[j] for s in sts], axis=0)
    outs_p = tuple(stack(new_p, j) for j in range(8))
    outs_s = tuple(stack(new_s, j) for j in range(8))
    return (xp, xs) + outs_p + outs_s
```

```python
import functools
import math

import numpy as np
import jax
import jax.numpy as jnp
from jax import lax
from jax.experimental import pallas as pl
from jax.experimental.pallas import tpu as pltpu

F32 = jnp.float32
BF16 = jnp.bfloat16

D_MODEL = 1024
N_HEADS = 4
HEAD_DIM = 64
BRANCH_W = N_HEADS * HEAD_DIM
CHUNK = 64
W_LORA, A_LORA, G_LORA = 32, 32, 64
NOPE, ROPE, Q_LORA, KV_LORA = 64, 32, 192, 128
ROPE_THETA = 10000.0
N_EXPERTS, N_GROUPS, EXPERT_FF = 16, 4, 256
GROUP_SIZE = N_EXPERTS // N_GROUPS
PLE_DIM = 256
A_COLS = 4 * BRANCH_W
B_COLS = 3 * BRANCH_W + W_LORA + A_LORA + G_LORA
C_COLS = Q_LORA + KV_LORA + ROPE
D_COLS = 3 * BRANCH_W + N_HEADS
DEPTH_ALPHA_POW = 0.25
LN_EPS = 1e-5
RMS_EPS = 1e-6
RWKV_GN_EPS = 64e-5
MASK_VALUE = -1e30
LOG2E = math.log2(math.e)

LANES = 128
SUBLANES = 8
VMEM_LIMIT_BYTES = 56 * 1024 * 1024

NN = ((1,), (0,))
NT = ((1,), (1,))
TN = ((0,), (0,))


def _dg(a, b, dims=NN):
    return lax.dot_general(a, b, (dims, ((), ())), preferred_element_type=F32)


def _bdot(a, b, dims=NN):
    return _dg(a.astype(BF16), b.astype(BF16), dims)


def _split(x, pieces):
    out = []
    r = x
    for i in range(pieces):
        p = r.astype(BF16)
        out.append(p)
        if i + 1 < pieces:
            r = r - p.astype(F32)
    return out


def _dot3(a, b, dims=NN):
    ah, al = _split(a, 2)
    bh, bl = _split(b, 2)
    return _dg(ah, bh, dims) + (_dg(ah, bl, dims) + _dg(al, bh, dims))


def _dotc(a, c, dims=NN, pieces=3):
    ps = _split(a, pieces)
    acc = _dg(ps[0], c, dims)
    for p in ps[1:]:
        acc = acc + _dg(p, c, dims)
    return acc


def _tree_sum(xs):
    while len(xs) > 1:
        xs = [xs[i] + xs[i + 1] for i in range(0, len(xs) - 1, 2)] + ([xs[-1]] if len(xs) % 2 else [])
    return xs[0]


def _iota(shape, dim):
    return lax.broadcasted_iota(jnp.int32, shape, dim)


def _tri_incl(n, dtype=BF16):
    return (_iota((n, n), 0) >= _iota((n, n), 1)).astype(dtype)


def _head_indicator():
    r = _iota((BRANCH_W, BRANCH_W), 0) // HEAD_DIM
    c = _iota((BRANCH_W, BRANCH_W), 1) // HEAD_DIM
    return (r == c).astype(BF16)


def _layer_norm(x, g, b):
    mu = jnp.mean(x, axis=-1, keepdims=True)
    xc = x - mu
    var = jnp.mean(xc * xc, axis=-1, keepdims=True)
    return xc * lax.rsqrt(var + LN_EPS) * g + b


def _softplus(x):
    return jnp.maximum(x, 0.0) + jnp.log(1.0 + jnp.exp(-jnp.abs(x)))


def _silu(x):
    return x * jax.nn.sigmoid(x)


def _cparams(*sem):
    return pltpu.CompilerParams(dimension_semantics=sem, vmem_limit_bytes=VMEM_LIMIT_BYTES)


def _resident(shape):
    nd = len(shape)
    return pl.BlockSpec(shape, lambda *_: (0,) * nd)


TOKEN_TILE = 512
MOE_TOKEN_TILE = 1024
MOE_BLOCK = 256
HGRN_SUB_BLOCK = 16
ATTN_Q_TILE = 512
ATTN_KV_TILE = 512
ATTN_V_ROWS = 80


def _token_tile(t, largest=TOKEN_TILE):
    tm = largest
    while tm >= SUBLANES:
        if t % tm == 0:
            return tm
        tm //= 2
    raise ValueError(f"token count {t} not a multiple of 8")


def _ln_kernel(x_ref, g_ref, b_ref, o_ref):
    o_ref[...] = _layer_norm(x_ref[...], g_ref[...], b_ref[...])


def _ln_call(x, g, b):
    t, d = x.shape
    tm = _token_tile(t)
    return pl.pallas_call(
        _ln_kernel,
        grid=(t // tm,),
        in_specs=[pl.BlockSpec((tm, d), lambda i: (i, 0)), _resident((1, d)), _resident((1, d))],
        out_specs=pl.BlockSpec((tm, d), lambda i: (i, 0)),
        out_shape=jax.ShapeDtypeStruct((t, d), F32),
        compiler_params=_cparams("parallel"),
        name="ln_in",
    )(x, g.reshape(1, d), b.reshape(1, d))


_IN_SLOTS = (("a", A_COLS, A_COLS), ("b", B_COLS, B_COLS), ("qlat", Q_LORA, 256), ("kvlat", KV_LORA, 128),
             ("kr", ROPE, 128), ("fq", BRANCH_W, BRANCH_W), ("fk", BRANCH_W, BRANCH_W), ("fv", BRANCH_W, BRANCH_W),
             ("ff", N_HEADS, 128))


def _relayout_w_in(w_in):
    parts, off = [], 0
    for _, width, slot in _IN_SLOTS:
        w = w_in[:, off:off + width]
        if slot > width:
            w = jnp.pad(w, ((0, 0), (0, slot - width)))
        parts.append(w)
        off += width
    assert off == w_in.shape[1]
    return jnp.concatenate(parts, axis=1).astype(BF16)


def _in_kernel(x_ref, w_ref, bf_ref, *o_refs):
    xb = x_ref[...].astype(BF16)
    off = 0
    for o_ref, (name, _, slot) in zip(o_refs, _IN_SLOTS):
        cols = _dg(xb, w_ref[:, off:off + slot])
        if name == "ff":
            z = cols + bf_ref[...]
            cols = jnp.where(_iota(z.shape, 1) < N_HEADS, -_softplus(-z), 0.0)
            o_refs[-1][...] = cols[:, :N_HEADS]
        o_ref[...] = cols
        off += slot


def _in_call(x, w_in_p, fox_bf):
    t, d = x.shape
    tm = _token_tile(t)
    n = w_in_p.shape[1]
    widths = [slot for _, _, slot in _IN_SLOTS] + [N_HEADS]
    return pl.pallas_call(
        _in_kernel,
        grid=(t // tm,),
        in_specs=[pl.BlockSpec((tm, d), lambda i: (i, 0)), _resident((d, n)), _resident((1, LANES))],
        out_specs=[pl.BlockSpec((tm, width), lambda i: (i, 0)) for width in widths],
        out_shape=[jax.ShapeDtypeStruct((t, width), F32) for width in widths],
        compiler_params=_cparams("parallel"),
        name="in_proj",
    )(x, w_in_p, jnp.pad(fox_bf, (0, LANES - N_HEADS)).reshape(1, LANES))


def _hgrn_parts(c_ref, lb_ref, ng_ref, s0_ref, o_ref, s_ref, st_scr, *, chunk, group):
    def init():
        st_scr[...] = s0_ref[...]

    def finish():
        s_ref[...] = st_scr[...]

    return init, _hgrn_stages(c_ref, lb_ref, ng_ref, o_ref, st_scr, chunk, group), finish


def _hgrn_stages(c_ref, lb_ref, ng_ref, o_ref, st_scr, chunk, group):
    c = chunk
    w = BRANCH_W
    hd = HEAD_DIM
    nh = N_HEADS
    sb = min(HGRN_SUB_BLOCK, c)
    seqs = range(group)
    tri = _tri_incl(c)
    ind = _head_indicator()
    st_keep = (_iota((w, w), 0) // hd) == (_iota((w, w), 1) // hd)
    hr_keep = (_iota((nh * sb, w), 0) // sb) == (_iota((nh * sb, w), 1) // hd)
    row8 = _iota((SUBLANES, w), 0)
    lb = lb_ref[...]

    sq = []
    for bi in seqs:
        cols = c_ref[bi]
        q, fz, iv, g = cols[:, :w], cols[:, w:2 * w], cols[:, 2 * w:3 * w], cols[:, 3 * w:]
        f = lb + (1.0 - lb) * jax.nn.sigmoid(fz)
        a = _dotc(tri, jnp.log(f) * LOG2E, pieces=3)
        sq.append(dict(q=q, k=1.0 - f, iv=iv, g=g, a=a, a_last=a[c - 1:c, :], outs=[]))
    yield

    for bi, s in zip(seqs, sq):
        st = st_scr[bi]
        s["o"] = _bdot(s["q"] * jnp.exp2(s["a"]), st, NT)
        kd = s["k"] * jnp.exp2(s["a_last"] - s["a"])
        st_scr[bi] = st * jnp.exp2(s["a_last"]) + jnp.where(st_keep, _bdot(s["iv"], kd, TN), 0.0)
    yield

    for tb in range(c // sb):
        lo = tb * sb
        for s in sq:
            q, k, iv, a = s["q"], s["k"], s["iv"], s["a"]
            parts = []
            for rg in range(sb // SUBLANES):
                r0 = lo + rg * SUBLANES
                q8, a8 = q[r0:r0 + SUBLANES], a[r0:r0 + SUBLANES]
                pieces = []
                for src in range(lo, r0 + SUBLANES):
                    d = a8 - a[src:src + 1]
                    if src >= r0:
                        d = jnp.where(row8 >= src - r0, d, MASK_VALUE)
                    pieces.append(q8 * jnp.exp2(d) * k[src:src + 1])
                e = _dotc(jnp.concatenate(pieces, axis=0), ind, pieces=1)
                parts.append(_tree_sum([e[i * SUBLANES:(i + 1) * SUBLANES] * iv[lo + i:lo + i + 1]
                                        for i in range(r0 + SUBLANES - lo)]))
            o_tb = jnp.concatenate(parts, axis=0)
            if tb > 0:
                a_lo = a[lo:lo + 1]
                qh = q[lo:lo + sb] * jnp.exp2(a[lo:lo + sb] - a_lo)
                kh = k[:lo] * jnp.exp2(a_lo - a[:lo])
                qh_rows = jnp.where(hr_keep, jnp.concatenate([qh] * nh, axis=0), 0.0)
                sc = _bdot(qh_rows, kh, NT)
                ov = jnp.where(hr_keep, _bdot(sc, iv[:lo]), 0.0)
                for h in range(nh):
                    o_tb = o_tb + ov[h * sb:(h + 1) * sb]
            s["outs"].append(o_tb)
        yield

    for bi, s in zip(seqs, sq):
        o = s["o"] + jnp.concatenate(s["outs"], axis=0)
        ms = _dotc(o * o, ind, pieces=2) * (1.0 / hd)
        o = o * lax.rsqrt(ms + RMS_EPS) * ng_ref[...] * _silu(s["g"])
        o_ref[bi] = o.astype(o_ref.dtype)


def _hgrn_state_to_blockdiag(s):
    b = s.shape[0]
    out = jnp.zeros((b, N_HEADS, HEAD_DIM, N_HEADS, HEAD_DIM), F32)
    for h in range(N_HEADS):
        out = out.at[:, h, :, h, :].set(jnp.swapaxes(s[:, h], -1, -2))
    return out.reshape(b, BRANCH_W, BRANCH_W)


def _hgrn_state_from_blockdiag(st):
    hs = [st[:, h * HEAD_DIM:(h + 1) * HEAD_DIM, h * HEAD_DIM:(h + 1) * HEAD_DIM] for h in range(N_HEADS)]
    return jnp.swapaxes(jnp.stack(hs, axis=1), -1, -2)


def _rwkv_tdot(a, b, dims=NN):
    return _bdot(a, b, dims)


def _rwkv_sdot(a, b, dims=NN):
    return _dot3(a, b, dims)


def _rwkv_operands(cols, prev, mu_ref, w0_ref, lora_ref, a0_ref, kk_ref, ka_ref, rk_ref):
    shifted = jnp.where(_iota(cols.shape, 0) == 0, prev, pltpu.roll(cols, 1, 0))
    m = cols + (shifted - cols) * mu_ref[...]
    w = BRANCH_W
    r, k, v = m[:, :w], m[:, w:2 * w], m[:, 2 * w:3 * w]
    slab = m[:, 3 * w:]
    lane = _iota(slab.shape, 1)
    act = jnp.where(lane < W_LORA, jnp.tanh(slab),
                    jnp.where(lane < W_LORA + A_LORA, slab, jax.nn.sigmoid(slab)))
    lora = _bdot(act, lora_ref[...])
    w_log = -_softplus(-(w0_ref[...] + lora[:, :w])) - 0.5
    a_rate = jax.nn.sigmoid(a0_ref[...] + lora[:, w:2 * w])
    g = lora[:, 2 * w:]
    ind = _head_indicator()
    kk = k * kk_ref[...]
    norm = jnp.sqrt(_dotc(kk * kk, ind, pieces=3))
    kk = kk / jnp.maximum(norm, 1e-12)
    kh = k * (1.0 + (a_rate - 1.0) * ka_ref[...])
    return dict(r=r, lw=-jnp.exp(w_log), k=kh, v=v, a=-kk, b=kk * a_rate, g=g,
                bonus=_dotc(r * kh * rk_ref[...], ind, pieces=3) * v)


def _rwkv_parts(c_ref, prow_ref, mu_ref, w0_ref, lora_ref, a0_ref, kk_ref, ka_ref, rk_ref, lg_ref, lb_ref,
                s0_ref, o_ref, s_ref, st_scr, prev_scr, *, chunk, group):
    hd = HEAD_DIM
    blocks = [(bi, slice(h * hd, (h + 1) * hd)) for bi in range(group) for h in range(N_HEADS)]

    def init():
        st_scr[...] = jnp.zeros(st_scr.shape, F32)
        prev_scr[...] = prow_ref[...]
        for bi, sl in blocks:
            st_scr[bi, sl, sl] = s0_ref[bi, sl.start // hd]

    def finish():
        for bi, sl in blocks:
            s_ref[bi, sl.start // hd] = st_scr[bi, sl, sl]

    stages = _rwkv_stages(c_ref, mu_ref, w0_ref, lora_ref, a0_ref, kk_ref, ka_ref, rk_ref, lg_ref, lb_ref, o_ref,
                          st_scr, prev_scr, chunk, group)
    return init, stages, finish


def _rwkv_stages(c_ref, mu_ref, w0_ref, lora_ref, a0_ref, kk_ref, ka_ref, rk_ref, lg_ref, lb_ref, o_ref, st_scr,
                 prev_scr, chunk, group):
    c = chunk
    w = BRANCH_W
    hd = HEAD_DIM
    nh = N_HEADS
    n = nh * c
    seqs = range(group)
    tri = _tri_incl(c)
    ind = _head_indicator()
    t_w = _iota((c, n), 0)
    s_w = _iota((c, n), 1) % c
    strict_w = t_w > s_w
    incl_w = t_w >= s_w
    eye_w = (t_w == s_w).astype(F32)
    bd_keep = (_iota((n, n), 0) // c) == (_iota((n, n), 1) // c)
    hr_keep = (_iota((n, w), 0) // c) == (_iota((n, w), 1) // hd)
    st_keep = (_iota((w, w), 0) // hd) == (_iota((w, w), 1) // hd)

    def bd(x):
        return jnp.where(bd_keep, jnp.concatenate([x] * nh, axis=0), 0.0)

    def head_rows(x):
        return jnp.where(hr_keep, jnp.concatenate([x] * nh, axis=0), 0.0)

    sq = []
    for bi in seqs:
        cols = c_ref[bi]
        ops = _rwkv_operands(cols, prev_scr[bi], mu_ref, w0_ref, lora_ref, a0_ref, kk_ref, ka_ref, rk_ref)
        prev_scr[bi] = cols[c - 1:c]
        r, lw, k, v, a, b = (ops[name] for name in ("r", "lw", "k", "v", "a", "b"))
        cum = _dotc(tri, lw, pieces=3)
        last = cum[c - 1:c, :]
        e_neg = jnp.exp(-cum)
        e_last = jnp.exp(last - cum)
        sq.append(dict(v=v, last=last, g=ops["g"], bonus=ops["bonus"],
                       lhs=jnp.concatenate([a * jnp.exp(cum - lw), r * jnp.exp(cum)], axis=0),
                       bt=b * e_neg, kt=k * e_neg, hat=jnp.concatenate([b * e_last, k * e_last], axis=0)))
    yield

    for q in sq:
        pb = _rwkv_tdot(q["lhs"], head_rows(q["bt"]), NT)
        pk = _rwkv_tdot(q["lhs"], head_rows(q["kt"]), NT)
        q["l_ab"] = jnp.where(strict_w, pb[:c], 0.0)
        q["a_rb"] = jnp.where(incl_w, pb[c:], 0.0)
        q["lk"] = jnp.concatenate([jnp.where(strict_w, pk[:c], 0.0), jnp.where(incl_w, pk[c:], 0.0)], axis=0)
        q["tinv"] = eye_w + q["l_ab"]
        q["p"] = q["l_ab"]
    yield

    for _ in range(int(math.log2(c)) - 1):
        for q in sq:
            q["p"] = _rwkv_tdot(q["p"], bd(q["p"]))
        yield
        for q in sq:
            q["tinv"] = q["tinv"] + _rwkv_tdot(q["tinv"], bd(q["p"]))
        yield

    for bi, q in zip(seqs, sq):
        q["st"] = st_scr[bi]
        q["sh"] = _rwkv_sdot(q["lhs"], q["st"], NT)
        q["lv"] = _rwkv_tdot(q["lk"], head_rows(q["v"]))
    yield
    for q in sq:
        q["u"] = _rwkv_tdot(q["tinv"], head_rows(q["sh"][:c] + q["lv"][:c]))
    yield
    for bi, q in zip(seqs, sq):
        upd = _rwkv_sdot(jnp.concatenate([q["u"], q["v"]], axis=0), q["hat"], TN)
        st_scr[bi] = q["st"] * jnp.exp(q["last"]) + jnp.where(st_keep, upd, 0.0)
    yield
    for q in sq:
        q["y"] = q["sh"][c:] + _rwkv_tdot(q["a_rb"], head_rows(q["u"])) + q["lv"][c:]
    yield

    for bi, q in zip(seqs, sq):
        y = q["y"]
        mu = _dotc(y, ind, pieces=3) * (1.0 / hd)
        yc = y - mu
        var = _dotc(yc * yc, ind, pieces=2) * (1.0 / hd)
        yn = yc * lax.rsqrt(var + RWKV_GN_EPS) * lg_ref[...] + lb_ref[...]
        o_ref[bi] = ((yn + q["bonus"]) * q["g"]).astype(o_ref.dtype)


def _batch_group(b):
    return 4 if b % 4 == 0 else 1


N_HGRN_IN, N_RWKV_IN = 4, 12
RWKV_STAGES_PER_HGRN_STAGE = 1


def _recurrent_kernel(*refs, chunk, group):
    h_in, refs = refs[:N_HGRN_IN], refs[N_HGRN_IN:]
    r_in, refs = refs[:N_RWKV_IN], refs[N_RWKV_IN:]
    h_o, h_s, r_o, r_s, h_st, r_st, r_prev = refs
    h_init, h_stages, h_finish = _hgrn_parts(*h_in, h_o, h_s, h_st, chunk=chunk, group=group)
    r_init, r_stages, r_finish = _rwkv_parts(*r_in, r_o, r_s, r_st, r_prev, chunk=chunk, group=group)
    j = pl.program_id(1)

    @pl.when(j == 0)
    def _():
        h_init()
        r_init()

    live = [r_stages] * RWKV_STAGES_PER_HGRN_STAGE + [h_stages]
    while live:
        for stages in list(live):
            if stages in live and next(stages, "done") == "done":
                live = [s for s in live if s is not stages]

    @pl.when(j == pl.num_programs(1) - 1)
    def _():
        h_finish()
        r_finish()


def _recurrent_call(cols_a, lb, norm_g, hgrn_st0, cols_b, prev_row, rwkv_s0, mu, w0, w2, a0, a2, g2, kk_s, ka, rk,
                    lnx_g, lnx_b):
    b, l, nb = cols_b.shape
    w = BRANCH_W
    c = CHUNK if l % CHUNK == 0 else l
    bg = _batch_group(b)
    row = lambda x: x.reshape(1, -1)
    lora_w = jnp.zeros((LANES, 3 * w), F32)
    lora_w = lora_w.at[:W_LORA, :w].set(w2).at[W_LORA:W_LORA + A_LORA, w:2 * w].set(a2)
    lora_w = lora_w.at[W_LORA + A_LORA:, 2 * w:].set(g2).astype(BF16)
    tok = lambda n: pl.BlockSpec((bg, c, n), lambda i, j: (i, j, 0))
    hst_spec = pl.BlockSpec((bg, w, w), lambda i, j: (i, 0, 0))
    rst_spec = pl.BlockSpec((bg, N_HEADS, HEAD_DIM, HEAD_DIM), lambda i, j: (i, 0, 0, 0))
    hgrn_in = [tok(A_COLS), _resident((1, w)), _resident((1, w)), hst_spec]
    rwkv_in = ([tok(nb), pl.BlockSpec((bg, 1, nb), lambda i, j: (i, 0, 0)), _resident((1, nb)), _resident((1, w)),
                _resident((LANES, 3 * w))] + [_resident((1, w))] * 6 + [rst_spec])
    assert len(hgrn_in) == N_HGRN_IN and len(rwkv_in) == N_RWKV_IN
    return pl.pallas_call(
        functools.partial(_recurrent_kernel, chunk=c, group=bg),
        grid=(b // bg, l // c),
        in_specs=hgrn_in + rwkv_in,
        out_specs=[tok(w), hst_spec, tok(w), rst_spec],
        out_shape=[jax.ShapeDtypeStruct((b, l, w), BF16), jax.ShapeDtypeStruct((b, w, w), F32),
                   jax.ShapeDtypeStruct((b, l, w), BF16),
                   jax.ShapeDtypeStruct((b, N_HEADS, HEAD_DIM, HEAD_DIM), F32)],
        scratch_shapes=[pltpu.VMEM((bg, w, w), F32), pltpu.VMEM((bg, w, w), F32), pltpu.VMEM((bg, 1, nb), F32)],
        compiler_params=_cparams("parallel", "arbitrary"),
        name="hgrn_rwkv",
    )(cols_a, row(lb), row(norm_g), hgrn_st0,
      cols_b, prev_row.reshape(b, 1, nb), row(mu), row(w0), lora_w, row(a0), row(kk_s), row(ka), row(rk),
      row(lnx_g), row(lnx_b), rwkv_s0)


def _rope_tables(p, l):
    half = ROPE // 2
    inv = 1.0 / (ROPE_THETA ** (jnp.arange(half, dtype=F32) / half))
    ang = (p + jnp.arange(l, dtype=jnp.int32)).astype(F32)[:, None] * inv[None, :]
    cos, sin = jnp.cos(ang), jnp.sin(ang)
    pad = jnp.zeros((l, LANES - ROPE), F32)
    return jnp.concatenate([cos, cos, pad], axis=1), jnp.concatenate([-sin, sin, pad], axis=1)


def _swap_halves(x, base):
    half = ROPE // 2
    n = x.shape[-1]
    lane = _iota(x.shape, x.ndim - 1) % LANES
    up = pltpu.roll(x, n - half, x.ndim - 1)
    down = pltpu.roll(x, half, x.ndim - 1)
    return jnp.where((lane >= base) & (lane < base + half), up,
                     jnp.where((lane >= base + half) & (lane < base + ROPE), down, 0.0))


def _rms_norm(x, g, width):
    ms = jnp.sum(x * x, axis=-1, keepdims=True) * (1.0 / width)
    return x * lax.rsqrt(ms + RMS_EPS) * g


def _mla_q_kernel(ql_ref, kvl_ref, kr_ref, cos_ref, sin_ref, qg_ref, wuq_ref, kvg_ref, q_ref, c_ref, krn_ref):
    scale = (NOPE + ROPE) ** -0.5 * LOG2E
    qn = _rms_norm(ql_ref[0], qg_ref[...], Q_LORA)
    q = _bdot(qn, wuq_ref[...])
    cos, sin = cos_ref[...], sin_ref[...]
    lane = _iota(cos.shape, 1)
    cos_q = jnp.where(lane < NOPE, 1.0, pltpu.roll(cos, NOPE, 1))
    sin_q = pltpu.roll(sin, NOPE, 1)
    for h in range(N_HEADS):
        qh = q[:, h * LANES:(h + 1) * LANES]
        qh = qh * cos_q + _swap_halves(qh, NOPE) * sin_q
        q_ref[0, h] = (qh * scale).astype(q_ref.dtype)
    c_ref[0] = _rms_norm(kvl_ref[0], kvg_ref[...], KV_LORA)
    kr = kr_ref[0]
    krn = kr * cos + _swap_halves(kr, 0) * sin
    krn_ref[0] = krn[:, :ROPE]


def _mla_q_call(qlat, kvlat, kr, cos, sin, qn_g, w_uq_p, kvn_g):
    b, l, _ = qlat.shape
    tm = _token_tile(l)
    tok = lambda n: pl.BlockSpec((1, tm, n), lambda i, j: (i, j, 0))
    tab = pl.BlockSpec((tm, LANES), lambda i, j: (j, 0))
    return pl.pallas_call(
        _mla_q_kernel,
        grid=(b, l // tm),
        in_specs=[tok(256), tok(KV_LORA), tok(LANES), tab, tab, _resident((1, 256)),
                  _resident((256, N_HEADS * LANES)), _resident((1, KV_LORA))],
        out_specs=[pl.BlockSpec((1, N_HEADS, tm, LANES), lambda i, j: (i, 0, j, 0)), tok(KV_LORA), tok(ROPE)],
        out_shape=[jax.ShapeDtypeStruct((b, N_HEADS, l, LANES), BF16), jax.ShapeDtypeStruct((b, l, KV_LORA), F32),
                   jax.ShapeDtypeStruct((b, l, ROPE), F32)],
        compiler_params=_cparams("parallel", "parallel"),
        name="mla_q",
    )(qlat, kvlat, kr, cos, sin, qn_g, w_uq_p, kvn_g)


def _mla_kv_kernel(c_ref, kr_ref, wk_ref, wv_ref, k_ref, v_ref):
    cb = c_ref[0].astype(BF16)
    kall = _dg(cb, wk_ref[...])
    vall = _dg(cb, wv_ref[...])
    tm = cb.shape[0]
    place = (_iota((ROPE, LANES), 0) + NOPE == _iota((ROPE, LANES), 1)).astype(BF16)
    kr_slot = _dg(kr_ref[0].astype(BF16), place)
    one_slot = (_iota((tm, LANES), 1) == HEAD_DIM).astype(F32)
    for h in range(N_HEADS):
        k_ref[0, h] = (kall[:, h * LANES:(h + 1) * LANES] + kr_slot).astype(k_ref.dtype)
        v_ref[0, h, 0] = (vall[:, h * LANES:(h + 1) * LANES] + one_slot).T[:ATTN_V_ROWS].astype(v_ref.dtype)


def _kv_specs(b, t, tm):
    kspec = pl.BlockSpec((1, N_HEADS, tm, LANES), lambda i, j: (i, 0, j, 0))
    vspec = pl.BlockSpec((1, N_HEADS, 1, ATTN_V_ROWS, tm), lambda i, j: (i, 0, j, 0, 0))
    shapes = [jax.ShapeDtypeStruct((b, N_HEADS, t, LANES), BF16),
              jax.ShapeDtypeStruct((b, N_HEADS, t // tm, ATTN_V_ROWS, tm), BF16)]
    return [kspec, vspec], shapes


def _mla_kv_call(c_all, kr_all, w_k_p, w_v_p):
    b, t, _ = c_all.shape
    tm = _token_tile(t, ATTN_KV_TILE)
    tok = lambda n: pl.BlockSpec((1, tm, n), lambda i, j: (i, j, 0))
    out_specs, out_shape = _kv_specs(b, t, tm)
    return pl.pallas_call(
        _mla_kv_kernel,
        grid=(b, t // tm),
        in_specs=[tok(KV_LORA), tok(ROPE), _resident((KV_LORA, N_HEADS * LANES)),
                  _resident((KV_LORA, N_HEADS * LANES))],
        out_specs=out_specs,
        out_shape=out_shape,
        compiler_params=_cparams("parallel", "parallel"),
        name="mla_kv",
    )(c_all, kr_all, w_k_p, w_v_p)


def _head_slot(x, h):
    pair = x[:, (h // 2) * LANES:(h // 2 + 1) * LANES]
    return pair if h % 2 == 0 else pltpu.roll(pair, HEAD_DIM, 1)


FOX_KEY_BIAS = HEAD_DIM
FOX_QUERY_BIAS = HEAD_DIM + 3 * N_HEADS


def _bias_pieces(c, base):
    p0 = c.astype(BF16).astype(F32)
    r1 = c - p0
    p1 = r1.astype(BF16).astype(F32)
    p2 = r1 - p1
    return (pltpu.roll(p0, base, 1) + pltpu.roll(p1, base + N_HEADS, 1)) + pltpu.roll(p2, base + 2 * N_HEADS, 1)


def _bias_ones(shape, base, h):
    lane = _iota(shape, 1)
    hit = (lane == base + h) | (lane == base + N_HEADS + h) | (lane == base + 2 * N_HEADS + h)
    return hit.astype(F32)


def _fox_q_kernel(q_ref, c_ref, o_ref):
    scale = HEAD_DIM ** -0.5 * LOG2E
    q = q_ref[0]
    pieces = _bias_pieces(c_ref[0] * LOG2E, FOX_QUERY_BIAS)
    lane = _iota(pieces.shape, 1)
    for h in range(N_HEADS):
        bias = pieces + _bias_ones(pieces.shape, FOX_KEY_BIAS, h)
        o_ref[0, h] = jnp.where(lane < HEAD_DIM, _head_slot(q, h) * scale, bias).astype(o_ref.dtype)


def _fox_kv_kernel(k_ref, v_ref, lf_ref, ko_ref, vo_ref, c_ref, carry):
    @pl.when(pl.program_id(1) == 0)
    def _():
        carry[...] = jnp.zeros_like(carry)

    k, v, lf = k_ref[0], v_ref[0], lf_ref[0]
    tm = k.shape[0]
    c = _dotc(_tri_incl(tm), lf, pieces=3) + carry[...]
    c_ref[0] = c
    carry[...] = c[tm - 1:, :]
    pieces = _bias_pieces(-c * LOG2E, FOX_KEY_BIAS)
    lane = _iota((tm, LANES), 1)
    for h in range(N_HEADS):
        bias = pieces + _bias_ones(pieces.shape, FOX_QUERY_BIAS, h)
        ko_ref[0, h] = jnp.where(lane < HEAD_DIM, _head_slot(k, h), bias).astype(ko_ref.dtype)
        vo_ref[0, h, 0] = jnp.where(lane < HEAD_DIM, _head_slot(v, h),
                                    (lane == HEAD_DIM).astype(F32)).T[:ATTN_V_ROWS].astype(vo_ref.dtype)


def _fox_q_call(q, c_new):
    b, l, w = q.shape
    tm = _token_tile(l)
    return pl.pallas_call(
        _fox_q_kernel,
        grid=(b, l // tm),
        in_specs=[pl.BlockSpec((1, tm, w), lambda i, j: (i, j, 0)),
                  pl.BlockSpec((1, tm, LANES), lambda i, j: (i, j, 0))],
        out_specs=pl.BlockSpec((1, N_HEADS, tm, LANES), lambda i, j: (i, 0, j, 0)),
        out_shape=jax.ShapeDtypeStruct((b, N_HEADS, l, LANES), BF16),
        compiler_params=_cparams("parallel", "parallel"),
        name="fox_q",
    )(q, c_new)


def _fox_kv_call(k_all, v_all, lf_all):
    b, t, w = k_all.shape
    tm = _token_tile(t, ATTN_KV_TILE)
    tok = pl.BlockSpec((1, tm, w), lambda i, j: (i, j, 0))
    wide = pl.BlockSpec((1, tm, LANES), lambda i, j: (i, j, 0))
    out_specs, out_shape = _kv_specs(b, t, tm)
    return pl.pallas_call(
        _fox_kv_kernel,
        grid=(b, t // tm),
        in_specs=[tok, tok, wide],
        out_specs=out_specs + [wide],
        out_shape=out_shape + [jax.ShapeDtypeStruct((b, t, LANES), F32)],
        scratch_shapes=[pltpu.VMEM((1, LANES), F32)],
        compiler_params=_cparams("parallel", "arbitrary"),
        name="fox_kv",
    )(k_all, v_all, lf_all)


def _flash_kernel(q_ref, k_ref, v_ref, o_ref, m_scr, acc_scr, s_scr, *, tq, tk, past, t_valid, chunk_mask):
    qi = pl.program_id(1)
    q_start = past + qi * tq
    if chunk_mask:
        vis_end = jnp.minimum(((q_start + tq + CHUNK - 1) // CHUNK) * CHUNK, t_valid)
    else:
        vis_end = q_start + tq
    n_blocks = (vis_end + tk - 1) // tk
    n_full = q_start // tk

    m_scr[...] = jnp.full(m_scr.shape, MASK_VALUE, F32)
    acc_scr[...] = jnp.zeros(acc_scr.shape, F32)

    def logits(h, kb):
        ks = pl.multiple_of(kb * tk, tk)
        return _dg(k_ref[0, h, pl.ds(ks, tk), :], q_ref[0, h], NT)

    s_scr[...] = logits(0, 0)

    def block(kb, masked, s_next):
        if masked:
            kpos = kb * tk + _iota((tk, tq), 0)
            qpos = q_start + _iota((tk, tq), 1)
            if chunk_mask:
                keep = ((kpos // CHUNK) <= (qpos // CHUNK)) & (kpos < t_valid)
            else:
                keep = kpos <= qpos
        for h in range(N_HEADS):
            s = jnp.where(keep, s_next, MASK_VALUE) if masked else s_next
            if h + 1 < N_HEADS:
                s_next = logits(h + 1, kb)
            else:
                s_next = logits(0, jnp.minimum(kb + 1, n_blocks - 1))
            m_old = m_scr[h]
            m_new = jnp.maximum(m_old, jnp.max(s, axis=0, keepdims=True))
            p = jnp.exp2(s - m_new[:1])
            pv = _dg(v_ref[0, h, kb], p.astype(BF16))
            acc_scr[h] = acc_scr[h] * jnp.exp2(m_old - m_new)[:1] + pv
            m_scr[h] = m_new
        return s_next

    def full_body(kb, carry):
        s_scr[...] = block(kb, False, s_scr[...])
        return carry

    def masked_body(kb, carry):
        s_scr[...] = block(kb, True, s_scr[...])
        return carry

    lax.fori_loop(0, n_full, full_body, 0)
    lax.fori_loop(n_full, n_blocks, masked_body, 0)
    for h in range(N_HEADS):
        acc = acc_scr[h]
        acc = jnp.concatenate([acc, jnp.zeros((LANES - ATTN_V_ROWS, tq), F32)], axis=0).T
        o = acc[:, :HEAD_DIM] / acc[:, HEAD_DIM:HEAD_DIM + 1]
        o_ref[0, :, h * HEAD_DIM:(h + 1) * HEAD_DIM] = o.astype(o_ref.dtype)


def _flash_call(q, k, v, *, past, t_valid, chunk_mask, name):
    b, nh, l, _ = q.shape
    t = k.shape[2]
    tk = v.shape[-1]
    lq = max(l, LANES)
    if lq != l:
        q = jnp.pad(q, ((0, 0), (0, 0), (0, lq - l), (0, 0)))
    tq = _token_tile(lq, ATTN_Q_TILE)
    out = pl.pallas_call(
        functools.partial(_flash_kernel, tq=tq, tk=tk, past=past, t_valid=t_valid, chunk_mask=chunk_mask),
        grid=(b, lq // tq),
        in_specs=[pl.BlockSpec((1, nh, tq, LANES), lambda i, j: (i, 0, j, 0)),
                  pl.BlockSpec((1, nh, t, LANES), lambda i, j: (i, 0, 0, 0)),
                  pl.BlockSpec((1, nh, t // tk, ATTN_V_ROWS, tk), lambda i, j: (i, 0, 0, 0, 0))],
        out_specs=pl.BlockSpec((1, tq, BRANCH_W), lambda i, j: (i, j, 0)),
        out_shape=jax.ShapeDtypeStruct((b, lq, BRANCH_W), BF16),
        scratch_shapes=[pltpu.VMEM((nh, SUBLANES, tq), F32), pltpu.VMEM((nh, ATTN_V_ROWS, tq), F32),
                        pltpu.VMEM((tk, tq), F32)],
        compiler_params=_cparams("parallel", "arbitrary"),
        name=name,
    )(q, k, v)
    return out[:, :l]


def _mix_kernel(x_ref, oa_ref, ob_ref, oc_ref, od_ref, wmg_ref, wbr_ref, wo_ref, g_ref, b_ref, o_ref, *, alpha):
    x = x_ref[...]
    xb = x.astype(BF16)
    d = D_MODEL
    acc = None
    for n, br_ref in enumerate((oa_ref, ob_ref, oc_ref, od_ref)):
        gate = jax.nn.sigmoid(_dg(xb, wmg_ref[:, n * d:(n + 1) * d]))
        term = gate * _dg(br_ref[...], wbr_ref[n])
        acc = term if acc is None else acc + term
    mix = _dg(acc.astype(BF16), wo_ref[...])
    o_ref[...] = _layer_norm(alpha * x + mix, g_ref[...], b_ref[...])


def _mix_call(x, oa, ob, oc, od, w_mg, w_br, w_o, g, b, alpha):
    t, d = x.shape
    tm = _token_tile(t)
    tok = lambda n: pl.BlockSpec((tm, n), lambda i: (i, 0))
    return pl.pallas_call(
        functools.partial(_mix_kernel, alpha=alpha),
        grid=(t // tm,),
        in_specs=[tok(d)] + [tok(BRANCH_W)] * 4 + [_resident(w_mg.shape), _resident(w_br.shape),
                                                   _resident(w_o.shape), _resident((1, d)), _resident((1, d))],
        out_specs=tok(d),
        out_shape=jax.ShapeDtypeStruct((t, d), F32),
        compiler_params=_cparams("parallel"),
        name="mix_ln1",
    )(x, oa, ob, oc, od, w_mg, w_br, w_o, g.reshape(1, d), b.reshape(1, d))


def _route(x, rw_ref, rb_ref):
    scores = jax.nn.sigmoid(_dot3(rw_ref[...], x, NT)[:N_EXPERTS])
    biased = scores + rb_ref[...][:N_EXPERTS]
    col = [biased[e:e + 1] for e in range(N_EXPERTS)]
    gs = []
    for g in range(N_GROUPS):
        v = col[g * GROUP_SIZE:(g + 1) * GROUP_SIZE]
        best = None
        for i in range(GROUP_SIZE):
            for j in range(i + 1, GROUP_SIZE):
                s = v[i] + v[j]
                best = s if best is None else jnp.maximum(best, s)
        gs.append(best)
    gates, sels = [], []
    for g in range(N_GROUPS):
        sel = None
        for o in range(N_GROUPS):
            if o == g:
                continue
            cond = (gs[g] > gs[o]) if o < g else (gs[g] >= gs[o])
            sel = cond if sel is None else sel & cond
        sels.append(sel)
        v = col[g * GROUP_SIZE:(g + 1) * GROUP_SIZE]
        for i in range(GROUP_SIZE):
            rank = None
            for j in range(GROUP_SIZE):
                if j == i:
                    continue
                ahead = (v[j] >= v[i]) if j < i else (v[j] > v[i])
                ahead = ahead.astype(F32)
                rank = ahead if rank is None else rank + ahead
            e = g * GROUP_SIZE + i
            gates.append(jnp.where(sel & (rank < 2.0), scores[e:e + 1], 0.0))
    total = gates[0]
    for gt in gates[1:]:
        total = total + gt
    return [gt / total for gt in gates], sels


MOE_POS_LANE = N_EXPERTS
MOE_ROW_FORM = 24
MOE_META_GROUP = 32
MOE_META_COUNT = 64


def _moe_route_kernel(x_ref, rw_ref, rb_ref, gp_ref, posr_ref, meta_ref, *, rows):
    tm = x_ref.shape[0]
    gates, sels = _route(x_ref[...], rw_ref, rb_ref)
    gsel = jnp.concatenate([s.astype(F32) for s in sels] + [jnp.zeros((SUBLANES - N_GROUPS, tm), F32)], axis=0)
    before = (_iota((tm, tm), 0) < _iota((tm, tm), 1)).astype(BF16)
    rank = _dg(gsel.astype(BF16), before)
    count = jnp.sum(gsel, axis=1, keepdims=True)
    offs = [jnp.zeros((1, 1), F32)]
    for g in range(1, N_GROUPS):
        offs.append(offs[-1] + count[g - 1:g])
    pos = gsel[0:1] * rank[0:1]
    for g in range(1, N_GROUPS):
        pos = pos + gsel[g:g + 1] * (offs[g] + rank[g:g + 1])
    gt = jnp.concatenate(gates + [pos, jnp.zeros((LANES - N_EXPERTS - 1, tm), F32)], axis=0)
    posr_ref[0] = gt[:MOE_ROW_FORM]
    gp_ref[...] = gt.T

    lane = _iota((1, LANES), 1)
    lane_g = lane - MOE_META_GROUP
    shift = int(math.log2(rows))
    meta = jnp.zeros((1, LANES), jnp.int32)
    start = jnp.zeros((1, 1), jnp.int32)
    for g in range(N_GROUPS):
        n_g = count[g:g + 1].astype(jnp.int32)
        off_g = offs[g].astype(jnp.int32)
        first = lax.shift_right_arithmetic(off_g, shift)
        last = lax.shift_right_arithmetic(off_g + n_g - 1, shift)
        k_g = jnp.where(n_g > 0, last - first + 1, 0)
        meta = jnp.where((lane >= start) & (lane < start + k_g), first + (lane - start), meta)
        meta = jnp.where((lane_g >= start) & (lane_g < start + k_g), g, meta)
        start = start + k_g
    meta_ref[0] = jnp.where(lane == MOE_META_COUNT, start, meta)


def _moe_sparse_kernel(meta_ref, x_ref, gp_ref, posr_ref, wg_ref, wu_ref, wd_ref, g_ref, b_ref, o_ref,
                       xb_scr, acc_scr, xs_scr, gs_scr, y_scr, back_scr, *, alpha, rows):
    i = pl.program_id(0)
    w = pl.program_id(1)
    tm = x_ref.shape[0]
    blk = meta_ref[i, w]
    grp = meta_ref[i, MOE_META_GROUP + w]
    n_items = meta_ref[i, MOE_META_COUNT]
    valid = w < n_items
    first_of_blk = valid & ((w == 0) | (meta_ref[i, jnp.maximum(w - 1, 0)] != blk))
    last_of_blk = valid & ((w + 1 >= n_items) | (meta_ref[i, w + 1] != blk))
    base = (blk * rows).astype(F32)

    @pl.when(w == 0)
    def _():
        xb_scr[...] = x_ref[...].astype(BF16)
        acc_scr[...] = jnp.zeros(acc_scr.shape, F32)

    @pl.when(first_of_blk)
    def _():
        rowform = posr_ref[0]
        pos_row = rowform[MOE_POS_LANE:MOE_POS_LANE + 1]
        perm = (pos_row == base + _iota((rows, tm), 0).astype(F32)).astype(BF16)
        back = (gp_ref[:, MOE_POS_LANE:MOE_POS_LANE + 1]
                == base + _iota((tm, rows), 1).astype(F32)).astype(BF16)
        back_scr[...] = back
        xs_scr[...] = _dg(perm, xb_scr[...]).astype(BF16)
        gst = _dotc(rowform[:N_EXPERTS], back, pieces=3)
        gst = jnp.concatenate([gst, jnp.zeros((LANES - N_EXPERTS, rows), F32)], axis=0)
        gs_scr[...] = gst.T
        y_scr[...] = jnp.zeros(y_scr.shape, F32)

    @pl.when(valid)
    def _():
        xs = xs_scr[...]
        gs = gs_scr[...]
        lane = _iota(gs.shape, 1)
        y = y_scr[...]
        for e in range(GROUP_SIZE):
            idx = grp * GROUP_SIZE + e
            gate = jnp.sum(jnp.where(lane == idx, gs, 0.0), axis=-1, keepdims=True)
            h = _silu(_dg(xs, wg_ref[idx])) * _dg(xs, wu_ref[idx]) * gate
            y = y + _dg(h.astype(BF16), wd_ref[idx])
        y_scr[...] = y

    @pl.when(last_of_blk)
    def _():
        acc_scr[...] += _dg(back_scr[...], y_scr[...].astype(BF16))

    @pl.when(w == pl.num_programs(1) - 1)
    def _():
        o_ref[...] = _layer_norm(alpha * x_ref[...] + acc_scr[...], g_ref[...], b_ref[...])


def _moe_call(x, router_w, router_b, w_g, w_u, w_d, g, b, alpha):
    t, d = x.shape
    tm = _token_tile(t, MOE_TOKEN_TILE)
    nt = t // tm
    rows = min(MOE_BLOCK, tm)
    tok = pl.BlockSpec((tm, d), lambda i: (i, 0))
    gp, posr, meta = pl.pallas_call(
        functools.partial(_moe_route_kernel, rows=rows),
        grid=(nt,),
        in_specs=[tok, _resident((LANES, d)), _resident((LANES, 1))],
        out_specs=[pl.BlockSpec((tm, LANES), lambda i: (i, 0)), pl.BlockSpec((1, MOE_ROW_FORM, tm), lambda i: (i, 0, 0)),
                   pl.BlockSpec((1, 1, LANES), lambda i: (i, 0, 0))],
        out_shape=[jax.ShapeDtypeStruct((t, LANES), F32), jax.ShapeDtypeStruct((nt, MOE_ROW_FORM, tm), F32),
                   jax.ShapeDtypeStruct((nt, 1, LANES), jnp.int32)],
        compiler_params=_cparams("parallel"),
        name="moe_route",
    )(x, jnp.pad(router_w.T, ((0, LANES - N_EXPERTS), (0, 0))),
      jnp.pad(router_b, (0, LANES - N_EXPERTS)).reshape(LANES, 1))

    n_items = tm // rows + N_GROUPS - 1
    once = lambda shape: pl.BlockSpec(shape, lambda i, w, m: (0,) * len(shape), pipeline_mode=pl.Buffered(1))
    tok2 = lambda n: pl.BlockSpec((tm, n), lambda i, w, m: (i, 0))
    return pl.pallas_call(
        functools.partial(_moe_sparse_kernel, alpha=alpha, rows=rows),
        grid_spec=pltpu.PrefetchScalarGridSpec(
            num_scalar_prefetch=1,
            grid=(nt, n_items),
            in_specs=[tok2(d), tok2(LANES), pl.BlockSpec((1, MOE_ROW_FORM, tm), lambda i, w, m: (i, 0, 0)),
                      once(w_g.shape), once(w_u.shape), once(w_d.shape), once((1, d)), once((1, d))],
            out_specs=tok2(d),
            scratch_shapes=[pltpu.VMEM((tm, d), BF16), pltpu.VMEM((tm, d), F32), pltpu.VMEM((rows, d), BF16),
                            pltpu.VMEM((rows, LANES), F32), pltpu.VMEM((rows, d), F32),
                            pltpu.VMEM((tm, rows), BF16)]),
        out_shape=jax.ShapeDtypeStruct((t, d), F32),
        compiler_params=_cparams("parallel", "arbitrary"),
        name="moe_ln2",
    )(meta.reshape(nt, LANES), x, gp, posr, w_g, w_u, w_d, g.reshape(1, d), b.reshape(1, d))


def _ple_kernel(x_ref, p_ref, pw_ref, gw_ref, g_ref, b_ref, o_ref, *, alpha):
    x = x_ref[...]
    ple = _bdot(p_ref[...], pw_ref[...]) * jax.nn.sigmoid(_bdot(x, gw_ref[...]))
    o_ref[...] = _layer_norm(alpha * x + ple, g_ref[...], b_ref[...])


def _ple_call(x, p, ple_w, gate_w, g, b, alpha):
    t, d = x.shape
    tm = _token_tile(t)
    tok = lambda n: pl.BlockSpec((tm, n), lambda i: (i, 0))
    return pl.pallas_call(
        functools.partial(_ple_kernel, alpha=alpha),
        grid=(t // tm,),
        in_specs=[tok(d), tok(PLE_DIM), _resident(ple_w.shape), _resident(gate_w.shape), _resident((1, d)),
                  _resident((1, d))],
        out_specs=tok(d),
        out_shape=jax.ShapeDtypeStruct((t, d), F32),
        compiler_params=_cparams("parallel"),
        name="ple_ln3",
    )(x, p, ple_w, gate_w, g.reshape(1, d), b.reshape(1, d))


def _lb_kernel(x_ref, o_ref):
    x = x_ref[...]
    depth = x.shape[0]
    m = jnp.max(x, axis=0, keepdims=True)
    e = jnp.exp(x - m)
    pr = e / jnp.sum(e, axis=0, keepdims=True)
    run = jnp.zeros_like(pr[0:1])
    for i in range(depth):
        o_ref[i:i + 1, :] = run
        run = run + pr[i:i + 1]


def _lb_call(logits):
    return pl.pallas_call(
        _lb_kernel,
        out_shape=jax.ShapeDtypeStruct(logits.shape, F32),
        name="hgrn_lb",
    )(logits)


def _pad_rows(x, t):
    pad = t - x.shape[1]
    if pad == 0:
        return x
    return jnp.pad(x, ((0, 0), (0, pad)) + ((0, 0),) * (x.ndim - 2))


def _layer(x, p, st, lw, alpha):
    b, l, d = x.shape
    t = b * l
    hgrn_s, rwkv_s, rwkv_prev, lat_past, kr_past, fk_past, fv_past, flf_past = st
    past = 0 if lat_past is None else lat_past.shape[1]
    t_valid = past + l
    t_pad = -(-t_valid // LANES) * LANES if past else t_valid

    cols = _in_call(x.reshape(t, d), lw["w_in"], lw["fox_bf"])
    ca, cb, qlat, kvlat, kr, fq, fk, fv, lf_wide, lf_new = [c.reshape(b, l, -1) for c in cols]

    st0 = jnp.zeros((b, BRANCH_W, BRANCH_W), F32) if hgrn_s is None else _hgrn_state_to_blockdiag(hgrn_s)
    prev_row = jnp.zeros((b, B_COLS), F32) if rwkv_prev is None else rwkv_prev
    s0 = jnp.zeros((b, N_HEADS, HEAD_DIM, HEAD_DIM), F32) if rwkv_s is None else rwkv_s
    o_a, hgrn_bd, o_b, rwkv_new = _recurrent_call(
        ca, lw["lb"], lw["hgrn_norm_g"], st0, cb, prev_row, s0, lw["rwkv_mu"], lw["rwkv_w0"], lw["rwkv_w2"],
        lw["rwkv_a0"], lw["rwkv_a2"], lw["rwkv_g2"], lw["rwkv_kk"], lw["rwkv_ka"], lw["rwkv_rk"],
        lw["rwkv_lnx_g"], lw["rwkv_lnx_b"])
    hgrn_new = _hgrn_state_from_blockdiag(hgrn_bd)
    shift_new = cb[:, l - 1]

    cos, sin = _rope_tables(past, l)
    q_c, lat_new, kr_new = _mla_q_call(qlat, kvlat, kr, cos, sin, lw["mla_qn_g"], lw["mla_w_uq"], lw["mla_kvn_g"])
    if past:
        c_all = _pad_rows(jnp.concatenate([lat_past, lat_new], axis=1), t_pad)
        kr_all = _pad_rows(jnp.concatenate([kr_past, kr_new], axis=1), t_pad)
    else:
        c_all, kr_all = lat_new, kr_new
    k_c, v_c = _mla_kv_call(c_all, kr_all, lw["mla_w_k"], lw["mla_w_v"])
    o_c = _flash_call(q_c, k_c, v_c, past=past, t_valid=t_valid, chunk_mask=True, name="mla_attn")

    if past:
        flf_wide = jnp.pad(flf_past, ((0, 0), (0, 0), (0, LANES - N_HEADS)))
        lf_all = _pad_rows(jnp.concatenate([flf_wide, lf_wide], axis=1), t_pad)
        k_all = _pad_rows(jnp.concatenate([fk_past.reshape(b, past, BRANCH_W), fk], axis=1), t_pad)
        v_all = _pad_rows(jnp.concatenate([fv_past.reshape(b, past, BRANCH_W), fv], axis=1), t_pad)
    else:
        lf_all, k_all, v_all = lf_wide, fk, fv
    k_d, v_d, c_all_f = _fox_kv_call(k_all, v_all, lf_all)
    q_d = _fox_q_call(fq, c_all_f[:, past:past + l])
    o_d = _flash_call(q_d, k_d, v_d, past=past, t_valid=t_valid, chunk_mask=False, name="fox_attn")

    flat = lambda o: o.reshape(t, BRANCH_W)
    x1 = _mix_call(x.reshape(t, d), flat(o_a), flat(o_b), flat(o_c), flat(o_d), lw["w_mg"], lw["w_br"], lw["w_o"],
                   lw["ln1_g"], lw["ln1_b"], alpha)
    x2 = _moe_call(x1, lw["router_w"], lw["router_b"], lw["w_g"], lw["w_u"], lw["w_d"], lw["ln2_g"], lw["ln2_b"],
                   alpha)
    x3 = _ple_call(x2, p.reshape(t, PLE_DIM), lw["ple_w"], lw["ple_gate_w"], lw["ln3_g"], lw["ln3_b"], alpha)
    new = (hgrn_new, rwkv_new, shift_new, lat_new, kr_new, fk.reshape(b, l, N_HEADS, HEAD_DIM),
           fv.reshape(b, l, N_HEADS, HEAD_DIM), lf_new)
    return x3.reshape(b, l, d), new


def _relayout_w_uq(w_uq):
    w = w_uq.reshape(Q_LORA, N_HEADS, NOPE + ROPE)
    w = jnp.pad(w, ((0, 256 - Q_LORA), (0, 0), (0, LANES - NOPE - ROPE)))
    return w.reshape(256, N_HEADS * LANES).astype(BF16)


def _relayout_w_ukv(w_ukv):
    w = w_ukv.reshape(KV_LORA, N_HEADS, NOPE + HEAD_DIM)
    pad = lambda x: jnp.pad(x, ((0, 0), (0, 0), (0, LANES - x.shape[-1]))).reshape(KV_LORA, N_HEADS * LANES)
    return pad(w[..., :NOPE]).astype(BF16), pad(w[..., NOPE:]).astype(BF16)


def kernel(x_prompt, x_sample, state_hgrn, state_rwkv, state_rwkv_shift, cache_mla_latent, cache_mla_krope, cache_fox_k, cache_fox_v, cache_fox_logf, p_prompt, p_sample, ln_in_g, ln_in_b, w_in, hgrn_lb_logits, hgrn_norm_g, rwkv_mu, rwkv_w0, rwkv_w2, rwkv_a0, rwkv_a2, rwkv_g2, rwkv_kk, rwkv_ka, rwkv_rk, rwkv_lnx_g, rwkv_lnx_b, mla_qnorm_g, mla_w_uq, mla_kvnorm_g, mla_w_ukv, fox_bf, w_br, w_mg, w_o, ln1_g, ln1_b, router_w, router_b, exp_w_gate, exp_w_up, exp_w_down, ln2_g, ln2_b, ple_w, ple_gate_w, ln3_g, ln3_b):
    depth = w_in.shape[0]
    alpha = (2 * depth) ** DEPTH_ALPHA_POW
    d = x_prompt.shape[-1]
    lb_all = _lb_call(hgrn_lb_logits)

    def ln_in(x):
        return _ln_call(x.reshape(-1, d), ln_in_g, ln_in_b).reshape(x.shape)

    xp, xs = ln_in(x_prompt), ln_in(x_sample)
    new_p, new_s = [], []
    for i in range(depth):
        w_k, w_v = _relayout_w_ukv(mla_w_ukv[i])
        lw = dict(
            w_in=_relayout_w_in(w_in[i]), lb=lb_all[i], hgrn_norm_g=hgrn_norm_g[i], rwkv_mu=rwkv_mu[i],
            rwkv_w0=rwkv_w0[i], rwkv_w2=rwkv_w2[i], rwkv_a0=rwkv_a0[i], rwkv_a2=rwkv_a2[i], rwkv_g2=rwkv_g2[i],
            rwkv_kk=rwkv_kk[i], rwkv_ka=rwkv_ka[i], rwkv_rk=rwkv_rk[i], rwkv_lnx_g=rwkv_lnx_g[i],
            rwkv_lnx_b=rwkv_lnx_b[i],
            mla_qn_g=jnp.pad(mla_qnorm_g[i], (0, 256 - Q_LORA)).reshape(1, 256), mla_w_uq=_relayout_w_uq(mla_w_uq[i]),
            mla_kvn_g=mla_kvnorm_g[i].reshape(1, KV_LORA), mla_w_k=w_k, mla_w_v=w_v, fox_bf=fox_bf[i],
            w_br=w_br[i].astype(BF16), w_mg=w_mg[i].astype(BF16), w_o=w_o[i].astype(BF16),
            ln1_g=ln1_g[i], ln1_b=ln1_b[i], router_w=router_w, router_b=router_b,
            w_g=exp_w_gate[i].astype(BF16), w_u=exp_w_up[i].astype(BF16), w_d=exp_w_down[i].astype(BF16),
            ln2_g=ln2_g[i], ln2_b=ln2_b[i], ple_w=ple_w[i].astype(BF16), ple_gate_w=ple_gate_w[i].astype(BF16),
            ln3_g=ln3_g[i], ln3_b=ln3_b[i])
        xp, st_p = _layer(xp, p_prompt[i], (None,) * 8, lw, alpha)
        new_p.append(st_p)
        st_in = (state_hgrn[i], state_rwkv[i], state_rwkv_shift[i], cache_mla_latent[i], cache_mla_krope[i],
                 cache_fox_k[i], cache_fox_v[i], cache_fox_logf[i])
        xs, st_s = _layer(xs, p_sample[i], st_in, lw, alpha)
        new_s.append(st_s)
    stack = lambda sts, j: jnp.stack([s[j] for s in sts], axis=0)
    outs_p = tuple(stack(new_p, j) for j in range(8))
    outs_s = tuple(stack(new_s, j) for j in range(8))
    return (xp, xs) + outs_p + outs_s
```

```python
import functools
import math

import numpy as np
import jax
import jax.numpy as jnp
from jax import lax
from jax.experimental import pallas as pl
from jax.experimental.pallas import tpu as pltpu

F32 = jnp.float32
BF16 = jnp.bfloat16

D_MODEL = 1024
N_HEADS = 4
HEAD_DIM = 64
BRANCH_W = N_HEADS * HEAD_DIM
CHUNK = 64
W_LORA, A_LORA, G_LORA = 32, 32, 64
NOPE, ROPE, Q_LORA, KV_LORA = 64, 32, 192, 128
ROPE_THETA = 10000.0
N_EXPERTS, N_GROUPS, EXPERT_FF = 16, 4, 256
GROUP_SIZE = N_EXPERTS // N_GROUPS
PLE_DIM = 256
A_COLS = 4 * BRANCH_W
B_COLS = 3 * BRANCH_W + W_LORA + A_LORA + G_LORA
C_COLS = Q_LORA + KV_LORA + ROPE
D_COLS = 3 * BRANCH_W + N_HEADS
DEPTH_ALPHA_POW = 0.25
LN_EPS = 1e-5
RMS_EPS = 1e-6
RWKV_GN_EPS = 64e-5
MASK_VALUE = -1e30
LOG2E = math.log2(math.e)

LANES = 128
SUBLANES = 8
VMEM_LIMIT_BYTES = 56 * 1024 * 1024

NN = ((1,), (0,))
NT = ((1,), (1,))
TN = ((0,), (0,))


def _dg(a, b, dims=NN):
    return lax.dot_general(a, b, (dims, ((), ())), preferred_element_type=F32)


def _bdot(a, b, dims=NN):
    return _dg(a.astype(BF16), b.astype(BF16), dims)


def _split(x, pieces):
    out = []
    r = x
    for i in range(pieces):
        p = r.astype(BF16)
        out.append(p)
        if i + 1 < pieces:
            r = r - p.astype(F32)
    return out


def _dot3(a, b, dims=NN):
    ah, al = _split(a, 2)
    bh, bl = _split(b, 2)
    return _dg(ah, bh, dims) + (_dg(ah, bl, dims) + _dg(al, bh, dims))


def _dotc(a, c, dims=NN, pieces=3):
    ps = _split(a, pieces)
    acc = _dg(ps[0], c, dims)
    for p in ps[1:]:
        acc = acc + _dg(p, c, dims)
    return acc


def _tree_sum(xs):
    while len(xs) > 1:
        xs = [xs[i] + xs[i + 1] for i in range(0, len(xs) - 1, 2)] + ([xs[-1]] if len(xs) % 2 else [])
    return xs[0]


def _iota(shape, dim):
    return lax.broadcasted_iota(jnp.int32, shape, dim)


def _tri_incl(n, dtype=BF16):
    return (_iota((n, n), 0) >= _iota((n, n), 1)).astype(dtype)


def _head_indicator():
    r = _iota((BRANCH_W, BRANCH_W), 0) // HEAD_DIM
    c = _iota((BRANCH_W, BRANCH_W), 1) // HEAD_DIM
    return (r == c).astype(BF16)


def _layer_norm(x, g, b):
    mu = jnp.mean(x, axis=-1, keepdims=True)
    xc = x - mu
    var = jnp.mean(xc * xc, axis=-1, keepdims=True)
    return xc * lax.rsqrt(var + LN_EPS) * g + b


def _softplus(x):
    return jnp.maximum(x, 0.0) + jnp.log(1.0 + jnp.exp(-jnp.abs(x)))


def _silu(x):
    return x * jax.nn.sigmoid(x)


def _cparams(*sem):
    return pltpu.CompilerParams(dimension_semantics=sem, vmem_limit_bytes=VMEM_LIMIT_BYTES)


def _resident(shape):
    nd = len(shape)
    return pl.BlockSpec(shape, lambda *_: (0,) * nd)


TOKEN_TILE = 512
MOE_TOKEN_TILE = 1024
MOE_BLOCK = 256
HGRN_SUB_BLOCK = 16
ATTN_Q_TILE = 512
ATTN_KV_TILE = 512


def _token_tile(t, largest=TOKEN_TILE):
    tm = largest
    while tm >= SUBLANES:
        if t % tm == 0:
            return tm
        tm //= 2
    raise ValueError(f"token count {t} not a multiple of 8")


def _ln_kernel(x_ref, g_ref, b_ref, o_ref):
    o_ref[...] = _layer_norm(x_ref[...], g_ref[...], b_ref[...])


def _ln_call(x, g, b):
    t, d = x.shape
    tm = _token_tile(t)
    return pl.pallas_call(
        _ln_kernel,
        grid=(t // tm,),
        in_specs=[pl.BlockSpec((tm, d), lambda i: (i, 0)), _resident((1, d)), _resident((1, d))],
        out_specs=pl.BlockSpec((tm, d), lambda i: (i, 0)),
        out_shape=jax.ShapeDtypeStruct((t, d), F32),
        compiler_params=_cparams("parallel"),
        name="ln_in",
    )(x, g.reshape(1, d), b.reshape(1, d))


_IN_SLOTS = (("a", A_COLS, A_COLS), ("b", B_COLS, B_COLS), ("qlat", Q_LORA, 256), ("kvlat", KV_LORA, 128),
             ("kr", ROPE, 128), ("fq", BRANCH_W, BRANCH_W), ("fk", BRANCH_W, BRANCH_W), ("fv", BRANCH_W, BRANCH_W),
             ("ff", N_HEADS, 128))


def _split_w_in(w_in):
    parts, off = [], 0
    for _, width, slot in _IN_SLOTS:
        w = w_in[:, off:off + width].astype(BF16)
        if slot > width:
            w = jnp.pad(w, ((0, 0), (0, slot - width)))
        parts.append(w)
        off += width
    assert off == w_in.shape[1]
    return parts


def _in_kernel(x_ref, *refs):
    n = len(_IN_SLOTS)
    w_refs, bf_ref, o_refs = refs[:n], refs[n], refs[n + 1:]
    xb = x_ref[...].astype(BF16)
    for w_ref, o_ref, (name, _, _) in zip(w_refs, o_refs, _IN_SLOTS):
        cols = _dg(xb, w_ref[...])
        if name == "ff":
            z = cols + bf_ref[...]
            cols = jnp.where(_iota(z.shape, 1) < N_HEADS, -_softplus(-z), 0.0)
            o_refs[-1][...] = cols[:, :N_HEADS]
        o_ref[...] = cols


def _in_call(x, w_parts, fox_bf):
    t, d = x.shape
    tm = _token_tile(t)
    widths = [slot for _, _, slot in _IN_SLOTS] + [N_HEADS]
    return pl.pallas_call(
        _in_kernel,
        grid=(t // tm,),
        in_specs=[pl.BlockSpec((tm, d), lambda i: (i, 0))] + [_resident(w.shape) for w in w_parts]
                 + [_resident((1, LANES))],
        out_specs=[pl.BlockSpec((tm, width), lambda i: (i, 0)) for width in widths],
        out_shape=[jax.ShapeDtypeStruct((t, width), F32) for width in widths],
        compiler_params=_cparams("parallel"),
        name="in_proj",
    )(x, *w_parts, jnp.pad(fox_bf, (0, LANES - N_HEADS)).reshape(1, LANES))


def _hgrn_parts(c_ref, lb_ref, ng_ref, s0_ref, o_ref, s_ref, st_scr, *, chunk, group):
    def init():
        st_scr[...] = s0_ref[...]

    def finish():
        s_ref[...] = st_scr[...]

    return init, _hgrn_stages(c_ref, lb_ref, ng_ref, o_ref, st_scr, chunk, group), finish


def _hgrn_stages(c_ref, lb_ref, ng_ref, o_ref, st_scr, chunk, group):
    c = chunk
    w = BRANCH_W
    hd = HEAD_DIM
    nh = N_HEADS
    sb = min(HGRN_SUB_BLOCK, c)
    seqs = range(group)
    tri = _tri_incl(c)
    ind = _head_indicator()
    st_keep = (_iota((w, w), 0) // hd) == (_iota((w, w), 1) // hd)
    hr_keep = (_iota((nh * sb, w), 0) // sb) == (_iota((nh * sb, w), 1) // hd)
    row8 = _iota((SUBLANES, w), 0)
    lb = lb_ref[...]

    sq = []
    for bi in seqs:
        cols = c_ref[bi]
        q, fz, iv, g = cols[:, :w], cols[:, w:2 * w], cols[:, 2 * w:3 * w], cols[:, 3 * w:]
        f = lb + (1.0 - lb) * jax.nn.sigmoid(fz)
        a = _dotc(tri, jnp.log(f) * LOG2E, pieces=3)
        sq.append(dict(q=q, k=1.0 - f, iv=iv, g=g, a=a, a_last=a[c - 1:c, :], outs=[]))
    yield

    for bi, s in zip(seqs, sq):
        st = st_scr[bi]
        s["o"] = _bdot(s["q"] * jnp.exp2(s["a"]), st, NT)
        kd = s["k"] * jnp.exp2(s["a_last"] - s["a"])
        st_scr[bi] = st * jnp.exp2(s["a_last"]) + jnp.where(st_keep, _bdot(s["iv"], kd, TN), 0.0)
    yield

    for tb in range(c // sb):
        lo = tb * sb
        for s in sq:
            q, k, iv, a = s["q"], s["k"], s["iv"], s["a"]
            parts = []
            for rg in range(sb // SUBLANES):
                r0 = lo + rg * SUBLANES
                q8, a8 = q[r0:r0 + SUBLANES], a[r0:r0 + SUBLANES]
                pieces = []
                for src in range(lo, r0 + SUBLANES):
                    d = a8 - a[src:src + 1]
                    if src >= r0:
                        d = jnp.where(row8 >= src - r0, d, MASK_VALUE)
                    pieces.append(q8 * jnp.exp2(d) * k[src:src + 1])
                e = _dotc(jnp.concatenate(pieces, axis=0), ind, pieces=1)
                parts.append(_tree_sum([e[i * SUBLANES:(i + 1) * SUBLANES] * iv[lo + i:lo + i + 1]
                                        for i in range(r0 + SUBLANES - lo)]))
            o_tb = jnp.concatenate(parts, axis=0)
            if tb > 0:
                a_lo = a[lo:lo + 1]
                qh = q[lo:lo + sb] * jnp.exp2(a[lo:lo + sb] - a_lo)
                kh = k[:lo] * jnp.exp2(a_lo - a[:lo])
                qh_rows = jnp.where(hr_keep, jnp.concatenate([qh] * nh, axis=0), 0.0)
                sc = _bdot(qh_rows, kh, NT)
                ov = jnp.where(hr_keep, _bdot(sc, iv[:lo]), 0.0)
                for h in range(nh):
                    o_tb = o_tb + ov[h * sb:(h + 1) * sb]
            s["outs"].append(o_tb)
        yield

    for bi, s in zip(seqs, sq):
        o = s["o"] + jnp.concatenate(s["outs"], axis=0)
        ms = _dotc(o * o, ind, pieces=2) * (1.0 / hd)
        o = o * lax.rsqrt(ms + RMS_EPS) * ng_ref[...] * _silu(s["g"])
        o_ref[bi] = o.astype(o_ref.dtype)


def _hgrn_state_to_blockdiag(s):
    b = s.shape[0]
    out = jnp.zeros((b, N_HEADS, HEAD_DIM, N_HEADS, HEAD_DIM), F32)
    for h in range(N_HEADS):
        out = out.at[:, h, :, h, :].set(jnp.swapaxes(s[:, h], -1, -2))
    return out.reshape(b, BRANCH_W, BRANCH_W)


def _hgrn_state_from_blockdiag(st):
    hs = [st[:, h * HEAD_DIM:(h + 1) * HEAD_DIM, h * HEAD_DIM:(h + 1) * HEAD_DIM] for h in range(N_HEADS)]
    return jnp.swapaxes(jnp.stack(hs, axis=1), -1, -2)


def _rwkv_tdot(a, b, dims=NN):
    return _bdot(a, b, dims)


def _rwkv_sdot(a, b, dims=NN):
    return _dot3(a, b, dims)


def _rwkv_operands(cols, prev, mu_ref, w0_ref, lora_ref, a0_ref, kk_ref, ka_ref, rk_ref):
    shifted = jnp.where(_iota(cols.shape, 0) == 0, prev, pltpu.roll(cols, 1, 0))
    m = cols + (shifted - cols) * mu_ref[...]
    w = BRANCH_W
    r, k, v = m[:, :w], m[:, w:2 * w], m[:, 2 * w:3 * w]
    slab = m[:, 3 * w:]
    lane = _iota(slab.shape, 1)
    act = jnp.where(lane < W_LORA, jnp.tanh(slab),
                    jnp.where(lane < W_LORA + A_LORA, slab, jax.nn.sigmoid(slab)))
    lora = _bdot(act, lora_ref[...])
    w_log = -_softplus(-(w0_ref[...] + lora[:, :w])) - 0.5
    a_rate = jax.nn.sigmoid(a0_ref[...] + lora[:, w:2 * w])
    g = lora[:, 2 * w:]
    ind = _head_indicator()
    kk = k * kk_ref[...]
    norm = jnp.sqrt(_dotc(kk * kk, ind, pieces=3))
    kk = kk / jnp.maximum(norm, 1e-12)
    kh = k * (1.0 + (a_rate - 1.0) * ka_ref[...])
    return dict(r=r, lw=-jnp.exp(w_log), k=kh, v=v, a=-kk, b=kk * a_rate, g=g,
                bonus=_dotc(r * kh * rk_ref[...], ind, pieces=3) * v)


def _rwkv_parts(c_ref, prow_ref, mu_ref, w0_ref, lora_ref, a0_ref, kk_ref, ka_ref, rk_ref, lg_ref, lb_ref,
                s0_ref, o_ref, s_ref, st_scr, prev_scr, *, chunk, group):
    hd = HEAD_DIM
    blocks = [(bi, slice(h * hd, (h + 1) * hd)) for bi in range(group) for h in range(N_HEADS)]

    def init():
        st_scr[...] = jnp.zeros(st_scr.shape, F32)
        prev_scr[...] = prow_ref[...]
        for bi, sl in blocks:
            st_scr[bi, sl, sl] = s0_ref[bi, sl.start // hd]

    def finish():
        for bi, sl in blocks:
            s_ref[bi, sl.start // hd] = st_scr[bi, sl, sl]

    stages = _rwkv_stages(c_ref, mu_ref, w0_ref, lora_ref, a0_ref, kk_ref, ka_ref, rk_ref, lg_ref, lb_ref, o_ref,
                          st_scr, prev_scr, chunk, group)
    return init, stages, finish


def _rwkv_stages(c_ref, mu_ref, w0_ref, lora_ref, a0_ref, kk_ref, ka_ref, rk_ref, lg_ref, lb_ref, o_ref, st_scr,
                 prev_scr, chunk, group):
    c = chunk
    w = BRANCH_W
    hd = HEAD_DIM
    nh = N_HEADS
    n = nh * c
    seqs = range(group)
    tri = _tri_incl(c)
    ind = _head_indicator()
    t_w = _iota((c, n), 0)
    s_w = _iota((c, n), 1) % c
    strict_w = t_w > s_w
    incl_w = t_w >= s_w
    eye_w = (t_w == s_w).astype(F32)
    bd_keep = (_iota((n, n), 0) // c) == (_iota((n, n), 1) // c)
    hr_keep = (_iota((n, w), 0) // c) == (_iota((n, w), 1) // hd)
    st_keep = (_iota((w, w), 0) // hd) == (_iota((w, w), 1) // hd)

    def bd(x):
        return jnp.where(bd_keep, jnp.concatenate([x] * nh, axis=0), 0.0)

    def head_rows(x):
        return jnp.where(hr_keep, jnp.concatenate([x] * nh, axis=0), 0.0)

    sq = []
    for bi in seqs:
        cols = c_ref[bi]
        ops = _rwkv_operands(cols, prev_scr[bi], mu_ref, w0_ref, lora_ref, a0_ref, kk_ref, ka_ref, rk_ref)
        prev_scr[bi] = cols[c - 1:c]
        r, lw, k, v, a, b = (ops[name] for name in ("r", "lw", "k", "v", "a", "b"))
        cum = _dotc(tri, lw, pieces=3)
        last = cum[c - 1:c, :]
        e_neg = jnp.exp(-cum)
        e_last = jnp.exp(last - cum)
        sq.append(dict(v=v, last=last, g=ops["g"], bonus=ops["bonus"],
                       lhs=jnp.concatenate([a * jnp.exp(cum - lw), r * jnp.exp(cum)], axis=0),
                       bt=b * e_neg, kt=k * e_neg, hat=jnp.concatenate([b * e_last, k * e_last], axis=0)))
    yield

    for q in sq:
        pb = _rwkv_tdot(q["lhs"], head_rows(q["bt"]), NT)
        pk = _rwkv_tdot(q["lhs"], head_rows(q["kt"]), NT)
        q["l_ab"] = jnp.where(strict_w, pb[:c], 0.0)
        q["a_rb"] = jnp.where(incl_w, pb[c:], 0.0)
        q["lk"] = jnp.concatenate([jnp.where(strict_w, pk[:c], 0.0), jnp.where(incl_w, pk[c:], 0.0)], axis=0)
        q["tinv"] = eye_w + q["l_ab"]
        q["p"] = q["l_ab"]
    yield

    for _ in range(int(math.log2(c)) - 1):
        for q in sq:
            q["p"] = _rwkv_tdot(q["p"], bd(q["p"]))
        yield
        for q in sq:
            q["tinv"] = q["tinv"] + _rwkv_tdot(q["tinv"], bd(q["p"]))
        yield

    for bi, q in zip(seqs, sq):
        q["st"] = st_scr[bi]
        q["sh"] = _rwkv_sdot(q["lhs"], q["st"], NT)
        q["lv"] = _rwkv_tdot(q["lk"], head_rows(q["v"]))
    yield
    for q in sq:
        q["u"] = _rwkv_tdot(q["tinv"], head_rows(q["sh"][:c] + q["lv"][:c]))
    yield
    for bi, q in zip(seqs, sq):
        upd = _rwkv_sdot(jnp.concatenate([q["u"], q["v"]], axis=0), q["hat"], TN)
        st_scr[bi] = q["st"] * jnp.exp(q["last"]) + jnp.where(st_keep, upd, 0.0)
    yield
    for q in sq:
        q["y"] = q["sh"][c:] + _rwkv_tdot(q["a_rb"], head_rows(q["u"])) + q["lv"][c:]
    yield

    for bi, q in zip(seqs, sq):
        y = q["y"]
        mu = _dotc(y, ind, pieces=3) * (1.0 / hd)
        yc = y - mu
        var = _dotc(yc * yc, ind, pieces=2) * (1.0 / hd)
        yn = yc * lax.rsqrt(var + RWKV_GN_EPS) * lg_ref[...] + lb_ref[...]
        o_ref[bi] = ((yn + q["bonus"]) * q["g"]).astype(o_ref.dtype)


def _batch_group(b):
    return 4 if b % 4 == 0 else 1


N_HGRN_IN, N_RWKV_IN = 4, 12
RWKV_STAGES_PER_HGRN_STAGE = 1


def _recurrent_kernel(*refs, chunk, group):
    h_in, refs = refs[:N_HGRN_IN], refs[N_HGRN_IN:]
    r_in, refs = refs[:N_RWKV_IN], refs[N_RWKV_IN:]
    h_o, h_s, r_o, r_s, h_st, r_st, r_prev = refs
    h_init, h_stages, h_finish = _hgrn_parts(*h_in, h_o, h_s, h_st, chunk=chunk, group=group)
    r_init, r_stages, r_finish = _rwkv_parts(*r_in, r_o, r_s, r_st, r_prev, chunk=chunk, group=group)
    j = pl.program_id(1)

    @pl.when(j == 0)
    def _():
        h_init()
        r_init()

    live = [r_stages] * RWKV_STAGES_PER_HGRN_STAGE + [h_stages]
    while live:
        for stages in list(live):
            if stages in live and next(stages, "done") == "done":
                live = [s for s in live if s is not stages]

    @pl.when(j == pl.num_programs(1) - 1)
    def _():
        h_finish()
        r_finish()


def _recurrent_call(cols_a, lb, norm_g, hgrn_st0, cols_b, prev_row, rwkv_s0, mu, w0, w2, a0, a2, g2, kk_s, ka, rk,
                    lnx_g, lnx_b):
    b, l, nb = cols_b.shape
    w = BRANCH_W
    c = CHUNK if l % CHUNK == 0 else l
    bg = _batch_group(b)
    row = lambda x: x.reshape(1, -1)
    lora_w = jnp.zeros((LANES, 3 * w), F32)
    lora_w = lora_w.at[:W_LORA, :w].set(w2).at[W_LORA:W_LORA + A_LORA, w:2 * w].set(a2)
    lora_w = lora_w.at[W_LORA + A_LORA:, 2 * w:].set(g2).astype(BF16)
    tok = lambda n: pl.BlockSpec((bg, c, n), lambda i, j: (i, j, 0))
    hst_spec = pl.BlockSpec((bg, w, w), lambda i, j: (i, 0, 0))
    rst_spec = pl.BlockSpec((bg, N_HEADS, HEAD_DIM, HEAD_DIM), lambda i, j: (i, 0, 0, 0))
    hgrn_in = [tok(A_COLS), _resident((1, w)), _resident((1, w)), hst_spec]
    rwkv_in = ([tok(nb), pl.BlockSpec((bg, 1, nb), lambda i, j: (i, 0, 0)), _resident((1, nb)), _resident((1, w)),
                _resident((LANES, 3 * w))] + [_resident((1, w))] * 6 + [rst_spec])
    assert len(hgrn_in) == N_HGRN_IN and len(rwkv_in) == N_RWKV_IN
    return pl.pallas_call(
        functools.partial(_recurrent_kernel, chunk=c, group=bg),
        grid=(b // bg, l // c),
        in_specs=hgrn_in + rwkv_in,
        out_specs=[tok(w), hst_spec, tok(w), rst_spec],
        out_shape=[jax.ShapeDtypeStruct((b, l, w), BF16), jax.ShapeDtypeStruct((b, w, w), F32),
                   jax.ShapeDtypeStruct((b, l, w), BF16),
                   jax.ShapeDtypeStruct((b, N_HEADS, HEAD_DIM, HEAD_DIM), F32)],
        scratch_shapes=[pltpu.VMEM((bg, w, w), F32), pltpu.VMEM((bg, w, w), F32), pltpu.VMEM((bg, 1, nb), F32)],
        compiler_params=_cparams("parallel", "arbitrary"),
        name="hgrn_rwkv",
    )(cols_a, row(lb), row(norm_g), hgrn_st0,
      cols_b, prev_row.reshape(b, 1, nb), row(mu), row(w0), lora_w, row(a0), row(kk_s), row(ka), row(rk),
      row(lnx_g), row(lnx_b), rwkv_s0)


def _rope_tables(p, l):
    half = ROPE // 2
    inv = 1.0 / (ROPE_THETA ** (jnp.arange(half, dtype=F32) / half))
    ang = (p + jnp.arange(l, dtype=jnp.int32)).astype(F32)[:, None] * inv[None, :]
    cos, sin = jnp.cos(ang), jnp.sin(ang)
    pad = jnp.zeros((l, LANES - ROPE), F32)
    return jnp.concatenate([cos, cos, pad], axis=1), jnp.concatenate([-sin, sin, pad], axis=1)


def _swap_halves(x, base):
    half = ROPE // 2
    n = x.shape[-1]
    lane = _iota(x.shape, x.ndim - 1) % LANES
    up = pltpu.roll(x, n - half, x.ndim - 1)
    down = pltpu.roll(x, half, x.ndim - 1)
    return jnp.where((lane >= base) & (lane < base + half), up,
                     jnp.where((lane >= base + half) & (lane < base + ROPE), down, 0.0))


def _rms_norm(x, g, width):
    ms = jnp.sum(x * x, axis=-1, keepdims=True) * (1.0 / width)
    return x * lax.rsqrt(ms + RMS_EPS) * g


def _mla_q_kernel(ql_ref, kvl_ref, kr_ref, cos_ref, sin_ref, qg_ref, wuq_ref, kvg_ref, q_ref, c_ref, krn_ref):
    scale = (NOPE + ROPE) ** -0.5 * LOG2E
    qn = _rms_norm(ql_ref[0], qg_ref[...], Q_LORA)
    q = _bdot(qn, wuq_ref[...])
    cos, sin = cos_ref[...], sin_ref[...]
    lane = _iota(cos.shape, 1)
    cos_q = jnp.where(lane < NOPE, 1.0, pltpu.roll(cos, NOPE, 1))
    sin_q = pltpu.roll(sin, NOPE, 1)
    for h in range(N_HEADS):
        qh = q[:, h * LANES:(h + 1) * LANES]
        qh = qh * cos_q + _swap_halves(qh, NOPE) * sin_q
        q_ref[0, h] = (qh * scale).astype(q_ref.dtype)
    c_ref[0] = _rms_norm(kvl_ref[0], kvg_ref[...], KV_LORA)
    kr = kr_ref[0]
    krn = kr * cos + _swap_halves(kr, 0) * sin
    krn_ref[0] = krn[:, :ROPE]


def _mla_q_call(qlat, kvlat, kr, cos, sin, qn_g, w_uq_p, kvn_g):
    b, l, _ = qlat.shape
    tm = _token_tile(l)
    tok = lambda n: pl.BlockSpec((1, tm, n), lambda i, j: (i, j, 0))
    tab = pl.BlockSpec((tm, LANES), lambda i, j: (j, 0))
    return pl.pallas_call(
        _mla_q_kernel,
        grid=(b, l // tm),
        in_specs=[tok(256), tok(KV_LORA), tok(LANES), tab, tab, _resident((1, 256)),
                  _resident((256, N_HEADS * LANES)), _resident((1, KV_LORA))],
        out_specs=[pl.BlockSpec((1, N_HEADS, tm, LANES), lambda i, j: (i, 0, j, 0)), tok(KV_LORA), tok(ROPE)],
        out_shape=[jax.ShapeDtypeStruct((b, N_HEADS, l, LANES), BF16), jax.ShapeDtypeStruct((b, l, KV_LORA), F32),
                   jax.ShapeDtypeStruct((b, l, ROPE), F32)],
        compiler_params=_cparams("parallel", "parallel"),
        name="mla_q",
    )(qlat, kvlat, kr, cos, sin, qn_g, w_uq_p, kvn_g)


def _mla_kv_kernel(c_ref, kr_ref, wk_ref, wv_ref, k_ref, v_ref):
    cb = c_ref[0].astype(BF16)
    kall = _dg(cb, wk_ref[...])
    vall = _dg(cb, wv_ref[...])
    tm = cb.shape[0]
    place = (_iota((ROPE, LANES), 0) + NOPE == _iota((ROPE, LANES), 1)).astype(BF16)
    kr_slot = _dg(kr_ref[0].astype(BF16), place)
    one_slot = (_iota((tm, LANES), 1) == HEAD_DIM).astype(F32)
    for h in range(N_HEADS):
        k_ref[0, h] = (kall[:, h * LANES:(h + 1) * LANES] + kr_slot).astype(k_ref.dtype)
        v_ref[0, h, 0] = (vall[:, h * LANES:(h + 1) * LANES] + one_slot).T.astype(v_ref.dtype)


def _kv_specs(b, t, tm):
    kspec = pl.BlockSpec((1, N_HEADS, tm, LANES), lambda i, j: (i, 0, j, 0))
    vspec = pl.BlockSpec((1, N_HEADS, 1, LANES, tm), lambda i, j: (i, 0, j, 0, 0))
    shapes = [jax.ShapeDtypeStruct((b, N_HEADS, t, LANES), BF16),
              jax.ShapeDtypeStruct((b, N_HEADS, t // tm, LANES, tm), BF16)]
    return [kspec, vspec], shapes


def _mla_kv_call(c_all, kr_all, w_k_p, w_v_p):
    b, t, _ = c_all.shape
    tm = _token_tile(t, ATTN_KV_TILE)
    tok = lambda n: pl.BlockSpec((1, tm, n), lambda i, j: (i, j, 0))
    out_specs, out_shape = _kv_specs(b, t, tm)
    return pl.pallas_call(
        _mla_kv_kernel,
        grid=(b, t // tm),
        in_specs=[tok(KV_LORA), tok(ROPE), _resident((KV_LORA, N_HEADS * LANES)),
                  _resident((KV_LORA, N_HEADS * LANES))],
        out_specs=out_specs,
        out_shape=out_shape,
        compiler_params=_cparams("parallel", "parallel"),
        name="mla_kv",
    )(c_all, kr_all, w_k_p, w_v_p)


def _head_slot(x, h):
    pair = x[:, (h // 2) * LANES:(h // 2 + 1) * LANES]
    return pair if h % 2 == 0 else pltpu.roll(pair, HEAD_DIM, 1)


FOX_KEY_BIAS = HEAD_DIM
FOX_QUERY_BIAS = HEAD_DIM + 3 * N_HEADS


def _bias_pieces(c, base):
    p0 = c.astype(BF16).astype(F32)
    r1 = c - p0
    p1 = r1.astype(BF16).astype(F32)
    p2 = r1 - p1
    return (pltpu.roll(p0, base, 1) + pltpu.roll(p1, base + N_HEADS, 1)) + pltpu.roll(p2, base + 2 * N_HEADS, 1)


def _bias_ones(shape, base, h):
    lane = _iota(shape, 1)
    hit = (lane == base + h) | (lane == base + N_HEADS + h) | (lane == base + 2 * N_HEADS + h)
    return hit.astype(F32)


def _fox_q_kernel(q_ref, c_ref, o_ref):
    scale = HEAD_DIM ** -0.5 * LOG2E
    q = q_ref[0]
    pieces = _bias_pieces(c_ref[0] * LOG2E, FOX_QUERY_BIAS)
    lane = _iota(pieces.shape, 1)
    for h in range(N_HEADS):
        bias = pieces + _bias_ones(pieces.shape, FOX_KEY_BIAS, h)
        o_ref[0, h] = jnp.where(lane < HEAD_DIM, _head_slot(q, h) * scale, bias).astype(o_ref.dtype)


def _fox_kv_kernel(k_ref, v_ref, lf_ref, ko_ref, vo_ref, c_ref, carry):
    @pl.when(pl.program_id(1) == 0)
    def _():
        carry[...] = jnp.zeros_like(carry)

    k, v, lf = k_ref[0], v_ref[0], lf_ref[0]
    tm = k.shape[0]
    c = _dotc(_tri_incl(tm), lf, pieces=3) + carry[...]
    c_ref[0] = c
    carry[...] = c[tm - 1:, :]
    pieces = _bias_pieces(-c * LOG2E, FOX_KEY_BIAS)
    lane = _iota((tm, LANES), 1)
    for h in range(N_HEADS):
        bias = pieces + _bias_ones(pieces.shape, FOX_QUERY_BIAS, h)
        ko_ref[0, h] = jnp.where(lane < HEAD_DIM, _head_slot(k, h), bias).astype(ko_ref.dtype)
        vo_ref[0, h, 0] = jnp.where(lane < HEAD_DIM, _head_slot(v, h),
                                    (lane == HEAD_DIM).astype(F32)).T.astype(vo_ref.dtype)


def _fox_q_call(q, c_new):
    b, l, w = q.shape
    tm = _token_tile(l)
    return pl.pallas_call(
        _fox_q_kernel,
        grid=(b, l // tm),
        in_specs=[pl.BlockSpec((1, tm, w), lambda i, j: (i, j, 0)),
                  pl.BlockSpec((1, tm, LANES), lambda i, j: (i, j, 0))],
        out_specs=pl.BlockSpec((1, N_HEADS, tm, LANES), lambda i, j: (i, 0, j, 0)),
        out_shape=jax.ShapeDtypeStruct((b, N_HEADS, l, LANES), BF16),
        compiler_params=_cparams("parallel", "parallel"),
        name="fox_q",
    )(q, c_new)


def _fox_kv_call(k_all, v_all, lf_all):
    b, t, w = k_all.shape
    tm = _token_tile(t, ATTN_KV_TILE)
    tok = pl.BlockSpec((1, tm, w), lambda i, j: (i, j, 0))
    wide = pl.BlockSpec((1, tm, LANES), lambda i, j: (i, j, 0))
    out_specs, out_shape = _kv_specs(b, t, tm)
    return pl.pallas_call(
        _fox_kv_kernel,
        grid=(b, t // tm),
        in_specs=[tok, tok, wide],
        out_specs=out_specs + [wide],
        out_shape=out_shape + [jax.ShapeDtypeStruct((b, t, LANES), F32)],
        scratch_shapes=[pltpu.VMEM((1, LANES), F32)],
        compiler_params=_cparams("parallel", "arbitrary"),
        name="fox_kv",
    )(k_all, v_all, lf_all)


def _flash_kernel(q_ref, k_ref, v_ref, o_ref, m_scr, acc_scr, s_scr, *, tq, tk, past, t_valid, chunk_mask):
    qi = pl.program_id(1)
    q_start = past + qi * tq
    if chunk_mask:
        vis_end = jnp.minimum(((q_start + tq + CHUNK - 1) // CHUNK) * CHUNK, t_valid)
    else:
        vis_end = q_start + tq
    n_blocks = (vis_end + tk - 1) // tk
    n_full = q_start // tk

    m_scr[...] = jnp.full(m_scr.shape, MASK_VALUE, F32)
    acc_scr[...] = jnp.zeros(acc_scr.shape, F32)

    def logits(h, kb):
        ks = pl.multiple_of(kb * tk, tk)
        return _dg(k_ref[0, h, pl.ds(ks, tk), :], q_ref[0, h], NT)

    s_scr[...] = logits(0, 0)

    def block(kb, masked, s_next):
        if masked:
            kpos = kb * tk + _iota((tk, tq), 0)
            qpos = q_start + _iota((tk, tq), 1)
            if chunk_mask:
                keep = ((kpos // CHUNK) <= (qpos // CHUNK)) & (kpos < t_valid)
            else:
                keep = kpos <= qpos
        for h in range(N_HEADS):
            s = jnp.where(keep, s_next, MASK_VALUE) if masked else s_next
            if h + 1 < N_HEADS:
                s_next = logits(h + 1, kb)
            else:
                s_next = logits(0, jnp.minimum(kb + 1, n_blocks - 1))
            m_old = m_scr[h]
            m_new = jnp.maximum(m_old, jnp.max(s, axis=0, keepdims=True))
            p = jnp.exp2(s - m_new[:1])
            pv = _dg(v_ref[0, h, kb], p.astype(BF16))
            acc_scr[h] = acc_scr[h] * jnp.exp2(m_old - m_new)[:1] + pv
            m_scr[h] = m_new
        return s_next

    def full_body(kb, carry):
        s_scr[...] = block(kb, False, s_scr[...])
        return carry

    def masked_body(kb, carry):
        s_scr[...] = block(kb, True, s_scr[...])
        return carry

    lax.fori_loop(0, n_full, full_body, 0)
    lax.fori_loop(n_full, n_blocks, masked_body, 0)
    for h in range(N_HEADS):
        acc = acc_scr[h].T
        o = acc[:, :HEAD_DIM] / acc[:, HEAD_DIM:HEAD_DIM + 1]
        o_ref[0, :, h * HEAD_DIM:(h + 1) * HEAD_DIM] = o.astype(o_ref.dtype)


def _flash_call(q, k, v, *, past, t_valid, chunk_mask, name):
    b, nh, l, _ = q.shape
    t = k.shape[2]
    tk = v.shape[-1]
    lq = max(l, LANES)
    if lq != l:
        q = jnp.pad(q, ((0, 0), (0, 0), (0, lq - l), (0, 0)))
    tq = _token_tile(lq, ATTN_Q_TILE)
    out = pl.pallas_call(
        functools.partial(_flash_kernel, tq=tq, tk=tk, past=past, t_valid=t_valid, chunk_mask=chunk_mask),
        grid=(b, lq // tq),
        in_specs=[pl.BlockSpec((1, nh, tq, LANES), lambda i, j: (i, 0, j, 0)),
                  pl.BlockSpec((1, nh, t, LANES), lambda i, j: (i, 0, 0, 0)),
                  pl.BlockSpec((1, nh, t // tk, LANES, tk), lambda i, j: (i, 0, 0, 0, 0))],
        out_specs=pl.BlockSpec((1, tq, BRANCH_W), lambda i, j: (i, j, 0)),
        out_shape=jax.ShapeDtypeStruct((b, lq, BRANCH_W), BF16),
        scratch_shapes=[pltpu.VMEM((nh, SUBLANES, tq), F32), pltpu.VMEM((nh, LANES, tq), F32),
                        pltpu.VMEM((tk, tq), F32)],
        compiler_params=_cparams("parallel", "arbitrary"),
        name=name,
    )(q, k, v)
    return out[:, :l]


def _mix_kernel(x_ref, oa_ref, ob_ref, oc_ref, od_ref, wmg_ref, wbr_ref, wo_ref, g_ref, b_ref, o_ref, *, alpha):
    x = x_ref[...]
    xb = x.astype(BF16)
    d = D_MODEL
    acc = None
    for n, br_ref in enumerate((oa_ref, ob_ref, oc_ref, od_ref)):
        gate = jax.nn.sigmoid(_dg(xb, wmg_ref[:, n * d:(n + 1) * d]))
        term = gate * _dg(br_ref[...], wbr_ref[n])
        acc = term if acc is None else acc + term
    mix = _dg(acc.astype(BF16), wo_ref[...])
    o_ref[...] = _layer_norm(alpha * x + mix, g_ref[...], b_ref[...])


def _mix_call(x, oa, ob, oc, od, w_mg, w_br, w_o, g, b, alpha):
    t, d = x.shape
    tm = _token_tile(t)
    tok = lambda n: pl.BlockSpec((tm, n), lambda i: (i, 0))
    return pl.pallas_call(
        functools.partial(_mix_kernel, alpha=alpha),
        grid=(t // tm,),
        in_specs=[tok(d)] + [tok(BRANCH_W)] * 4 + [_resident(w_mg.shape), _resident(w_br.shape),
                                                   _resident(w_o.shape), _resident((1, d)), _resident((1, d))],
        out_specs=tok(d),
        out_shape=jax.ShapeDtypeStruct((t, d), F32),
        compiler_params=_cparams("parallel"),
        name="mix_ln1",
    )(x, oa, ob, oc, od, w_mg, w_br, w_o, g.reshape(1, d), b.reshape(1, d))


def _route(x, rw_ref, rb_ref):
    scores = jax.nn.sigmoid(_dot3(rw_ref[...], x, NT)[:N_EXPERTS])
    biased = scores + rb_ref[...][:N_EXPERTS]
    col = [biased[e:e + 1] for e in range(N_EXPERTS)]
    gs = []
    for g in range(N_GROUPS):
        v = col[g * GROUP_SIZE:(g + 1) * GROUP_SIZE]
        best = None
        for i in range(GROUP_SIZE):
            for j in range(i + 1, GROUP_SIZE):
                s = v[i] + v[j]
                best = s if best is None else jnp.maximum(best, s)
        gs.append(best)
    gates, sels = [], []
    for g in range(N_GROUPS):
        sel = None
        for o in range(N_GROUPS):
            if o == g:
                continue
            cond = (gs[g] > gs[o]) if o < g else (gs[g] >= gs[o])
            sel = cond if sel is None else sel & cond
        sels.append(sel)
        v = col[g * GROUP_SIZE:(g + 1) * GROUP_SIZE]
        for i in range(GROUP_SIZE):
            rank = None
            for j in range(GROUP_SIZE):
                if j == i:
                    continue
                ahead = (v[j] >= v[i]) if j < i else (v[j] > v[i])
                ahead = ahead.astype(F32)
                rank = ahead if rank is None else rank + ahead
            e = g * GROUP_SIZE + i
            gates.append(jnp.where(sel & (rank < 2.0), scores[e:e + 1], 0.0))
    total = gates[0]
    for gt in gates[1:]:
        total = total + gt
    return [gt / total for gt in gates], sels


MOE_POS_LANE = N_EXPERTS
MOE_ROW_FORM = 24
MOE_META_GROUP = 32
MOE_META_COUNT = 64


def _moe_route_kernel(x_ref, rw_ref, rb_ref, gp_ref, posr_ref, meta_ref, *, rows):
    tm = x_ref.shape[0]
    gates, sels = _route(x_ref[...], rw_ref, rb_ref)
    gsel = jnp.concatenate([s.astype(F32) for s in sels] + [jnp.zeros((SUBLANES - N_GROUPS, tm), F32)], axis=0)
    before = (_iota((tm, tm), 0) < _iota((tm, tm), 1)).astype(BF16)
    rank = _dg(gsel.astype(BF16), before)
    count = jnp.sum(gsel, axis=1, keepdims=True)
    offs = [jnp.zeros((1, 1), F32)]
    for g in range(1, N_GROUPS):
        offs.append(offs[-1] + count[g - 1:g])
    pos = gsel[0:1] * rank[0:1]
    for g in range(1, N_GROUPS):
        pos = pos + gsel[g:g + 1] * (offs[g] + rank[g:g + 1])
    gt = jnp.concatenate(gates + [pos, jnp.zeros((LANES - N_EXPERTS - 1, tm), F32)], axis=0)
    posr_ref[0] = gt[:MOE_ROW_FORM]
    gp_ref[...] = gt.T

    lane = _iota((1, LANES), 1)
    lane_g = lane - MOE_META_GROUP
    shift = int(math.log2(rows))
    meta = jnp.zeros((1, LANES), jnp.int32)
    start = jnp.zeros((1, 1), jnp.int32)
    for g in range(N_GROUPS):
        n_g = count[g:g + 1].astype(jnp.int32)
        off_g = offs[g].astype(jnp.int32)
        first = lax.shift_right_arithmetic(off_g, shift)
        last = lax.shift_right_arithmetic(off_g + n_g - 1, shift)
        k_g = jnp.where(n_g > 0, last - first + 1, 0)
        meta = jnp.where((lane >= start) & (lane < start + k_g), first + (lane - start), meta)
        meta = jnp.where((lane_g >= start) & (lane_g < start + k_g), g, meta)
        start = start + k_g
    meta_ref[0] = jnp.where(lane == MOE_META_COUNT, start, meta)


def _moe_sparse_kernel(meta_ref, x_ref, gp_ref, posr_ref, wg_ref, wu_ref, wd_ref, g_ref, b_ref, o_ref,
                       xb_scr, acc_scr, xs_scr, gs_scr, y_scr, back_scr, *, alpha, rows):
    i = pl.program_id(0)
    w = pl.program_id(1)
    tm = x_ref.shape[0]
    blk = meta_ref[i, w]
    grp = meta_ref[i, MOE_META_GROUP + w]
    n_items = meta_ref[i, MOE_META_COUNT]
    valid = w < n_items
    first_of_blk = valid & ((w == 0) | (meta_ref[i, jnp.maximum(w - 1, 0)] != blk))
    last_of_blk = valid & ((w + 1 >= n_items) | (meta_ref[i, w + 1] != blk))
    base = (blk * rows).astype(F32)

    @pl.when(w == 0)
    def _():
        xb_scr[...] = x_ref[...].astype(BF16)
        acc_scr[...] = jnp.zeros(acc_scr.shape, F32)

    @pl.when(first_of_blk)
    def _():
        rowform = posr_ref[0]
        pos_row = rowform[MOE_POS_LANE:MOE_POS_LANE + 1]
        perm = (pos_row == base + _iota((rows, tm), 0).astype(F32)).astype(BF16)
        back = (gp_ref[:, MOE_POS_LANE:MOE_POS_LANE + 1]
                == base + _iota((tm, rows), 1).astype(F32)).astype(BF16)
        back_scr[...] = back
        xs_scr[...] = _dg(perm, xb_scr[...]).astype(BF16)
        gst = _dotc(rowform[:N_EXPERTS], back, pieces=3)
        gst = jnp.concatenate([gst, jnp.zeros((LANES - N_EXPERTS, rows), F32)], axis=0)
        gs_scr[...] = gst.T
        y_scr[...] = jnp.zeros(y_scr.shape, F32)

    @pl.when(valid)
    def _():
        xs = xs_scr[...]
        gs = gs_scr[...]
        lane = _iota(gs.shape, 1)
        y = y_scr[...]
        for e in range(GROUP_SIZE):
            idx = grp * GROUP_SIZE + e
            gate = jnp.sum(jnp.where(lane == idx, gs, 0.0), axis=-1, keepdims=True)
            h = _silu(_dg(xs, wg_ref[idx])) * _dg(xs, wu_ref[idx]) * gate
            y = y + _dg(h.astype(BF16), wd_ref[idx])
        y_scr[...] = y

    @pl.when(last_of_blk)
    def _():
        acc_scr[...] += _dg(back_scr[...], y_scr[...].astype(BF16))

    @pl.when(w == pl.num_programs(1) - 1)
    def _():
        o_ref[...] = _layer_norm(alpha * x_ref[...] + acc_scr[...], g_ref[...], b_ref[...])


def _moe_call(x, router_w, router_b, w_g, w_u, w_d, g, b, alpha):
    t, d = x.shape
    tm = _token_tile(t, MOE_TOKEN_TILE)
    nt = t // tm
    rows = min(MOE_BLOCK, tm)
    tok = pl.BlockSpec((tm, d), lambda i: (i, 0))
    gp, posr, meta = pl.pallas_call(
        functools.partial(_moe_route_kernel, rows=rows),
        grid=(nt,),
        in_specs=[tok, _resident((LANES, d)), _resident((LANES, 1))],
        out_specs=[pl.BlockSpec((tm, LANES), lambda i: (i, 0)), pl.BlockSpec((1, MOE_ROW_FORM, tm), lambda i: (i, 0, 0)),
                   pl.BlockSpec((1, 1, LANES), lambda i: (i, 0, 0))],
        out_shape=[jax.ShapeDtypeStruct((t, LANES), F32), jax.ShapeDtypeStruct((nt, MOE_ROW_FORM, tm), F32),
                   jax.ShapeDtypeStruct((nt, 1, LANES), jnp.int32)],
        compiler_params=_cparams("parallel"),
        name="moe_route",
    )(x, jnp.pad(router_w.T, ((0, LANES - N_EXPERTS), (0, 0))),
      jnp.pad(router_b, (0, LANES - N_EXPERTS)).reshape(LANES, 1))

    n_items = tm // rows + N_GROUPS - 1
    once = lambda shape: pl.BlockSpec(shape, lambda i, w, m: (0,) * len(shape), pipeline_mode=pl.Buffered(1))
    tok2 = lambda n: pl.BlockSpec((tm, n), lambda i, w, m: (i, 0))
    return pl.pallas_call(
        functools.partial(_moe_sparse_kernel, alpha=alpha, rows=rows),
        grid_spec=pltpu.PrefetchScalarGridSpec(
            num_scalar_prefetch=1,
            grid=(nt, n_items),
            in_specs=[tok2(d), tok2(LANES), pl.BlockSpec((1, MOE_ROW_FORM, tm), lambda i, w, m: (i, 0, 0)),
                      once(w_g.shape), once(w_u.shape), once(w_d.shape), once((1, d)), once((1, d))],
            out_specs=tok2(d),
            scratch_shapes=[pltpu.VMEM((tm, d), BF16), pltpu.VMEM((tm, d), F32), pltpu.VMEM((rows, d), BF16),
                            pltpu.VMEM((rows, LANES), F32), pltpu.VMEM((rows, d), F32),
                            pltpu.VMEM((tm, rows), BF16)]),
        out_shape=jax.ShapeDtypeStruct((t, d), F32),
        compiler_params=_cparams("parallel", "arbitrary"),
        name="moe_ln2",
    )(meta.reshape(nt, LANES), x, gp, posr, w_g, w_u, w_d, g.reshape(1, d), b.reshape(1, d))


def _ple_kernel(x_ref, p_ref, pw_ref, gw_ref, g_ref, b_ref, o_ref, *, alpha):
    x = x_ref[...]
    ple = _bdot(p_ref[...], pw_ref[...]) * jax.nn.sigmoid(_bdot(x, gw_ref[...]))
    o_ref[...] = _layer_norm(alpha * x + ple, g_ref[...], b_ref[...])


def _ple_call(x, p, ple_w, gate_w, g, b, alpha):
    t, d = x.shape
    tm = _token_tile(t)
    tok = lambda n: pl.BlockSpec((tm, n), lambda i: (i, 0))
    return pl.pallas_call(
        functools.partial(_ple_kernel, alpha=alpha),
        grid=(t // tm,),
        in_specs=[tok(d), tok(PLE_DIM), _resident(ple_w.shape), _resident(gate_w.shape), _resident((1, d)),
                  _resident((1, d))],
        out_specs=tok(d),
        out_shape=jax.ShapeDtypeStruct((t, d), F32),
        compiler_params=_cparams("parallel"),
        name="ple_ln3",
    )(x, p, ple_w, gate_w, g.reshape(1, d), b.reshape(1, d))


def _lb_kernel(x_ref, o_ref):
    x = x_ref[...]
    depth = x.shape[0]
    m = jnp.max(x, axis=0, keepdims=True)
    e = jnp.exp(x - m)
    pr = e / jnp.sum(e, axis=0, keepdims=True)
    run = jnp.zeros_like(pr[0:1])
    for i in range(depth):
        o_ref[i:i + 1, :] = run
        run = run + pr[i:i + 1]


def _lb_call(logits):
    return pl.pallas_call(
        _lb_kernel,
        out_shape=jax.ShapeDtypeStruct(logits.shape, F32),
        name="hgrn_lb",
    )(logits)


def _pad_rows(x, t):
    pad = t - x.shape[1]
    if pad == 0:
        return x
    return jnp.pad(x, ((0, 0), (0, pad)) + ((0, 0),) * (x.ndim - 2))


def _layer(x, p, st, lw, alpha):
    b, l, d = x.shape
    t = b * l
    hgrn_s, rwkv_s, rwkv_prev, lat_past, kr_past, fk_past, fv_past, flf_past = st
    past = 0 if lat_past is None else lat_past.shape[1]
    t_valid = past + l
    t_pad = -(-t_valid // LANES) * LANES if past else t_valid

    cols = _in_call(x.reshape(t, d), lw["w_in"], lw["fox_bf"])
    ca, cb, qlat, kvlat, kr, fq, fk, fv, lf_wide, lf_new = [c.reshape(b, l, -1) for c in cols]

    st0 = jnp.zeros((b, BRANCH_W, BRANCH_W), F32) if hgrn_s is None else _hgrn_state_to_blockdiag(hgrn_s)
    prev_row = jnp.zeros((b, B_COLS), F32) if rwkv_prev is None else rwkv_prev
    s0 = jnp.zeros((b, N_HEADS, HEAD_DIM, HEAD_DIM), F32) if rwkv_s is None else rwkv_s
    o_a, hgrn_bd, o_b, rwkv_new = _recurrent_call(
        ca, lw["lb"], lw["hgrn_norm_g"], st0, cb, prev_row, s0, lw["rwkv_mu"], lw["rwkv_w0"], lw["rwkv_w2"],
        lw["rwkv_a0"], lw["rwkv_a2"], lw["rwkv_g2"], lw["rwkv_kk"], lw["rwkv_ka"], lw["rwkv_rk"],
        lw["rwkv_lnx_g"], lw["rwkv_lnx_b"])
    hgrn_new = _hgrn_state_from_blockdiag(hgrn_bd)
    shift_new = cb[:, l - 1]

    cos, sin = _rope_tables(past, l)
    q_c, lat_new, kr_new = _mla_q_call(qlat, kvlat, kr, cos, sin, lw["mla_qn_g"], lw["mla_w_uq"], lw["mla_kvn_g"])
    if past:
        c_all = _pad_rows(jnp.concatenate([lat_past, lat_new], axis=1), t_pad)
        kr_all = _pad_rows(jnp.concatenate([kr_past, kr_new], axis=1), t_pad)
    else:
        c_all, kr_all = lat_new, kr_new
    k_c, v_c = _mla_kv_call(c_all, kr_all, lw["mla_w_k"], lw["mla_w_v"])
    o_c = _flash_call(q_c, k_c, v_c, past=past, t_valid=t_valid, chunk_mask=True, name="mla_attn")

    if past:
        flf_wide = jnp.pad(flf_past, ((0, 0), (0, 0), (0, LANES - N_HEADS)))
        lf_all = _pad_rows(jnp.concatenate([flf_wide, lf_wide], axis=1), t_pad)
        k_all = _pad_rows(jnp.concatenate([fk_past.reshape(b, past, BRANCH_W), fk], axis=1), t_pad)
        v_all = _pad_rows(jnp.concatenate([fv_past.reshape(b, past, BRANCH_W), fv], axis=1), t_pad)
    else:
        lf_all, k_all, v_all = lf_wide, fk, fv
    k_d, v_d, c_all_f = _fox_kv_call(k_all, v_all, lf_all)
    q_d = _fox_q_call(fq, c_all_f[:, past:past + l])
    o_d = _flash_call(q_d, k_d, v_d, past=past, t_valid=t_valid, chunk_mask=False, name="fox_attn")

    flat = lambda o: o.reshape(t, BRANCH_W)
    x1 = _mix_call(x.reshape(t, d), flat(o_a), flat(o_b), flat(o_c), flat(o_d), lw["w_mg"], lw["w_br"], lw["w_o"],
                   lw["ln1_g"], lw["ln1_b"], alpha)
    x2 = _moe_call(x1, lw["router_w"], lw["router_b"], lw["w_g"], lw["w_u"], lw["w_d"], lw["ln2_g"], lw["ln2_b"],
                   alpha)
    x3 = _ple_call(x2, p.reshape(t, PLE_DIM), lw["ple_w"], lw["ple_gate_w"], lw["ln3_g"], lw["ln3_b"], alpha)
    new = (hgrn_new, rwkv_new, shift_new, lat_new, kr_new, fk.reshape(b, l, N_HEADS, HEAD_DIM),
           fv.reshape(b, l, N_HEADS, HEAD_DIM), lf_new)
    return x3.reshape(b, l, d), new


def _relayout_w_uq(w_uq):
    w = w_uq.reshape(Q_LORA, N_HEADS, NOPE + ROPE)
    w = jnp.pad(w, ((0, 256 - Q_LORA), (0, 0), (0, LANES - NOPE - ROPE)))
    return w.reshape(256, N_HEADS * LANES).astype(BF16)


def _relayout_w_ukv(w_ukv):
    w = w_ukv.reshape(KV_LORA, N_HEADS, NOPE + HEAD_DIM)
    pad = lambda x: jnp.pad(x, ((0, 0), (0, 0), (0, LANES - x.shape[-1]))).reshape(KV_LORA, N_HEADS * LANES)
    return pad(w[..., :NOPE]).astype(BF16), pad(w[..., NOPE:]).astype(BF16)


def kernel(x_prompt, x_sample, state_hgrn, state_rwkv, state_rwkv_shift, cache_mla_latent, cache_mla_krope, cache_fox_k, cache_fox_v, cache_fox_logf, p_prompt, p_sample, ln_in_g, ln_in_b, w_in, hgrn_lb_logits, hgrn_norm_g, rwkv_mu, rwkv_w0, rwkv_w2, rwkv_a0, rwkv_a2, rwkv_g2, rwkv_kk, rwkv_ka, rwkv_rk, rwkv_lnx_g, rwkv_lnx_b, mla_qnorm_g, mla_w_uq, mla_kvnorm_g, mla_w_ukv, fox_bf, w_br, w_mg, w_o, ln1_g, ln1_b, router_w, router_b, exp_w_gate, exp_w_up, exp_w_down, ln2_g, ln2_b, ple_w, ple_gate_w, ln3_g, ln3_b):
    depth = w_in.shape[0]
    alpha = (2 * depth) ** DEPTH_ALPHA_POW
    d = x_prompt.shape[-1]
    lb_all = _lb_call(hgrn_lb_logits)

    def ln_in(x):
        return _ln_call(x.reshape(-1, d), ln_in_g, ln_in_b).reshape(x.shape)

    xp, xs = ln_in(x_prompt), ln_in(x_sample)
    new_p, new_s = [], []
    for i in range(depth):
        w_k, w_v = _relayout_w_ukv(mla_w_ukv[i])
        lw = dict(
            w_in=_split_w_in(w_in[i]), lb=lb_all[i], hgrn_norm_g=hgrn_norm_g[i], rwkv_mu=rwkv_mu[i],
            rwkv_w0=rwkv_w0[i], rwkv_w2=rwkv_w2[i], rwkv_a0=rwkv_a0[i], rwkv_a2=rwkv_a2[i], rwkv_g2=rwkv_g2[i],
            rwkv_kk=rwkv_kk[i], rwkv_ka=rwkv_ka[i], rwkv_rk=rwkv_rk[i], rwkv_lnx_g=rwkv_lnx_g[i],
            rwkv_lnx_b=rwkv_lnx_b[i],
            mla_qn_g=jnp.pad(mla_qnorm_g[i], (0, 256 - Q_LORA)).reshape(1, 256), mla_w_uq=_relayout_w_uq(mla_w_uq[i]),
            mla_kvn_g=mla_kvnorm_g[i].reshape(1, KV_LORA), mla_w_k=w_k, mla_w_v=w_v, fox_bf=fox_bf[i],
            w_br=w_br[i].astype(BF16), w_mg=w_mg[i].astype(BF16), w_o=w_o[i].astype(BF16),
            ln1_g=ln1_g[i], ln1_b=ln1_b[i], router_w=router_w, router_b=router_b,
            w_g=exp_w_gate[i].astype(BF16), w_u=exp_w_up[i].astype(BF16), w_d=exp_w_down[i].astype(BF16),
            ln2_g=ln2_g[i], ln2_b=ln2_b[i], ple_w=ple_w[i].astype(BF16), ple_gate_w=ple_gate_w[i].astype(BF16),
            ln3_g=ln3_g[i], ln3_b=ln3_b[i])
        xp, st_p = _layer(xp, p_prompt[i], (None,) * 8, lw, alpha)
        new_p.append(st_p)
        st_in = (state_hgrn[i], state_rwkv[i], state_rwkv_shift[i], cache_mla_latent[i], cache_mla_krope[i],
                 cache_fox_k[i], cache_fox_v[i], cache_fox_logf[i])
        xs, st_s = _layer(xs, p_sample[i], st_in, lw, alpha)
        new_s.append(st_s)
    stack = lambda sts, j: jnp.stack([s[j] for s in sts], axis=0)
    outs_p = tuple(stack(new_p, j) for j in range(8))
    outs_s = tuple(stack(new_s, j) for j in range(8))
    return (xp, xs) + outs_p + outs_s
```

```python
import functools
import math

import jax
import jax.numpy as jnp
from jax import lax
from jax.experimental import pallas as pl
from jax.experimental.pallas import tpu as pltpu

F32 = jnp.float32
BF16 = jnp.bfloat16

D_MODEL = 1024
N_HEADS = 4
HEAD_DIM = 64
BRANCH_W = N_HEADS * HEAD_DIM
CHUNK = 64
W_LORA, A_LORA, G_LORA = 32, 32, 64
NOPE, ROPE, Q_LORA, KV_LORA = 64, 32, 192, 128
ROPE_THETA = 10000.0
N_EXPERTS, N_GROUPS, EXPERT_FF = 16, 4, 256
GROUP_SIZE = N_EXPERTS // N_GROUPS
PLE_DIM = 256
A_COLS = 4 * BRANCH_W
B_COLS = 3 * BRANCH_W + W_LORA + A_LORA + G_LORA
DEPTH_ALPHA_POW = 0.25
LN_EPS = 1e-5
RMS_EPS = 1e-6
RWKV_GN_EPS = 64e-5
MASK_VALUE = -1e30
LOG2E = math.log2(math.e)

LANES = 128
SUBLANES = 8
VMEM_LIMIT_BYTES = 56 * 1024 * 1024

NN = ((1,), (0,))
NT = ((1,), (1,))
TN = ((0,), (0,))


def _dg(a, b, dims=NN):
    return lax.dot_general(a, b, (dims, ((), ())), preferred_element_type=F32)


def _bdot(a, b, dims=NN):
    return _dg(a.astype(BF16), b.astype(BF16), dims)


def _split(x, pieces):
    out = []
    r = x
    for i in range(pieces):
        p = r.astype(BF16)
        out.append(p)
        if i + 1 < pieces:
            r = r - p.astype(F32)
    return out


def _dot3(a, b, dims=NN):
    ah, al = _split(a, 2)
    bh, bl = _split(b, 2)
    return _dg(ah, bh, dims) + (_dg(ah, bl, dims) + _dg(al, bh, dims))


def _dotc(a, c, dims=NN, pieces=3):
    ps = _split(a, pieces)
    acc = _dg(ps[0], c, dims)
    for p in ps[1:]:
        acc = acc + _dg(p, c, dims)
    return acc


def _tree_sum(xs):
    while len(xs) > 1:
        xs = [xs[i] + xs[i + 1] for i in range(0, len(xs) - 1, 2)] + ([xs[-1]] if len(xs) % 2 else [])
    return xs[0]


def _iota(shape, dim):
    return lax.broadcasted_iota(jnp.int32, shape, dim)


def _tri_incl(n, dtype=BF16):
    return (_iota((n, n), 0) >= _iota((n, n), 1)).astype(dtype)


def _head_indicator():
    r = _iota((BRANCH_W, BRANCH_W), 0) // HEAD_DIM
    c = _iota((BRANCH_W, BRANCH_W), 1) // HEAD_DIM
    return (r == c).astype(BF16)


def _layer_norm(x, g, b):
    mu = jnp.mean(x, axis=-1, keepdims=True)
    xc = x - mu
    var = jnp.mean(xc * xc, axis=-1, keepdims=True)
    return xc * lax.rsqrt(var + LN_EPS) * g + b


def _softplus(x):
    return jnp.maximum(x, 0.0) + jnp.log(1.0 + jnp.exp(-jnp.abs(x)))


def _silu(x):
    return x * jax.nn.sigmoid(x)


def _cparams(*sem):
    return pltpu.CompilerParams(dimension_semantics=sem, vmem_limit_bytes=VMEM_LIMIT_BYTES)


def _resident(shape):
    nd = len(shape)
    return pl.BlockSpec(shape, lambda *_: (0,) * nd)


TOKEN_TILE = 512
MOE_TOKEN_TILE = 1024
MOE_BLOCK = 256
HGRN_SUB_BLOCK = 16
ATTN_Q_TILE = 512
ATTN_KV_TILE = 512


def _token_tile(t, largest=TOKEN_TILE):
    tm = largest
    while tm >= SUBLANES:
        if t % tm == 0:
            return tm
        tm //= 2
    raise ValueError(f"token count {t} not a multiple of 8")


def _ln_kernel(x_ref, g_ref, b_ref, o_ref):
    o_ref[...] = _layer_norm(x_ref[...], g_ref[...], b_ref[...])


def _ln_call(x, g, b):
    t, d = x.shape
    tm = _token_tile(t)
    return pl.pallas_call(
        _ln_kernel,
        grid=(t // tm,),
        in_specs=[pl.BlockSpec((tm, d), lambda i: (i, 0)), _resident((1, d)), _resident((1, d))],
        out_specs=pl.BlockSpec((tm, d), lambda i: (i, 0)),
        out_shape=jax.ShapeDtypeStruct((t, d), F32),
        compiler_params=_cparams("parallel"),
        name="ln_in",
    )(x, g.reshape(1, d), b.reshape(1, d))


_IN_SLOTS = (("a", A_COLS, A_COLS), ("b", B_COLS, B_COLS), ("qlat", Q_LORA, 256), ("kvlat", KV_LORA, 128),
             ("kr", ROPE, 128), ("fq", BRANCH_W, BRANCH_W), ("fk", BRANCH_W, BRANCH_W), ("fv", BRANCH_W, BRANCH_W),
             ("ff", N_HEADS, 128))


def _split_w_in(w_in):
    parts, off = [], 0
    for _, width, slot in _IN_SLOTS:
        w = w_in[:, off:off + width].astype(BF16)
        if slot > width:
            w = jnp.pad(w, ((0, 0), (0, slot - width)))
        parts.append(w)
        off += width
    assert off == w_in.shape[1]
    return parts


def _in_kernel(x_ref, *refs):
    n = len(_IN_SLOTS)
    w_refs, bf_ref, o_refs = refs[:n], refs[n], refs[n + 1:]
    xb = x_ref[...].astype(BF16)
    for w_ref, o_ref, (name, _, _) in zip(w_refs, o_refs, _IN_SLOTS):
        cols = _dg(xb, w_ref[...])
        if name == "ff":
            z = cols + bf_ref[...]
            cols = jnp.where(_iota(z.shape, 1) < N_HEADS, -_softplus(-z), 0.0)
            o_refs[-1][...] = cols[:, :N_HEADS]
        o_ref[...] = cols


def _in_call(x, w_parts, fox_bf):
    t, d = x.shape
    tm = _token_tile(t)
    widths = [slot for _, _, slot in _IN_SLOTS] + [N_HEADS]
    return pl.pallas_call(
        _in_kernel,
        grid=(t // tm,),
        in_specs=[pl.BlockSpec((tm, d), lambda i: (i, 0))] + [_resident(w.shape) for w in w_parts]
                 + [_resident((1, LANES))],
        out_specs=[pl.BlockSpec((tm, width), lambda i: (i, 0)) for width in widths],
        out_shape=[jax.ShapeDtypeStruct((t, width), F32) for width in widths],
        compiler_params=_cparams("parallel"),
        name="in_proj",
    )(x, *w_parts, jnp.pad(fox_bf, (0, LANES - N_HEADS)).reshape(1, LANES))


def _hgrn_parts(c_ref, lb_ref, ng_ref, s0_ref, o_ref, s_ref, st_scr, *, chunk, group):
    def init():
        st_scr[...] = s0_ref[...]

    def finish():
        s_ref[...] = st_scr[...]

    return init, _hgrn_stages(c_ref, lb_ref, ng_ref, o_ref, st_scr, chunk, group), finish


def _hgrn_stages(c_ref, lb_ref, ng_ref, o_ref, st_scr, chunk, group):
    c = chunk
    w = BRANCH_W
    hd = HEAD_DIM
    nh = N_HEADS
    sb = min(HGRN_SUB_BLOCK, c)
    seqs = range(group)
    tri = _tri_incl(c)
    ind = _head_indicator()
    st_keep = (_iota((w, w), 0) // hd) == (_iota((w, w), 1) // hd)
    hr_keep = (_iota((nh * sb, w), 0) // sb) == (_iota((nh * sb, w), 1) // hd)
    row8 = _iota((SUBLANES, w), 0)
    lb = lb_ref[...]

    sq = []
    for bi in seqs:
        cols = c_ref[bi]
        q, fz, iv, g = cols[:, :w], cols[:, w:2 * w], cols[:, 2 * w:3 * w], cols[:, 3 * w:]
        f = lb + (1.0 - lb) * jax.nn.sigmoid(fz)
        a = _dotc(tri, jnp.log(f) * LOG2E, pieces=3)
        sq.append(dict(q=q, k=1.0 - f, iv=iv, g=g, a=a, a_last=a[c - 1:c, :], outs=[]))
    yield

    for bi, s in zip(seqs, sq):
        st = st_scr[bi]
        s["o"] = _bdot(s["q"] * jnp.exp2(s["a"]), st, NT)
        kd = s["k"] * jnp.exp2(s["a_last"] - s["a"])
        st_scr[bi] = st * jnp.exp2(s["a_last"]) + jnp.where(st_keep, _bdot(s["iv"], kd, TN), 0.0)
    yield

    for tb in range(c // sb):
        lo = tb * sb
        for s in sq:
            q, k, iv, a = s["q"], s["k"], s["iv"], s["a"]
            parts = []
            for rg in range(sb // SUBLANES):
                r0 = lo + rg * SUBLANES
                q8, a8 = q[r0:r0 + SUBLANES], a[r0:r0 + SUBLANES]
                pieces = []
                for src in range(lo, r0 + SUBLANES):
                    d = a8 - a[src:src + 1]
                    if src >= r0:
                        d = jnp.where(row8 >= src - r0, d, MASK_VALUE)
                    pieces.append(q8 * jnp.exp2(d) * k[src:src + 1])
                e = _dotc(jnp.concatenate(pieces, axis=0), ind, pieces=1)
                parts.append(_tree_sum([e[i * SUBLANES:(i + 1) * SUBLANES] * iv[lo + i:lo + i + 1]
                                        for i in range(r0 + SUBLANES - lo)]))
            o_tb = jnp.concatenate(parts, axis=0)
            if tb > 0:
                a_lo = a[lo:lo + 1]
                qh = q[lo:lo + sb] * jnp.exp2(a[lo:lo + sb] - a_lo)
                kh = k[:lo] * jnp.exp2(a_lo - a[:lo])
                qh_rows = jnp.where(hr_keep, jnp.concatenate([qh] * nh, axis=0), 0.0)
                sc = _bdot(qh_rows, kh, NT)
                ov = jnp.where(hr_keep, _bdot(sc, iv[:lo]), 0.0)
                for h in range(nh):
                    o_tb = o_tb + ov[h * sb:(h + 1) * sb]
            s["outs"].append(o_tb)
        yield

    for bi, s in zip(seqs, sq):
        o = s["o"] + jnp.concatenate(s["outs"], axis=0)
        ms = _dotc(o * o, ind, pieces=2) * (1.0 / hd)
        o = o * lax.rsqrt(ms + RMS_EPS) * ng_ref[...] * _silu(s["g"])
        o_ref[bi] = o.astype(o_ref.dtype)


def _hgrn_state_to_blockdiag(s):
    b = s.shape[0]
    out = jnp.zeros((b, N_HEADS, HEAD_DIM, N_HEADS, HEAD_DIM), F32)
    for h in range(N_HEADS):
        out = out.at[:, h, :, h, :].set(jnp.swapaxes(s[:, h], -1, -2))
    return out.reshape(b, BRANCH_W, BRANCH_W)


def _hgrn_state_from_blockdiag(st):
    hs = [st[:, h * HEAD_DIM:(h + 1) * HEAD_DIM, h * HEAD_DIM:(h + 1) * HEAD_DIM] for h in range(N_HEADS)]
    return jnp.swapaxes(jnp.stack(hs, axis=1), -1, -2)


def _rwkv_tdot(a, b, dims=NN):
    return _bdot(a, b, dims)


def _rwkv_sdot(a, b, dims=NN):
    return _dot3(a, b, dims)


def _rwkv_operands(cols, prev, mu_ref, w0_ref, lora_ref, a0_ref, kk_ref, ka_ref, rk_ref):
    shifted = jnp.where(_iota(cols.shape, 0) == 0, prev, pltpu.roll(cols, 1, 0))
    m = cols + (shifted - cols) * mu_ref[...]
    w = BRANCH_W
    r, k, v = m[:, :w], m[:, w:2 * w], m[:, 2 * w:3 * w]
    slab = m[:, 3 * w:]
    lane = _iota(slab.shape, 1)
    act = jnp.where(lane < W_LORA, jnp.tanh(slab),
                    jnp.where(lane < W_LORA + A_LORA, slab, jax.nn.sigmoid(slab)))
    lora = _bdot(act, lora_ref[...])
    w_log = -_softplus(-(w0_ref[...] + lora[:, :w])) - 0.5
    a_rate = jax.nn.sigmoid(a0_ref[...] + lora[:, w:2 * w])
    g = lora[:, 2 * w:]
    ind = _head_indicator()
    kk = k * kk_ref[...]
    norm = jnp.sqrt(_dotc(kk * kk, ind, pieces=3))
    kk = kk / jnp.maximum(norm, 1e-12)
    kh = k * (1.0 + (a_rate - 1.0) * ka_ref[...])
    return dict(r=r, lw=-jnp.exp(w_log), k=kh, v=v, a=-kk, b=kk * a_rate, g=g,
                bonus=_dotc(r * kh * rk_ref[...], ind, pieces=3) * v)


def _rwkv_parts(c_ref, prow_ref, mu_ref, w0_ref, lora_ref, a0_ref, kk_ref, ka_ref, rk_ref, lg_ref, lb_ref,
                s0_ref, o_ref, s_ref, st_scr, prev_scr, *, chunk, group):
    hd = HEAD_DIM
    blocks = [(bi, slice(h * hd, (h + 1) * hd)) for bi in range(group) for h in range(N_HEADS)]

    def init():
        st_scr[...] = jnp.zeros(st_scr.shape, F32)
        prev_scr[...] = prow_ref[...]
        for bi, sl in blocks:
            st_scr[bi, sl, sl] = s0_ref[bi, sl.start // hd]

    def finish():
        for bi, sl in blocks:
            s_ref[bi, sl.start // hd] = st_scr[bi, sl, sl]

    stages = _rwkv_stages(c_ref, mu_ref, w0_ref, lora_ref, a0_ref, kk_ref, ka_ref, rk_ref, lg_ref, lb_ref, o_ref,
                          st_scr, prev_scr, chunk, group)
    return init, stages, finish


def _rwkv_stages(c_ref, mu_ref, w0_ref, lora_ref, a0_ref, kk_ref, ka_ref, rk_ref, lg_ref, lb_ref, o_ref, st_scr,
                 prev_scr, chunk, group):
    c = chunk
    w = BRANCH_W
    hd = HEAD_DIM
    nh = N_HEADS
    n = nh * c
    seqs = range(group)
    tri = _tri_incl(c)
    ind = _head_indicator()
    t_w = _iota((c, n), 0)
    s_w = _iota((c, n), 1) % c
    strict_w = t_w > s_w
    incl_w = t_w >= s_w
    eye_w = (t_w == s_w).astype(F32)
    bd_keep = (_iota((n, n), 0) // c) == (_iota((n, n), 1) // c)
    hr_keep = (_iota((n, w), 0) // c) == (_iota((n, w), 1) // hd)
    st_keep = (_iota((w, w), 0) // hd) == (_iota((w, w), 1) // hd)

    def bd(x):
        return jnp.where(bd_keep, jnp.concatenate([x] * nh, axis=0), 0.0)

    def head_rows(x):
        return jnp.where(hr_keep, jnp.concatenate([x] * nh, axis=0), 0.0)

    sq = []
    for bi in seqs:
        cols = c_ref[bi]
        ops = _rwkv_operands(cols, prev_scr[bi], mu_ref, w0_ref, lora_ref, a0_ref, kk_ref, ka_ref, rk_ref)
        prev_scr[bi] = cols[c - 1:c]
        r, lw, k, v, a, b = (ops[name] for name in ("r", "lw", "k", "v", "a", "b"))
        cum = _dotc(tri, lw, pieces=3)
        last = cum[c - 1:c, :]
        e_neg = jnp.exp(-cum)
        e_last = jnp.exp(last - cum)
        sq.append(dict(v=v, last=last, g=ops["g"], bonus=ops["bonus"],
                       lhs=jnp.concatenate([a * jnp.exp(cum - lw), r * jnp.exp(cum)], axis=0),
                       bt=b * e_neg, kt=k * e_neg, hat=jnp.concatenate([b * e_last, k * e_last], axis=0)))
    yield

    for q in sq:
        pb = _rwkv_tdot(q["lhs"], head_rows(q["bt"]), NT)
        pk = _rwkv_tdot(q["lhs"], head_rows(q["kt"]), NT)
        q["l_ab"] = jnp.where(strict_w, pb[:c], 0.0)
        q["a_rb"] = jnp.where(incl_w, pb[c:], 0.0)
        q["lk"] = jnp.concatenate([jnp.where(strict_w, pk[:c], 0.0), jnp.where(incl_w, pk[c:], 0.0)], axis=0)
        q["tinv"] = eye_w + q["l_ab"]
        q["p"] = q["l_ab"]
    yield

    for _ in range(int(math.log2(c)) - 1):
        for q in sq:
            q["p"] = _rwkv_tdot(q["p"], bd(q["p"]))
        yield
        for q in sq:
            q["tinv"] = q["tinv"] + _rwkv_tdot(q["tinv"], bd(q["p"]))
        yield

    for bi, q in zip(seqs, sq):
        q["st"] = st_scr[bi]
        q["sh"] = _rwkv_sdot(q["lhs"], q["st"], NT)
        q["lv"] = _rwkv_tdot(q["lk"], head_rows(q["v"]))
    yield
    for q in sq:
        q["u"] = _rwkv_tdot(q["tinv"], head_rows(q["sh"][:c] + q["lv"][:c]))
    yield
    for bi, q in zip(seqs, sq):
        upd = _rwkv_sdot(jnp.concatenate([q["u"], q["v"]], axis=0), q["hat"], TN)
        st_scr[bi] = q["st"] * jnp.exp(q["last"]) + jnp.where(st_keep, upd, 0.0)
    yield
    for q in sq:
        q["y"] = q["sh"][c:] + _rwkv_tdot(q["a_rb"], head_rows(q["u"])) + q["lv"][c:]
    yield

    for bi, q in zip(seqs, sq):
        y = q["y"]
        mu = _dotc(y, ind, pieces=3) * (1.0 / hd)
        yc = y - mu
        var = _dotc(yc * yc, ind, pieces=2) * (1.0 / hd)
        yn = yc * lax.rsqrt(var + RWKV_GN_EPS) * lg_ref[...] + lb_ref[...]
        o_ref[bi] = ((yn + q["bonus"]) * q["g"]).astype(o_ref.dtype)


def _batch_group(b):
    return 4 if b % 4 == 0 else 1


N_HGRN_IN, N_RWKV_IN = 4, 12
RWKV_STAGES_PER_HGRN_STAGE = 1


def _recurrent_kernel(*refs, chunk, group):
    h_in, refs = refs[:N_HGRN_IN], refs[N_HGRN_IN:]
    r_in, refs = refs[:N_RWKV_IN], refs[N_RWKV_IN:]
    h_o, h_s, r_o, r_s, h_st, r_st, r_prev = refs
    h_init, h_stages, h_finish = _hgrn_parts(*h_in, h_o, h_s, h_st, chunk=chunk, group=group)
    r_init, r_stages, r_finish = _rwkv_parts(*r_in, r_o, r_s, r_st, r_prev, chunk=chunk, group=group)
    j = pl.program_id(1)

    @pl.when(j == 0)
    def _():
        h_init()
        r_init()

    live = [r_stages] * RWKV_STAGES_PER_HGRN_STAGE + [h_stages]
    while live:
        for stages in list(live):
            if stages in live and next(stages, "done") == "done":
                live = [s for s in live if s is not stages]

    @pl.when(j == pl.num_programs(1) - 1)
    def _():
        h_finish()
        r_finish()


def _recurrent_call(cols_a, lb, norm_g, hgrn_st0, cols_b, prev_row, rwkv_s0, mu, w0, w2, a0, a2, g2, kk_s, ka, rk,
                    lnx_g, lnx_b):
    b, l, nb = cols_b.shape
    w = BRANCH_W
    c = CHUNK if l % CHUNK == 0 else l
    bg = _batch_group(b)
    row = lambda x: x.reshape(1, -1)
    lora_w = jnp.zeros((LANES, 3 * w), F32)
    lora_w = lora_w.at[:W_LORA, :w].set(w2).at[W_LORA:W_LORA + A_LORA, w:2 * w].set(a2)
    lora_w = lora_w.at[W_LORA + A_LORA:, 2 * w:].set(g2).astype(BF16)
    tok = lambda n: pl.BlockSpec((bg, c, n), lambda i, j: (i, j, 0))
    hst_spec = pl.BlockSpec((bg, w, w), lambda i, j: (i, 0, 0))
    rst_spec = pl.BlockSpec((bg, N_HEADS, HEAD_DIM, HEAD_DIM), lambda i, j: (i, 0, 0, 0))
    hgrn_in = [tok(A_COLS), _resident((1, w)), _resident((1, w)), hst_spec]
    rwkv_in = ([tok(nb), pl.BlockSpec((bg, 1, nb), lambda i, j: (i, 0, 0)), _resident((1, nb)), _resident((1, w)),
                _resident((LANES, 3 * w))] + [_resident((1, w))] * 6 + [rst_spec])
    assert len(hgrn_in) == N_HGRN_IN and len(rwkv_in) == N_RWKV_IN
    return pl.pallas_call(
        functools.partial(_recurrent_kernel, chunk=c, group=bg),
        grid=(b // bg, l // c),
        in_specs=hgrn_in + rwkv_in,
        out_specs=[tok(w), hst_spec, tok(w), rst_spec],
        out_shape=[jax.ShapeDtypeStruct((b, l, w), BF16), jax.ShapeDtypeStruct((b, w, w), F32),
                   jax.ShapeDtypeStruct((b, l, w), BF16),
                   jax.ShapeDtypeStruct((b, N_HEADS, HEAD_DIM, HEAD_DIM), F32)],
        scratch_shapes=[pltpu.VMEM((bg, w, w), F32), pltpu.VMEM((bg, w, w), F32), pltpu.VMEM((bg, 1, nb), F32)],
        compiler_params=_cparams("parallel", "arbitrary"),
        name="hgrn_rwkv",
    )(cols_a, row(lb), row(norm_g), hgrn_st0,
      cols_b, prev_row.reshape(b, 1, nb), row(mu), row(w0), lora_w, row(a0), row(kk_s), row(ka), row(rk),
      row(lnx_g), row(lnx_b), rwkv_s0)


def _rope_tables(p, l):
    half = ROPE // 2
    inv = 1.0 / (ROPE_THETA ** (jnp.arange(half, dtype=F32) / half))
    ang = (p + jnp.arange(l, dtype=jnp.int32)).astype(F32)[:, None] * inv[None, :]
    cos, sin = jnp.cos(ang), jnp.sin(ang)
    pad = jnp.zeros((l, LANES - ROPE), F32)
    return jnp.concatenate([cos, cos, pad], axis=1), jnp.concatenate([-sin, sin, pad], axis=1)


def _swap_halves(x, base):
    half = ROPE // 2
    n = x.shape[-1]
    lane = _iota(x.shape, x.ndim - 1) % LANES
    up = pltpu.roll(x, n - half, x.ndim - 1)
    down = pltpu.roll(x, half, x.ndim - 1)
    return jnp.where((lane >= base) & (lane < base + half), up,
                     jnp.where((lane >= base + half) & (lane < base + ROPE), down, 0.0))


def _rms_norm(x, g, width):
    ms = jnp.sum(x * x, axis=-1, keepdims=True) * (1.0 / width)
    return x * lax.rsqrt(ms + RMS_EPS) * g


def _mla_q_kernel(ql_ref, kvl_ref, kr_ref, cos_ref, sin_ref, qg_ref, wuq_ref, wuqp_ref, kvg_ref, q_ref, c_ref,
                  krn_ref):
    scale = (NOPE + ROPE) ** -0.5 * LOG2E
    qn = _rms_norm(ql_ref[0], qg_ref[...], Q_LORA).astype(BF16)
    q = _dg(qn, wuq_ref[...])
    q_partner = _dg(qn, wuqp_ref[...])
    cos, sin = cos_ref[...], sin_ref[...]
    lane = _iota(cos.shape, 1)
    cos_q = jnp.where(lane < NOPE, 1.0, pltpu.roll(cos, NOPE, 1))
    sin_q = pltpu.roll(sin, NOPE, 1)
    for h in range(N_HEADS):
        sl = slice(h * LANES, (h + 1) * LANES)
        q_ref[0, h] = ((q[:, sl] * cos_q + q_partner[:, sl] * sin_q) * scale).astype(q_ref.dtype)
    c_ref[0] = _rms_norm(kvl_ref[0], kvg_ref[...], KV_LORA)
    kr = kr_ref[0]
    krn = kr * cos + _swap_halves(kr, 0) * sin
    krn_ref[0] = krn[:, :ROPE]


def _mla_q_call(qlat, kvlat, kr, cos, sin, qn_g, w_uq_pair, kvn_g):
    b, l, _ = qlat.shape
    tm = _token_tile(l)
    tok = lambda n: pl.BlockSpec((1, tm, n), lambda i, j: (i, j, 0))
    tab = pl.BlockSpec((tm, LANES), lambda i, j: (j, 0))
    return pl.pallas_call(
        _mla_q_kernel,
        grid=(b, l // tm),
        in_specs=[tok(256), tok(KV_LORA), tok(LANES), tab, tab, _resident((1, 256)),
                  _resident((256, N_HEADS * LANES)), _resident((256, N_HEADS * LANES)), _resident((1, KV_LORA))],
        out_specs=[pl.BlockSpec((1, N_HEADS, tm, LANES), lambda i, j: (i, 0, j, 0)), tok(KV_LORA), tok(ROPE)],
        out_shape=[jax.ShapeDtypeStruct((b, N_HEADS, l, LANES), BF16), jax.ShapeDtypeStruct((b, l, KV_LORA), F32),
                   jax.ShapeDtypeStruct((b, l, ROPE), F32)],
        compiler_params=_cparams("parallel", "parallel"),
        name="mla_q",
    )(qlat, kvlat, kr, cos, sin, qn_g, *w_uq_pair, kvn_g)


def _mla_kv_kernel(c_ref, kr_ref, wk_ref, wv_ref, k_ref, v_ref):
    cb = c_ref[0].astype(BF16)
    kall = _dg(cb, wk_ref[...])
    vall = _dg(cb, wv_ref[...])
    tm = cb.shape[0]
    place = (_iota((ROPE, LANES), 0) + NOPE == _iota((ROPE, LANES), 1)).astype(BF16)
    kr_slot = _dg(kr_ref[0].astype(BF16), place)
    one_slot = (_iota((tm, LANES), 1) == HEAD_DIM).astype(F32)
    for h in range(N_HEADS):
        k_ref[0, h] = (kall[:, h * LANES:(h + 1) * LANES] + kr_slot).astype(k_ref.dtype)
        v_ref[0, h, 0] = (vall[:, h * LANES:(h + 1) * LANES] + one_slot).T.astype(v_ref.dtype)


def _kv_specs(b, t, tm):
    kspec = pl.BlockSpec((1, N_HEADS, tm, LANES), lambda i, j: (i, 0, j, 0))
    vspec = pl.BlockSpec((1, N_HEADS, 1, LANES, tm), lambda i, j: (i, 0, j, 0, 0))
    shapes = [jax.ShapeDtypeStruct((b, N_HEADS, t, LANES), BF16),
              jax.ShapeDtypeStruct((b, N_HEADS, t // tm, LANES, tm), BF16)]
    return [kspec, vspec], shapes


def _mla_kv_call(c_all, kr_all, w_k_p, w_v_p):
    b, t, _ = c_all.shape
    tm = _token_tile(t, ATTN_KV_TILE)
    tok = lambda n: pl.BlockSpec((1, tm, n), lambda i, j: (i, j, 0))
    out_specs, out_shape = _kv_specs(b, t, tm)
    return pl.pallas_call(
        _mla_kv_kernel,
        grid=(b, t // tm),
        in_specs=[tok(KV_LORA), tok(ROPE), _resident((KV_LORA, N_HEADS * LANES)),
                  _resident((KV_LORA, N_HEADS * LANES))],
        out_specs=out_specs,
        out_shape=out_shape,
        compiler_params=_cparams("parallel", "parallel"),
        name="mla_kv",
    )(c_all, kr_all, w_k_p, w_v_p)


def _head_slot(x, h):
    pair = x[:, (h // 2) * LANES:(h // 2 + 1) * LANES]
    return pair if h % 2 == 0 else pltpu.roll(pair, HEAD_DIM, 1)


FOX_KEY_BIAS = HEAD_DIM
FOX_QUERY_BIAS = HEAD_DIM + 3 * N_HEADS


def _bias_pieces(c, base):
    p0 = c.astype(BF16).astype(F32)
    r1 = c - p0
    p1 = r1.astype(BF16).astype(F32)
    p2 = r1 - p1
    return (pltpu.roll(p0, base, 1) + pltpu.roll(p1, base + N_HEADS, 1)) + pltpu.roll(p2, base + 2 * N_HEADS, 1)


def _bias_ones(shape, base, h):
    lane = _iota(shape, 1)
    hit = (lane == base + h) | (lane == base + N_HEADS + h) | (lane == base + 2 * N_HEADS + h)
    return hit.astype(F32)


def _fox_q_kernel(q_ref, c_ref, o_ref):
    scale = HEAD_DIM ** -0.5 * LOG2E
    q = q_ref[0]
    pieces = _bias_pieces(c_ref[0] * LOG2E, FOX_QUERY_BIAS)
    lane = _iota(pieces.shape, 1)
    for h in range(N_HEADS):
        bias = pieces + _bias_ones(pieces.shape, FOX_KEY_BIAS, h)
        o_ref[0, h] = jnp.where(lane < HEAD_DIM, _head_slot(q, h) * scale, bias).astype(o_ref.dtype)


def _fox_kv_kernel(k_ref, v_ref, lf_ref, ko_ref, vo_ref, c_ref, carry):
    @pl.when(pl.program_id(1) == 0)
    def _():
        carry[...] = jnp.zeros_like(carry)

    k, v, lf = k_ref[0], v_ref[0], lf_ref[0]
    tm = k.shape[0]
    c = _dotc(_tri_incl(tm), lf, pieces=3) + carry[...]
    c_ref[0] = c
    carry[...] = c[tm - 1:, :]
    pieces = _bias_pieces(-c * LOG2E, FOX_KEY_BIAS)
    lane = _iota((tm, LANES), 1)
    for h in range(N_HEADS):
        bias = pieces + _bias_ones(pieces.shape, FOX_QUERY_BIAS, h)
        ko_ref[0, h] = jnp.where(lane < HEAD_DIM, _head_slot(k, h), bias).astype(ko_ref.dtype)
        vo_ref[0, h, 0] = jnp.where(lane < HEAD_DIM, _head_slot(v, h),
                                    (lane == HEAD_DIM).astype(F32)).T.astype(vo_ref.dtype)


def _fox_q_call(q, c_new):
    b, l, w = q.shape
    tm = _token_tile(l)
    return pl.pallas_call(
        _fox_q_kernel,
        grid=(b, l // tm),
        in_specs=[pl.BlockSpec((1, tm, w), lambda i, j: (i, j, 0)),
                  pl.BlockSpec((1, tm, LANES), lambda i, j: (i, j, 0))],
        out_specs=pl.BlockSpec((1, N_HEADS, tm, LANES), lambda i, j: (i, 0, j, 0)),
        out_shape=jax.ShapeDtypeStruct((b, N_HEADS, l, LANES), BF16),
        compiler_params=_cparams("parallel", "parallel"),
        name="fox_q",
    )(q, c_new)


def _fox_kv_call(k_all, v_all, lf_all):
    b, t, w = k_all.shape
    tm = _token_tile(t, ATTN_KV_TILE)
    tok = pl.BlockSpec((1, tm, w), lambda i, j: (i, j, 0))
    wide = pl.BlockSpec((1, tm, LANES), lambda i, j: (i, j, 0))
    out_specs, out_shape = _kv_specs(b, t, tm)
    return pl.pallas_call(
        _fox_kv_kernel,
        grid=(b, t // tm),
        in_specs=[tok, tok, wide],
        out_specs=out_specs + [wide],
        out_shape=out_shape + [jax.ShapeDtypeStruct((b, t, LANES), F32)],
        scratch_shapes=[pltpu.VMEM((1, LANES), F32)],
        compiler_params=_cparams("parallel", "arbitrary"),
        name="fox_kv",
    )(k_all, v_all, lf_all)


def _flash_kernel(q_ref, k_ref, v_ref, o_ref, m_scr, acc_scr, s_scr, *, tq, tk, past, t_valid, chunk_mask):
    qi = pl.program_id(1)
    q_start = past + qi * tq
    if chunk_mask:
        vis_end = jnp.minimum(((q_start + tq + CHUNK - 1) // CHUNK) * CHUNK, t_valid)
    else:
        vis_end = q_start + tq
    n_blocks = (vis_end + tk - 1) // tk
    n_full = q_start // tk

    m_scr[...] = jnp.full(m_scr.shape, MASK_VALUE, F32)
    acc_scr[...] = jnp.zeros(acc_scr.shape, F32)

    def logits(h, kb):
        ks = pl.multiple_of(kb * tk, tk)
        return _dg(k_ref[0, h, pl.ds(ks, tk), :], q_ref[0, h], NT)

    s_scr[...] = logits(0, 0)

    def block(kb, masked, s_next):
        if masked:
            kpos = kb * tk + _iota((tk, tq), 0)
            qpos = q_start + _iota((tk, tq), 1)
            if chunk_mask:
                keep = ((kpos // CHUNK) <= (qpos // CHUNK)) & (kpos < t_valid)
            else:
                keep = kpos <= qpos
        for h in range(N_HEADS):
            s = jnp.where(keep, s_next, MASK_VALUE) if masked else s_next
            if h + 1 < N_HEADS:
                s_next = logits(h + 1, kb)
            else:
                s_next = logits(0, jnp.minimum(kb + 1, n_blocks - 1))
            m_old = m_scr[h]
            m_new = jnp.maximum(m_old, jnp.max(s, axis=0, keepdims=True))
            p = jnp.exp2(s - m_new[:1])
            pv = _dg(v_ref[0, h, kb], p.astype(BF16))
            acc_scr[h] = acc_scr[h] * jnp.exp2(m_old - m_new)[:1] + pv
            m_scr[h] = m_new
        return s_next

    def full_body(kb, carry):
        s_scr[...] = block(kb, False, s_scr[...])
        return carry

    def masked_body(kb, carry):
        s_scr[...] = block(kb, True, s_scr[...])
        return carry

    lax.fori_loop(0, n_full, full_body, 0)
    lax.fori_loop(n_full, n_blocks, masked_body, 0)
    for h in range(N_HEADS):
        acc = acc_scr[h].T
        o = acc[:, :HEAD_DIM] / acc[:, HEAD_DIM:HEAD_DIM + 1]
        o_ref[0, :, h * HEAD_DIM:(h + 1) * HEAD_DIM] = o.astype(o_ref.dtype)


def _flash_call(q, k, v, *, past, t_valid, chunk_mask, name):
    b, nh, l, _ = q.shape
    t = k.shape[2]
    tk = v.shape[-1]
    lq = max(l, LANES)
    if lq != l:
        q = jnp.pad(q, ((0, 0), (0, 0), (0, lq - l), (0, 0)))
    tq = _token_tile(lq, ATTN_Q_TILE)
    out = pl.pallas_call(
        functools.partial(_flash_kernel, tq=tq, tk=tk, past=past, t_valid=t_valid, chunk_mask=chunk_mask),
        grid=(b, lq // tq),
        in_specs=[pl.BlockSpec((1, nh, tq, LANES), lambda i, j: (i, 0, j, 0)),
                  pl.BlockSpec((1, nh, t, LANES), lambda i, j: (i, 0, 0, 0)),
                  pl.BlockSpec((1, nh, t // tk, LANES, tk), lambda i, j: (i, 0, 0, 0, 0))],
        out_specs=pl.BlockSpec((1, tq, BRANCH_W), lambda i, j: (i, j, 0)),
        out_shape=jax.ShapeDtypeStruct((b, lq, BRANCH_W), BF16),
        scratch_shapes=[pltpu.VMEM((nh, SUBLANES, tq), F32), pltpu.VMEM((nh, LANES, tq), F32),
                        pltpu.VMEM((tk, tq), F32)],
        compiler_params=_cparams("parallel", "arbitrary"),
        name=name,
    )(q, k, v)
    return out[:, :l]


def _mix_kernel(x_ref, oa_ref, ob_ref, oc_ref, od_ref, wmg_ref, wbr_ref, wo_ref, g_ref, b_ref, o_ref, *, alpha):
    x = x_ref[...]
    xb = x.astype(BF16)
    d = D_MODEL
    acc = None
    for n, br_ref in enumerate((oa_ref, ob_ref, oc_ref, od_ref)):
        gate = jax.nn.sigmoid(_dg(xb, wmg_ref[:, n * d:(n + 1) * d]))
        term = gate * _dg(br_ref[...], wbr_ref[n])
        acc = term if acc is None else acc + term
    mix = _dg(acc.astype(BF16), wo_ref[...])
    o_ref[...] = _layer_norm(alpha * x + mix, g_ref[...], b_ref[...])


def _mix_call(x, oa, ob, oc, od, w_mg, w_br, w_o, g, b, alpha):
    t, d = x.shape
    tm = _token_tile(t)
    tok = lambda n: pl.BlockSpec((tm, n), lambda i: (i, 0))
    return pl.pallas_call(
        functools.partial(_mix_kernel, alpha=alpha),
        grid=(t // tm,),
        in_specs=[tok(d)] + [tok(BRANCH_W)] * 4 + [_resident(w_mg.shape), _resident(w_br.shape),
                                                   _resident(w_o.shape), _resident((1, d)), _resident((1, d))],
        out_specs=tok(d),
        out_shape=jax.ShapeDtypeStruct((t, d), F32),
        compiler_params=_cparams("parallel"),
        name="mix_ln1",
    )(x, oa, ob, oc, od, w_mg, w_br, w_o, g.reshape(1, d), b.reshape(1, d))


def _route(x, rw_ref, rb_ref):
    scores = jax.nn.sigmoid(_dot3(rw_ref[...], x, NT)[:N_EXPERTS])
    biased = scores + rb_ref[...][:N_EXPERTS]
    col = [biased[e:e + 1] for e in range(N_EXPERTS)]
    gs = []
    for g in range(N_GROUPS):
        v = col[g * GROUP_SIZE:(g + 1) * GROUP_SIZE]
        best = None
        for i in range(GROUP_SIZE):
            for j in range(i + 1, GROUP_SIZE):
                s = v[i] + v[j]
                best = s if best is None else jnp.maximum(best, s)
        gs.append(best)
    gates, sels = [], []
    for g in range(N_GROUPS):
        sel = None
        for o in range(N_GROUPS):
            if o == g:
                continue
            cond = (gs[g] > gs[o]) if o < g else (gs[g] >= gs[o])
            sel = cond if sel is None else sel & cond
        sels.append(sel)
        v = col[g * GROUP_SIZE:(g + 1) * GROUP_SIZE]
        for i in range(GROUP_SIZE):
            rank = None
            for j in range(GROUP_SIZE):
                if j == i:
                    continue
                ahead = (v[j] >= v[i]) if j < i else (v[j] > v[i])
                ahead = ahead.astype(F32)
                rank = ahead if rank is None else rank + ahead
            e = g * GROUP_SIZE + i
            gates.append(jnp.where(sel & (rank < 2.0), scores[e:e + 1], 0.0))
    total = gates[0]
    for gt in gates[1:]:
        total = total + gt
    return [gt / total for gt in gates], sels


MOE_POS_LANE = N_EXPERTS
MOE_ROW_FORM = 24
MOE_META_GROUP = 32
MOE_META_COUNT = 64


def _moe_route_kernel(x_ref, rw_ref, rb_ref, gp_ref, posr_ref, meta_ref, *, rows):
    tm = x_ref.shape[0]
    gates, sels = _route(x_ref[...], rw_ref, rb_ref)
    gsel = jnp.concatenate([s.astype(F32) for s in sels] + [jnp.zeros((SUBLANES - N_GROUPS, tm), F32)], axis=0)
    before = (_iota((tm, tm), 0) < _iota((tm, tm), 1)).astype(BF16)
    rank = _dg(gsel.astype(BF16), before)
    count = jnp.sum(gsel, axis=1, keepdims=True)
    offs = [jnp.zeros((1, 1), F32)]
    for g in range(1, N_GROUPS):
        offs.append(offs[-1] + count[g - 1:g])
    pos = gsel[0:1] * rank[0:1]
    for g in range(1, N_GROUPS):
        pos = pos + gsel[g:g + 1] * (offs[g] + rank[g:g + 1])
    gt = jnp.concatenate(gates + [pos, jnp.zeros((LANES - N_EXPERTS - 1, tm), F32)], axis=0)
    posr_ref[0] = gt[:MOE_ROW_FORM]
    gp_ref[...] = gt.T

    lane = _iota((1, LANES), 1)
    lane_g = lane - MOE_META_GROUP
    shift = int(math.log2(rows))
    meta = jnp.zeros((1, LANES), jnp.int32)
    start = jnp.zeros((1, 1), jnp.int32)
    for g in range(N_GROUPS):
        n_g = count[g:g + 1].astype(jnp.int32)
        off_g = offs[g].astype(jnp.int32)
        first = lax.shift_right_arithmetic(off_g, shift)
        last = lax.shift_right_arithmetic(off_g + n_g - 1, shift)
        k_g = jnp.where(n_g > 0, last - first + 1, 0)
        meta = jnp.where((lane >= start) & (lane < start + k_g), first + (lane - start), meta)
        meta = jnp.where((lane_g >= start) & (lane_g < start + k_g), g, meta)
        start = start + k_g
    meta_ref[0] = jnp.where(lane == MOE_META_COUNT, start, meta)


def _moe_sparse_kernel(meta_ref, x_ref, gp_ref, posr_ref, wg_ref, wu_ref, wd_ref, g_ref, b_ref, o_ref,
                       xb_scr, acc_scr, xs_scr, gs_scr, y_scr, back_scr, *, alpha, rows):
    i = pl.program_id(0)
    w = pl.program_id(1)
    tm = x_ref.shape[0]
    blk = meta_ref[i, w]
    grp = meta_ref[i, MOE_META_GROUP + w]
    n_items = meta_ref[i, MOE_META_COUNT]
    valid = w < n_items
    first_of_blk = valid & ((w == 0) | (meta_ref[i, jnp.maximum(w - 1, 0)] != blk))
    last_of_blk = valid & ((w + 1 >= n_items) | (meta_ref[i, w + 1] != blk))
    base = (blk * rows).astype(F32)

    @pl.when(w == 0)
    def _():
        xb_scr[...] = x_ref[...].astype(BF16)
        acc_scr[...] = jnp.zeros(acc_scr.shape, F32)

    @pl.when(first_of_blk)
    def _():
        rowform = posr_ref[0]
        pos_row = rowform[MOE_POS_LANE:MOE_POS_LANE + 1]
        perm = (pos_row == base + _iota((rows, tm), 0).astype(F32)).astype(BF16)
        back = (gp_ref[:, MOE_POS_LANE:MOE_POS_LANE + 1]
                == base + _iota((tm, rows), 1).astype(F32)).astype(BF16)
        back_scr[...] = back
        xs_scr[...] = _dg(perm, xb_scr[...]).astype(BF16)
        gst = _dotc(rowform[:N_EXPERTS], back, pieces=3)
        gst = jnp.concatenate([gst, jnp.zeros((LANES - N_EXPERTS, rows), F32)], axis=0)
        gs_scr[...] = gst.T
        y_scr[...] = jnp.zeros(y_scr.shape, F32)

    @pl.when(valid)
    def _():
        xs = xs_scr[...]
        gs = gs_scr[...]
        lane = _iota(gs.shape, 1)
        y = y_scr[...]
        for e in range(GROUP_SIZE):
            idx = grp * GROUP_SIZE + e
            gate = jnp.sum(jnp.where(lane == idx, gs, 0.0), axis=-1, keepdims=True)
            h = _silu(_dg(xs, wg_ref[idx])) * _dg(xs, wu_ref[idx]) * gate
            y = y + _dg(h.astype(BF16), wd_ref[idx])
        y_scr[...] = y

    @pl.when(last_of_blk)
    def _():
        acc_scr[...] += _dg(back_scr[...], y_scr[...].astype(BF16))

    @pl.when(w == pl.num_programs(1) - 1)
    def _():
        o_ref[...] = _layer_norm(alpha * x_ref[...] + acc_scr[...], g_ref[...], b_ref[...])


def _moe_call(x, router_w, router_b, w_g, w_u, w_d, g, b, alpha):
    t, d = x.shape
    tm = _token_tile(t, MOE_TOKEN_TILE)
    nt = t // tm
    rows = min(MOE_BLOCK, tm)
    tok = pl.BlockSpec((tm, d), lambda i: (i, 0))
    gp, posr, meta = pl.pallas_call(
        functools.partial(_moe_route_kernel, rows=rows),
        grid=(nt,),
        in_specs=[tok, _resident((LANES, d)), _resident((LANES, 1))],
        out_specs=[pl.BlockSpec((tm, LANES), lambda i: (i, 0)), pl.BlockSpec((1, MOE_ROW_FORM, tm), lambda i: (i, 0, 0)),
                   pl.BlockSpec((1, 1, LANES), lambda i: (i, 0, 0))],
        out_shape=[jax.ShapeDtypeStruct((t, LANES), F32), jax.ShapeDtypeStruct((nt, MOE_ROW_FORM, tm), F32),
                   jax.ShapeDtypeStruct((nt, 1, LANES), jnp.int32)],
        compiler_params=_cparams("parallel"),
        name="moe_route",
    )(x, jnp.pad(router_w.T, ((0, LANES - N_EXPERTS), (0, 0))),
      jnp.pad(router_b, (0, LANES - N_EXPERTS)).reshape(LANES, 1))

    n_items = tm // rows + N_GROUPS - 1
    once = lambda shape: pl.BlockSpec(shape, lambda i, w, m: (0,) * len(shape), pipeline_mode=pl.Buffered(1))
    tok2 = lambda n: pl.BlockSpec((tm, n), lambda i, w, m: (i, 0))
    return pl.pallas_call(
        functools.partial(_moe_sparse_kernel, alpha=alpha, rows=rows),
        grid_spec=pltpu.PrefetchScalarGridSpec(
            num_scalar_prefetch=1,
            grid=(nt, n_items),
            in_specs=[tok2(d), tok2(LANES), pl.BlockSpec((1, MOE_ROW_FORM, tm), lambda i, w, m: (i, 0, 0)),
                      once(w_g.shape), once(w_u.shape), once(w_d.shape), once((1, d)), once((1, d))],
            out_specs=tok2(d),
            scratch_shapes=[pltpu.VMEM((tm, d), BF16), pltpu.VMEM((tm, d), F32), pltpu.VMEM((rows, d), BF16),
                            pltpu.VMEM((rows, LANES), F32), pltpu.VMEM((rows, d), F32),
                            pltpu.VMEM((tm, rows), BF16)]),
        out_shape=jax.ShapeDtypeStruct((t, d), F32),
        compiler_params=_cparams("parallel", "arbitrary"),
        name="moe_ln2",
    )(meta.reshape(nt, LANES), x, gp, posr, w_g, w_u, w_d, g.reshape(1, d), b.reshape(1, d))


def _ple_kernel(x_ref, p_ref, pw_ref, gw_ref, g_ref, b_ref, o_ref, *, alpha):
    x = x_ref[...]
    ple = _bdot(p_ref[...], pw_ref[...]) * jax.nn.sigmoid(_bdot(x, gw_ref[...]))
    o_ref[...] = _layer_norm(alpha * x + ple, g_ref[...], b_ref[...])


def _ple_call(x, p, ple_w, gate_w, g, b, alpha):
    t, d = x.shape
    tm = _token_tile(t)
    tok = lambda n: pl.BlockSpec((tm, n), lambda i: (i, 0))
    return pl.pallas_call(
        functools.partial(_ple_kernel, alpha=alpha),
        grid=(t // tm,),
        in_specs=[tok(d), tok(PLE_DIM), _resident(ple_w.shape), _resident(gate_w.shape), _resident((1, d)),
                  _resident((1, d))],
        out_specs=tok(d),
        out_shape=jax.ShapeDtypeStruct((t, d), F32),
        compiler_params=_cparams("parallel"),
        name="ple_ln3",
    )(x, p, ple_w, gate_w, g.reshape(1, d), b.reshape(1, d))


def _lb_kernel(x_ref, o_ref):
    x = x_ref[...]
    depth = x.shape[0]
    m = jnp.max(x, axis=0, keepdims=True)
    e = jnp.exp(x - m)
    pr = e / jnp.sum(e, axis=0, keepdims=True)
    run = jnp.zeros_like(pr[0:1])
    for i in range(depth):
        o_ref[i:i + 1, :] = run
        run = run + pr[i:i + 1]


def _lb_call(logits):
    return pl.pallas_call(
        _lb_kernel,
        out_shape=jax.ShapeDtypeStruct(logits.shape, F32),
        name="hgrn_lb",
    )(logits)


def _pad_rows(x, t):
    pad = t - x.shape[1]
    if pad == 0:
        return x
    return jnp.pad(x, ((0, 0), (0, pad)) + ((0, 0),) * (x.ndim - 2))


def _layer(x, p, st, lw, alpha):
    b, l, d = x.shape
    t = b * l
    hgrn_s, rwkv_s, rwkv_prev, lat_past, kr_past, fk_past, fv_past, flf_past = st
    past = 0 if lat_past is None else lat_past.shape[1]
    t_valid = past + l
    t_pad = -(-t_valid // LANES) * LANES if past else t_valid

    cols = _in_call(x.reshape(t, d), lw["w_in"], lw["fox_bf"])
    ca, cb, qlat, kvlat, kr, fq, fk, fv, lf_wide, lf_new = [c.reshape(b, l, -1) for c in cols]

    st0 = jnp.zeros((b, BRANCH_W, BRANCH_W), F32) if hgrn_s is None else _hgrn_state_to_blockdiag(hgrn_s)
    prev_row = jnp.zeros((b, B_COLS), F32) if rwkv_prev is None else rwkv_prev
    s0 = jnp.zeros((b, N_HEADS, HEAD_DIM, HEAD_DIM), F32) if rwkv_s is None else rwkv_s
    o_a, hgrn_bd, o_b, rwkv_new = _recurrent_call(
        ca, lw["lb"], lw["hgrn_norm_g"], st0, cb, prev_row, s0, lw["rwkv_mu"], lw["rwkv_w0"], lw["rwkv_w2"],
        lw["rwkv_a0"], lw["rwkv_a2"], lw["rwkv_g2"], lw["rwkv_kk"], lw["rwkv_ka"], lw["rwkv_rk"],
        lw["rwkv_lnx_g"], lw["rwkv_lnx_b"])
    hgrn_new = _hgrn_state_from_blockdiag(hgrn_bd)
    shift_new = cb[:, l - 1]

    cos, sin = _rope_tables(past, l)
    q_c, lat_new, kr_new = _mla_q_call(qlat, kvlat, kr, cos, sin, lw["mla_qn_g"], lw["mla_w_uq"], lw["mla_kvn_g"])
    if past:
        c_all = _pad_rows(jnp.concatenate([lat_past, lat_new], axis=1), t_pad)
        kr_all = _pad_rows(jnp.concatenate([kr_past, kr_new], axis=1), t_pad)
    else:
        c_all, kr_all = lat_new, kr_new
    k_c, v_c = _mla_kv_call(c_all, kr_all, lw["mla_w_k"], lw["mla_w_v"])
    o_c = _flash_call(q_c, k_c, v_c, past=past, t_valid=t_valid, chunk_mask=True, name="mla_attn")

    if past:
        flf_wide = jnp.pad(flf_past, ((0, 0), (0, 0), (0, LANES - N_HEADS)))
        lf_all = _pad_rows(jnp.concatenate([flf_wide, lf_wide], axis=1), t_pad)
        k_all = _pad_rows(jnp.concatenate([fk_past.reshape(b, past, BRANCH_W), fk], axis=1), t_pad)
        v_all = _pad_rows(jnp.concatenate([fv_past.reshape(b, past, BRANCH_W), fv], axis=1), t_pad)
    else:
        lf_all, k_all, v_all = lf_wide, fk, fv
    k_d, v_d, c_all_f = _fox_kv_call(k_all, v_all, lf_all)
    q_d = _fox_q_call(fq, c_all_f[:, past:past + l])
    o_d = _flash_call(q_d, k_d, v_d, past=past, t_valid=t_valid, chunk_mask=False, name="fox_attn")

    flat = lambda o: o.reshape(t, BRANCH_W)
    x1 = _mix_call(x.reshape(t, d), flat(o_a), flat(o_b), flat(o_c), flat(o_d), lw["w_mg"], lw["w_br"], lw["w_o"],
                   lw["ln1_g"], lw["ln1_b"], alpha)
    x2 = _moe_call(x1, lw["router_w"], lw["router_b"], lw["w_g"], lw["w_u"], lw["w_d"], lw["ln2_g"], lw["ln2_b"],
                   alpha)
    x3 = _ple_call(x2, p.reshape(t, PLE_DIM), lw["ple_w"], lw["ple_gate_w"], lw["ln3_g"], lw["ln3_b"], alpha)
    new = (hgrn_new, rwkv_new, shift_new, lat_new, kr_new, fk.reshape(b, l, N_HEADS, HEAD_DIM),
           fv.reshape(b, l, N_HEADS, HEAD_DIM), lf_new)
    return x3.reshape(b, l, d), new


def _relayout_w_uq(w_uq):
    half = ROPE // 2
    w = w_uq.reshape(Q_LORA, N_HEADS, NOPE + ROPE)
    partner = jnp.concatenate([jnp.zeros_like(w[..., :NOPE]), w[..., NOPE + half:], w[..., NOPE:NOPE + half]], axis=-1)
    pad = lambda a: jnp.pad(a, ((0, 256 - Q_LORA), (0, 0), (0, LANES - NOPE - ROPE))).reshape(256, N_HEADS * LANES)
    return pad(w).astype(BF16), pad(partner).astype(BF16)


def _relayout_w_ukv(w_ukv):
    w = w_ukv.reshape(KV_LORA, N_HEADS, NOPE + HEAD_DIM)
    pad = lambda x: jnp.pad(x, ((0, 0), (0, 0), (0, LANES - x.shape[-1]))).reshape(KV_LORA, N_HEADS * LANES)
    return pad(w[..., :NOPE]).astype(BF16), pad(w[..., NOPE:]).astype(BF16)


def kernel(x_prompt, x_sample, state_hgrn, state_rwkv, state_rwkv_shift, cache_mla_latent, cache_mla_krope, cache_fox_k, cache_fox_v, cache_fox_logf, p_prompt, p_sample, ln_in_g, ln_in_b, w_in, hgrn_lb_logits, hgrn_norm_g, rwkv_mu, rwkv_w0, rwkv_w2, rwkv_a0, rwkv_a2, rwkv_g2, rwkv_kk, rwkv_ka, rwkv_rk, rwkv_lnx_g, rwkv_lnx_b, mla_qnorm_g, mla_w_uq, mla_kvnorm_g, mla_w_ukv, fox_bf, w_br, w_mg, w_o, ln1_g, ln1_b, router_w, router_b, exp_w_gate, exp_w_up, exp_w_down, ln2_g, ln2_b, ple_w, ple_gate_w, ln3_g, ln3_b):
    depth = w_in.shape[0]
    alpha = (2 * depth) ** DEPTH_ALPHA_POW
    d = x_prompt.shape[-1]
    lb_all = _lb_call(hgrn_lb_logits)

    def ln_in(x):
        return _ln_call(x.reshape(-1, d), ln_in_g, ln_in_b).reshape(x.shape)

    xp, xs = ln_in(x_prompt), ln_in(x_sample)
    new_p, new_s = [], []
    for i in range(depth):
        w_k, w_v = _relayout_w_ukv(mla_w_ukv[i])
        lw = dict(
            w_in=_split_w_in(w_in[i]), lb=lb_all[i], hgrn_norm_g=hgrn_norm_g[i], rwkv_mu=rwkv_mu[i],
            rwkv_w0=rwkv_w0[i], rwkv_w2=rwkv_w2[i], rwkv_a0=rwkv_a0[i], rwkv_a2=rwkv_a2[i], rwkv_g2=rwkv_g2[i],
            rwkv_kk=rwkv_kk[i], rwkv_ka=rwkv_ka[i], rwkv_rk=rwkv_rk[i], rwkv_lnx_g=rwkv_lnx_g[i],
            rwkv_lnx_b=rwkv_lnx_b[i],
            mla_qn_g=jnp.pad(mla_qnorm_g[i], (0, 256 - Q_LORA)).reshape(1, 256), mla_w_uq=_relayout_w_uq(mla_w_uq[i]),
            mla_kvn_g=mla_kvnorm_g[i].reshape(1, KV_LORA), mla_w_k=w_k, mla_w_v=w_v, fox_bf=fox_bf[i],
            w_br=w_br[i].astype(BF16), w_mg=w_mg[i].astype(BF16), w_o=w_o[i].astype(BF16),
            ln1_g=ln1_g[i], ln1_b=ln1_b[i], router_w=router_w, router_b=router_b,
            w_g=exp_w_gate[i].astype(BF16), w_u=exp_w_up[i].astype(BF16), w_d=exp_w_down[i].astype(BF16),
            ln2_g=ln2_g[i], ln2_b=ln2_b[i], ple_w=ple_w[i].astype(BF16), ple_gate_w=ple_gate_w[i].astype(BF16),
            ln3_g=ln3_g[i], ln3_b=ln3_b[i])
        xp, st_p = _layer(xp, p_prompt[i], (None,) * 8, lw, alpha)
        new_p.append(st_p)
        st_in = (state_hgrn[i], state_rwkv[i], state_rwkv_shift[i], cache_mla_latent[i], cache_mla_krope[i],
                 cache_fox_k[i], cache_fox_v[i], cache_fox_logf[i])
        xs, st_s = _layer(xs, p_sample[i], st_in, lw, alpha)
        new_s.append(st_s)
    stack = lambda sts, j: jnp.stack([s[j] for s in sts], axis=0)
    outs_p = tuple(stack(new_p, j) for j in range(8))
    outs_s = tuple(stack(new_s, j) for j in range(8))
    return (xp, xs) + outs_p + outs_s
```

```python
import functools
import math

import jax
import jax.numpy as jnp
from jax import lax
from jax.experimental import pallas as pl
from jax.experimental.pallas import tpu as pltpu

F32 = jnp.float32
BF16 = jnp.bfloat16

D_MODEL = 1024
N_HEADS = 4
HEAD_DIM = 64
BRANCH_W = N_HEADS * HEAD_DIM
CHUNK = 64
W_LORA, A_LORA, G_LORA = 32, 32, 64
NOPE, ROPE, Q_LORA, KV_LORA = 64, 32, 192, 128
ROPE_THETA = 10000.0
N_EXPERTS, N_GROUPS, EXPERT_FF = 16, 4, 256
GROUP_SIZE = N_EXPERTS // N_GROUPS
PLE_DIM = 256
A_COLS = 4 * BRANCH_W
B_COLS = 3 * BRANCH_W + W_LORA + A_LORA + G_LORA
DEPTH_ALPHA_POW = 0.25
LN_EPS = 1e-5
RMS_EPS = 1e-6
RWKV_GN_EPS = 64e-5
MASK_VALUE = -1e30
LOG2E = math.log2(math.e)

LANES = 128
SUBLANES = 8
VMEM_LIMIT_BYTES = 56 * 1024 * 1024

NN = ((1,), (0,))
NT = ((1,), (1,))
TN = ((0,), (0,))


def _dg(a, b, dims=NN):
    return lax.dot_general(a, b, (dims, ((), ())), preferred_element_type=F32)


def _bdot(a, b, dims=NN):
    return _dg(a.astype(BF16), b.astype(BF16), dims)


def _split(x, pieces):
    out = []
    r = x
    for i in range(pieces):
        p = r.astype(BF16)
        out.append(p)
        if i + 1 < pieces:
            r = r - p.astype(F32)
    return out


def _dot3(a, b, dims=NN):
    ah, al = _split(a, 2)
    bh, bl = _split(b, 2)
    return _dg(ah, bh, dims) + (_dg(ah, bl, dims) + _dg(al, bh, dims))


def _dotc(a, c, dims=NN, pieces=3):
    ps = _split(a, pieces)
    acc = _dg(ps[0], c, dims)
    for p in ps[1:]:
        acc = acc + _dg(p, c, dims)
    return acc


def _tree_sum(xs):
    while len(xs) > 1:
        xs = [xs[i] + xs[i + 1] for i in range(0, len(xs) - 1, 2)] + ([xs[-1]] if len(xs) % 2 else [])
    return xs[0]


def _iota(shape, dim):
    return lax.broadcasted_iota(jnp.int32, shape, dim)


def _tri_incl(n, dtype=BF16):
    return (_iota((n, n), 0) >= _iota((n, n), 1)).astype(dtype)


def _head_indicator():
    r = _iota((BRANCH_W, BRANCH_W), 0) // HEAD_DIM
    c = _iota((BRANCH_W, BRANCH_W), 1) // HEAD_DIM
    return (r == c).astype(BF16)


def _layer_norm(x, g, b):
    mu = jnp.mean(x, axis=-1, keepdims=True)
    xc = x - mu
    var = jnp.mean(xc * xc, axis=-1, keepdims=True)
    return xc * lax.rsqrt(var + LN_EPS) * g + b


def _softplus(x):
    return jnp.maximum(x, 0.0) + jnp.log(1.0 + jnp.exp(-jnp.abs(x)))


def _silu(x):
    return x * jax.nn.sigmoid(x)


def _cparams(*sem):
    return pltpu.CompilerParams(dimension_semantics=sem, vmem_limit_bytes=VMEM_LIMIT_BYTES)


def _resident(shape):
    nd = len(shape)
    return pl.BlockSpec(shape, lambda *_: (0,) * nd)


TOKEN_TILE = 512
MOE_TOKEN_TILE = 1024
MOE_BLOCK = 256
HGRN_SUB_BLOCK = 16
ATTN_Q_TILE = 512
ATTN_KV_TILE = 512


def _token_tile(t, largest=TOKEN_TILE):
    tm = largest
    while tm >= SUBLANES:
        if t % tm == 0:
            return tm
        tm //= 2
    raise ValueError(f"token count {t} not a multiple of 8")


def _ln_kernel(x_ref, g_ref, b_ref, o_ref):
    o_ref[...] = _layer_norm(x_ref[...], g_ref[...], b_ref[...])


def _ln_call(x, g, b):
    t, d = x.shape
    tm = _token_tile(t)
    return pl.pallas_call(
        _ln_kernel,
        grid=(t // tm,),
        in_specs=[pl.BlockSpec((tm, d), lambda i: (i, 0)), _resident((1, d)), _resident((1, d))],
        out_specs=pl.BlockSpec((tm, d), lambda i: (i, 0)),
        out_shape=jax.ShapeDtypeStruct((t, d), F32),
        compiler_params=_cparams("parallel"),
        name="ln_in",
    )(x, g.reshape(1, d), b.reshape(1, d))


_IN_SLOTS = (("a", A_COLS, A_COLS), ("b", B_COLS, B_COLS), ("qlat", Q_LORA, 256), ("kvlat", KV_LORA, 128),
             ("kr", ROPE, 128), ("fq", BRANCH_W, BRANCH_W), ("fk", BRANCH_W, BRANCH_W), ("fv", BRANCH_W, BRANCH_W),
             ("ff", N_HEADS, 128))


def _split_w_in(w_in):
    parts, off = [], 0
    for _, width, slot in _IN_SLOTS:
        w = w_in[:, off:off + width].astype(BF16)
        if slot > width:
            w = jnp.pad(w, ((0, 0), (0, slot - width)))
        parts.append(w)
        off += width
    assert off == w_in.shape[1]
    return parts


def _in_kernel(x_ref, *refs):
    n = len(_IN_SLOTS)
    w_refs, bf_ref, o_refs = refs[:n], refs[n], refs[n + 1:]
    xb = x_ref[...].astype(BF16)
    for w_ref, o_ref, (name, _, _) in zip(w_refs, o_refs, _IN_SLOTS):
        cols = _dg(xb, w_ref[...])
        if name == "ff":
            z = cols + bf_ref[...]
            cols = jnp.where(_iota(z.shape, 1) < N_HEADS, -_softplus(-z), 0.0)
            o_refs[-1][...] = cols[:, :N_HEADS]
        o_ref[...] = cols


def _in_call(x, w_parts, fox_bf):
    t, d = x.shape
    tm = _token_tile(t)
    widths = [slot for _, _, slot in _IN_SLOTS] + [N_HEADS]
    return pl.pallas_call(
        _in_kernel,
        grid=(t // tm,),
        in_specs=[pl.BlockSpec((tm, d), lambda i: (i, 0))] + [_resident(w.shape) for w in w_parts]
                 + [_resident((1, LANES))],
        out_specs=[pl.BlockSpec((tm, width), lambda i: (i, 0)) for width in widths],
        out_shape=[jax.ShapeDtypeStruct((t, width), F32) for width in widths],
        compiler_params=_cparams("parallel"),
        name="in_proj",
    )(x, *w_parts, jnp.pad(fox_bf, (0, LANES - N_HEADS)).reshape(1, LANES))


def _hgrn_parts(c_ref, lb_ref, ng_ref, s0_ref, o_ref, s_ref, st_scr, *, chunk, group):
    def init():
        st_scr[...] = s0_ref[...]

    def finish():
        s_ref[...] = st_scr[...]

    return init, _hgrn_stages(c_ref, lb_ref, ng_ref, o_ref, st_scr, chunk, group), finish


def _hgrn_stages(c_ref, lb_ref, ng_ref, o_ref, st_scr, chunk, group):
    c = chunk
    w = BRANCH_W
    hd = HEAD_DIM
    nh = N_HEADS
    sb = min(HGRN_SUB_BLOCK, c)
    seqs = range(group)
    tri = _tri_incl(c)
    ind = _head_indicator()
    st_keep = (_iota((w, w), 0) // hd) == (_iota((w, w), 1) // hd)
    hr_keep = (_iota((nh * sb, w), 0) // sb) == (_iota((nh * sb, w), 1) // hd)
    row8 = _iota((SUBLANES, w), 0)
    lb = lb_ref[...]

    sq = []
    for bi in seqs:
        cols = c_ref[bi]
        q, fz, iv, g = cols[:, :w], cols[:, w:2 * w], cols[:, 2 * w:3 * w], cols[:, 3 * w:]
        f = lb + (1.0 - lb) * jax.nn.sigmoid(fz)
        a = _dotc(tri, jnp.log(f) * LOG2E, pieces=3)
        sq.append(dict(q=q, k=1.0 - f, iv=iv, g=g, a=a, a_last=a[c - 1:c, :], outs=[]))
    yield

    for bi, s in zip(seqs, sq):
        st = st_scr[bi]
        s["o"] = _bdot(s["q"] * jnp.exp2(s["a"]), st, NT)
        kd = s["k"] * jnp.exp2(s["a_last"] - s["a"])
        st_scr[bi] = st * jnp.exp2(s["a_last"]) + jnp.where(st_keep, _bdot(s["iv"], kd, TN), 0.0)
    yield

    for tb in range(c // sb):
        lo = tb * sb
        for s in sq:
            q, k, iv, a = s["q"], s["k"], s["iv"], s["a"]
            parts = []
            for rg in range(sb // SUBLANES):
                r0 = lo + rg * SUBLANES
                q8, a8 = q[r0:r0 + SUBLANES], a[r0:r0 + SUBLANES]
                pieces = []
                for src in range(lo, r0 + SUBLANES):
                    d = a8 - a[src:src + 1]
                    if src >= r0:
                        d = jnp.where(row8 >= src - r0, d, MASK_VALUE)
                    pieces.append(q8 * jnp.exp2(d) * k[src:src + 1])
                e = _dotc(jnp.concatenate(pieces, axis=0), ind, pieces=1)
                parts.append(_tree_sum([e[i * SUBLANES:(i + 1) * SUBLANES] * iv[lo + i:lo + i + 1]
                                        for i in range(r0 + SUBLANES - lo)]))
            o_tb = jnp.concatenate(parts, axis=0)
            if tb > 0:
                a_lo = a[lo:lo + 1]
                qh = q[lo:lo + sb] * jnp.exp2(a[lo:lo + sb] - a_lo)
                kh = k[:lo] * jnp.exp2(a_lo - a[:lo])
                qh_rows = jnp.where(hr_keep, jnp.concatenate([qh] * nh, axis=0), 0.0)
                sc = _bdot(qh_rows, kh, NT)
                ov = jnp.where(hr_keep, _bdot(sc, iv[:lo]), 0.0)
                for h in range(nh):
                    o_tb = o_tb + ov[h * sb:(h + 1) * sb]
            s["outs"].append(o_tb)
        yield

    for bi, s in zip(seqs, sq):
        o = s["o"] + jnp.concatenate(s["outs"], axis=0)
        ms = _dotc(o * o, ind, pieces=2) * (1.0 / hd)
        o = o * lax.rsqrt(ms + RMS_EPS) * ng_ref[...] * _silu(s["g"])
        o_ref[bi] = o.astype(o_ref.dtype)


def _hgrn_state_to_blockdiag(s):
    b = s.shape[0]
    out = jnp.zeros((b, N_HEADS, HEAD_DIM, N_HEADS, HEAD_DIM), F32)
    for h in range(N_HEADS):
        out = out.at[:, h, :, h, :].set(jnp.swapaxes(s[:, h], -1, -2))
    return out.reshape(b, BRANCH_W, BRANCH_W)


def _hgrn_state_from_blockdiag(st):
    hs = [st[:, h * HEAD_DIM:(h + 1) * HEAD_DIM, h * HEAD_DIM:(h + 1) * HEAD_DIM] for h in range(N_HEADS)]
    return jnp.swapaxes(jnp.stack(hs, axis=1), -1, -2)


def _rwkv_tdot(a, b, dims=NN):
    return _bdot(a, b, dims)


def _rwkv_sdot(a, b, dims=NN):
    return _dot3(a, b, dims)


def _rwkv_operands(cols, prev, mu_ref, w0_ref, lora_ref, a0_ref, kk_ref, ka_ref, rk_ref):
    shifted = jnp.where(_iota(cols.shape, 0) == 0, prev, pltpu.roll(cols, 1, 0))
    m = cols + (shifted - cols) * mu_ref[...]
    w = BRANCH_W
    r, k, v = m[:, :w], m[:, w:2 * w], m[:, 2 * w:3 * w]
    slab = m[:, 3 * w:]
    lane = _iota(slab.shape, 1)
    act = jnp.where(lane < W_LORA, jnp.tanh(slab),
                    jnp.where(lane < W_LORA + A_LORA, slab, jax.nn.sigmoid(slab)))
    lora = _bdot(act, lora_ref[...])
    w_log = -_softplus(-(w0_ref[...] + lora[:, :w])) - 0.5
    a_rate = jax.nn.sigmoid(a0_ref[...] + lora[:, w:2 * w])
    g = lora[:, 2 * w:]
    ind = _head_indicator()
    kk = k * kk_ref[...]
    norm = jnp.sqrt(_dotc(kk * kk, ind, pieces=3))
    kk = kk / jnp.maximum(norm, 1e-12)
    kh = k * (1.0 + (a_rate - 1.0) * ka_ref[...])
    return dict(r=r, lw=-jnp.exp(w_log), k=kh, v=v, a=-kk, b=kk * a_rate, g=g,
                bonus=_dotc(r * kh * rk_ref[...], ind, pieces=3) * v)


def _rwkv_parts(c_ref, prow_ref, mu_ref, w0_ref, lora_ref, a0_ref, kk_ref, ka_ref, rk_ref, lg_ref, lb_ref,
                s0_ref, o_ref, s_ref, st_scr, prev_scr, *, chunk, group):
    hd = HEAD_DIM
    blocks = [(bi, slice(h * hd, (h + 1) * hd)) for bi in range(group) for h in range(N_HEADS)]

    def init():
        st_scr[...] = jnp.zeros(st_scr.shape, F32)
        prev_scr[...] = prow_ref[...]
        for bi, sl in blocks:
            st_scr[bi, sl, sl] = s0_ref[bi, sl.start // hd]

    def finish():
        for bi, sl in blocks:
            s_ref[bi, sl.start // hd] = st_scr[bi, sl, sl]

    stages = _rwkv_stages(c_ref, mu_ref, w0_ref, lora_ref, a0_ref, kk_ref, ka_ref, rk_ref, lg_ref, lb_ref, o_ref,
                          st_scr, prev_scr, chunk, group)
    return init, stages, finish


def _rwkv_stages(c_ref, mu_ref, w0_ref, lora_ref, a0_ref, kk_ref, ka_ref, rk_ref, lg_ref, lb_ref, o_ref, st_scr,
                 prev_scr, chunk, group):
    c = chunk
    w = BRANCH_W
    hd = HEAD_DIM
    nh = N_HEADS
    n = nh * c
    seqs = range(group)
    tri = _tri_incl(c)
    ind = _head_indicator()
    t_w = _iota((c, n), 0)
    s_w = _iota((c, n), 1) % c
    strict_w = t_w > s_w
    incl_w = t_w >= s_w
    eye_w = (t_w == s_w).astype(F32)
    bd_keep = (_iota((n, n), 0) // c) == (_iota((n, n), 1) // c)
    hr_keep = (_iota((n, w), 0) // c) == (_iota((n, w), 1) // hd)
    st_keep = (_iota((w, w), 0) // hd) == (_iota((w, w), 1) // hd)

    def bd(x):
        return jnp.where(bd_keep, jnp.concatenate([x] * nh, axis=0), 0.0)

    def head_rows(x):
        return jnp.where(hr_keep, jnp.concatenate([x] * nh, axis=0), 0.0)

    sq = []
    for bi in seqs:
        cols = c_ref[bi]
        ops = _rwkv_operands(cols, prev_scr[bi], mu_ref, w0_ref, lora_ref, a0_ref, kk_ref, ka_ref, rk_ref)
        prev_scr[bi] = cols[c - 1:c]
        r, lw, k, v, a, b = (ops[name] for name in ("r", "lw", "k", "v", "a", "b"))
        cum = _dotc(tri, lw, pieces=3)
        last = cum[c - 1:c, :]
        e_neg = jnp.exp(-cum)
        e_last = jnp.exp(last - cum)
        sq.append(dict(v=v, last=last, g=ops["g"], bonus=ops["bonus"],
                       lhs=jnp.concatenate([a * jnp.exp(cum - lw), r * jnp.exp(cum)], axis=0),
                       bt=b * e_neg, kt=k * e_neg, hat=jnp.concatenate([b * e_last, k * e_last], axis=0)))
    yield

    for q in sq:
        pb = _rwkv_tdot(q["lhs"], head_rows(q["bt"]), NT)
        pk = _rwkv_tdot(q["lhs"], head_rows(q["kt"]), NT)
        q["l_ab"] = jnp.where(strict_w, pb[:c], 0.0)
        q["a_rb"] = jnp.where(incl_w, pb[c:], 0.0)
        q["lk"] = jnp.concatenate([jnp.where(strict_w, pk[:c], 0.0), jnp.where(incl_w, pk[c:], 0.0)], axis=0)
        q["tinv"] = eye_w + q["l_ab"]
        q["p"] = q["l_ab"]
    yield

    for _ in range(int(math.log2(c)) - 1):
        for q in sq:
            q["p"] = _rwkv_tdot(q["p"], bd(q["p"]))
        yield
        for q in sq:
            q["tinv"] = q["tinv"] + _rwkv_tdot(q["tinv"], bd(q["p"]))
        yield

    for bi, q in zip(seqs, sq):
        q["st"] = st_scr[bi]
        q["sh"] = _rwkv_sdot(q["lhs"], q["st"], NT)
        q["lv"] = _rwkv_tdot(q["lk"], head_rows(q["v"]))
    yield
    for q in sq:
        q["u"] = _rwkv_tdot(q["tinv"], head_rows(q["sh"][:c] + q["lv"][:c]))
    yield
    for bi, q in zip(seqs, sq):
        upd = _rwkv_sdot(jnp.concatenate([q["u"], q["v"]], axis=0), q["hat"], TN)
        st_scr[bi] = q["st"] * jnp.exp(q["last"]) + jnp.where(st_keep, upd, 0.0)
    yield
    for q in sq:
        q["y"] = q["sh"][c:] + _rwkv_tdot(q["a_rb"], head_rows(q["u"])) + q["lv"][c:]
    yield

    for bi, q in zip(seqs, sq):
        y = q["y"]
        mu = _dotc(y, ind, pieces=3) * (1.0 / hd)
        yc = y - mu
        var = _dotc(yc * yc, ind, pieces=2) * (1.0 / hd)
        yn = yc * lax.rsqrt(var + RWKV_GN_EPS) * lg_ref[...] + lb_ref[...]
        o_ref[bi] = ((yn + q["bonus"]) * q["g"]).astype(o_ref.dtype)


def _batch_group(b):
    return 4 if b % 4 == 0 else 1


N_HGRN_IN, N_RWKV_IN = 4, 12
RWKV_STAGES_PER_HGRN_STAGE = 1


def _recurrent_kernel(*refs, chunk, group):
    h_in, refs = refs[:N_HGRN_IN], refs[N_HGRN_IN:]
    r_in, refs = refs[:N_RWKV_IN], refs[N_RWKV_IN:]
    h_o, h_s, r_o, r_s, h_st, r_st, r_prev = refs
    h_init, h_stages, h_finish = _hgrn_parts(*h_in, h_o, h_s, h_st, chunk=chunk, group=group)
    r_init, r_stages, r_finish = _rwkv_parts(*r_in, r_o, r_s, r_st, r_prev, chunk=chunk, group=group)
    j = pl.program_id(1)

    @pl.when(j == 0)
    def _():
        h_init()
        r_init()

    live = [r_stages] * RWKV_STAGES_PER_HGRN_STAGE + [h_stages]
    while live:
        for stages in list(live):
            if stages in live and next(stages, "done") == "done":
                live = [s for s in live if s is not stages]

    @pl.when(j == pl.num_programs(1) - 1)
    def _():
        h_finish()
        r_finish()


def _recurrent_call(cols_a, lb, norm_g, hgrn_st0, cols_b, prev_row, rwkv_s0, mu, w0, w2, a0, a2, g2, kk_s, ka, rk,
                    lnx_g, lnx_b):
    b, l, nb = cols_b.shape
    w = BRANCH_W
    c = CHUNK if l % CHUNK == 0 else l
    bg = _batch_group(b)
    row = lambda x: x.reshape(1, -1)
    lora_w = jnp.zeros((LANES, 3 * w), F32)
    lora_w = lora_w.at[:W_LORA, :w].set(w2).at[W_LORA:W_LORA + A_LORA, w:2 * w].set(a2)
    lora_w = lora_w.at[W_LORA + A_LORA:, 2 * w:].set(g2).astype(BF16)
    tok = lambda n: pl.BlockSpec((bg, c, n), lambda i, j: (i, j, 0))
    hst_spec = pl.BlockSpec((bg, w, w), lambda i, j: (i, 0, 0))
    rst_spec = pl.BlockSpec((bg, N_HEADS, HEAD_DIM, HEAD_DIM), lambda i, j: (i, 0, 0, 0))
    hgrn_in = [tok(A_COLS), _resident((1, w)), _resident((1, w)), hst_spec]
    rwkv_in = ([tok(nb), pl.BlockSpec((bg, 1, nb), lambda i, j: (i, 0, 0)), _resident((1, nb)), _resident((1, w)),
                _resident((LANES, 3 * w))] + [_resident((1, w))] * 6 + [rst_spec])
    assert len(hgrn_in) == N_HGRN_IN and len(rwkv_in) == N_RWKV_IN
    return pl.pallas_call(
        functools.partial(_recurrent_kernel, chunk=c, group=bg),
        grid=(b // bg, l // c),
        in_specs=hgrn_in + rwkv_in,
        out_specs=[tok(w), hst_spec, tok(w), rst_spec],
        out_shape=[jax.ShapeDtypeStruct((b, l, w), BF16), jax.ShapeDtypeStruct((b, w, w), F32),
                   jax.ShapeDtypeStruct((b, l, w), BF16),
                   jax.ShapeDtypeStruct((b, N_HEADS, HEAD_DIM, HEAD_DIM), F32)],
        scratch_shapes=[pltpu.VMEM((bg, w, w), F32), pltpu.VMEM((bg, w, w), F32), pltpu.VMEM((bg, 1, nb), F32)],
        compiler_params=_cparams("parallel", "arbitrary"),
        name="hgrn_rwkv",
    )(cols_a, row(lb), row(norm_g), hgrn_st0,
      cols_b, prev_row.reshape(b, 1, nb), row(mu), row(w0), lora_w, row(a0), row(kk_s), row(ka), row(rk),
      row(lnx_g), row(lnx_b), rwkv_s0)


def _rope_tables(p, l):
    half = ROPE // 2
    inv = 1.0 / (ROPE_THETA ** (jnp.arange(half, dtype=F32) / half))
    ang = (p + jnp.arange(l, dtype=jnp.int32)).astype(F32)[:, None] * inv[None, :]
    cos, sin = jnp.cos(ang), jnp.sin(ang)
    pad = jnp.zeros((l, LANES - ROPE), F32)
    return jnp.concatenate([cos, cos, pad], axis=1), jnp.concatenate([-sin, sin, pad], axis=1)


def _swap_halves(x, base):
    half = ROPE // 2
    n = x.shape[-1]
    lane = _iota(x.shape, x.ndim - 1) % LANES
    up = pltpu.roll(x, n - half, x.ndim - 1)
    down = pltpu.roll(x, half, x.ndim - 1)
    return jnp.where((lane >= base) & (lane < base + half), up,
                     jnp.where((lane >= base + half) & (lane < base + ROPE), down, 0.0))


def _rms_norm(x, g, width):
    ms = jnp.sum(x * x, axis=-1, keepdims=True) * (1.0 / width)
    return x * lax.rsqrt(ms + RMS_EPS) * g


def _mla_q_kernel(ql_ref, kvl_ref, kr_ref, cos_ref, sin_ref, qg_ref, wuq_ref, wuqp_ref, kvg_ref, q_ref, c_ref,
                  krn_ref):
    scale = (NOPE + ROPE) ** -0.5 * LOG2E
    qn = _rms_norm(ql_ref[0], qg_ref[...], Q_LORA).astype(BF16)
    q = _dg(qn, wuq_ref[...])
    q_partner = _dg(qn, wuqp_ref[...])
    cos, sin = cos_ref[...], sin_ref[...]
    lane = _iota(cos.shape, 1)
    cos_q = jnp.where(lane < NOPE, 1.0, pltpu.roll(cos, NOPE, 1))
    sin_q = pltpu.roll(sin, NOPE, 1)
    for h in range(N_HEADS):
        sl = slice(h * LANES, (h + 1) * LANES)
        q_ref[0, h] = ((q[:, sl] * cos_q + q_partner[:, sl] * sin_q) * scale).astype(q_ref.dtype)
    c_ref[0] = _rms_norm(kvl_ref[0], kvg_ref[...], KV_LORA)
    kr = kr_ref[0]
    krn = kr * cos + _swap_halves(kr, 0) * sin
    krn_ref[0] = krn[:, :ROPE]


def _mla_q_call(qlat, kvlat, kr, cos, sin, qn_g, w_uq_pair, kvn_g):
    b, l, _ = qlat.shape
    tm = _token_tile(l)
    tok = lambda n: pl.BlockSpec((1, tm, n), lambda i, j: (i, j, 0))
    tab = pl.BlockSpec((tm, LANES), lambda i, j: (j, 0))
    return pl.pallas_call(
        _mla_q_kernel,
        grid=(b, l // tm),
        in_specs=[tok(256), tok(KV_LORA), tok(LANES), tab, tab, _resident((1, 256)),
                  _resident((256, N_HEADS * LANES)), _resident((256, N_HEADS * LANES)), _resident((1, KV_LORA))],
        out_specs=[pl.BlockSpec((1, N_HEADS, tm, LANES), lambda i, j: (i, 0, j, 0)), tok(KV_LORA), tok(ROPE)],
        out_shape=[jax.ShapeDtypeStruct((b, N_HEADS, l, LANES), BF16), jax.ShapeDtypeStruct((b, l, KV_LORA), F32),
                   jax.ShapeDtypeStruct((b, l, ROPE), F32)],
        compiler_params=_cparams("parallel", "parallel"),
        name="mla_q",
    )(qlat, kvlat, kr, cos, sin, qn_g, *w_uq_pair, kvn_g)


def _mla_kv_kernel(c_ref, kr_ref, wk_ref, wv_ref, k_ref, v_ref):
    cb = c_ref[0].astype(BF16)
    kall = _dg(cb, wk_ref[...])
    vall = _dg(cb, wv_ref[...])
    tm = cb.shape[0]
    place = (_iota((ROPE, LANES), 0) + NOPE == _iota((ROPE, LANES), 1)).astype(BF16)
    kr_slot = _dg(kr_ref[0].astype(BF16), place)
    one_slot = (_iota((tm, LANES), 1) == HEAD_DIM).astype(F32)
    for h in range(N_HEADS):
        k_ref[0, h] = (kall[:, h * LANES:(h + 1) * LANES] + kr_slot).astype(k_ref.dtype)
        v_ref[0, h, 0] = (vall[:, h * LANES:(h + 1) * LANES] + one_slot).T.astype(v_ref.dtype)


def _kv_specs(b, t, tm):
    kspec = pl.BlockSpec((1, N_HEADS, tm, LANES), lambda i, j: (i, 0, j, 0))
    vspec = pl.BlockSpec((1, N_HEADS, 1, LANES, tm), lambda i, j: (i, 0, j, 0, 0))
    shapes = [jax.ShapeDtypeStruct((b, N_HEADS, t, LANES), BF16),
              jax.ShapeDtypeStruct((b, N_HEADS, t // tm, LANES, tm), BF16)]
    return [kspec, vspec], shapes


def _mla_kv_call(c_all, kr_all, w_k_p, w_v_p):
    b, t, _ = c_all.shape
    tm = _token_tile(t, ATTN_KV_TILE)
    tok = lambda n: pl.BlockSpec((1, tm, n), lambda i, j: (i, j, 0))
    out_specs, out_shape = _kv_specs(b, t, tm)
    return pl.pallas_call(
        _mla_kv_kernel,
        grid=(b, t // tm),
        in_specs=[tok(KV_LORA), tok(ROPE), _resident((KV_LORA, N_HEADS * LANES)),
                  _resident((KV_LORA, N_HEADS * LANES))],
        out_specs=out_specs,
        out_shape=out_shape,
        compiler_params=_cparams("parallel", "parallel"),
        name="mla_kv",
    )(c_all, kr_all, w_k_p, w_v_p)


def _head_slot(x, h):
    pair = x[:, (h // 2) * LANES:(h // 2 + 1) * LANES]
    return pair if h % 2 == 0 else pltpu.roll(pair, HEAD_DIM, 1)


FOX_KEY_BIAS = HEAD_DIM
FOX_QUERY_BIAS = HEAD_DIM + 3 * N_HEADS


def _bias_pieces(c, base):
    p0 = c.astype(BF16).astype(F32)
    r1 = c - p0
    p1 = r1.astype(BF16).astype(F32)
    p2 = r1 - p1
    return (pltpu.roll(p0, base, 1) + pltpu.roll(p1, base + N_HEADS, 1)) + pltpu.roll(p2, base + 2 * N_HEADS, 1)


def _bias_ones(shape, base, h):
    lane = _iota(shape, 1)
    hit = (lane == base + h) | (lane == base + N_HEADS + h) | (lane == base + 2 * N_HEADS + h)
    return hit.astype(F32)


def _fox_q_kernel(q_ref, c_ref, o_ref):
    scale = HEAD_DIM ** -0.5 * LOG2E
    q = q_ref[0]
    pieces = _bias_pieces(c_ref[0] * LOG2E, FOX_QUERY_BIAS)
    lane = _iota(pieces.shape, 1)
    for h in range(N_HEADS):
        bias = pieces + _bias_ones(pieces.shape, FOX_KEY_BIAS, h)
        o_ref[0, h] = jnp.where(lane < HEAD_DIM, _head_slot(q, h) * scale, bias).astype(o_ref.dtype)


def _fox_kv_kernel(k_ref, v_ref, lf_ref, ko_ref, vo_ref, c_ref, carry):
    @pl.when(pl.program_id(1) == 0)
    def _():
        carry[...] = jnp.zeros_like(carry)

    k, v, lf = k_ref[0], v_ref[0], lf_ref[0]
    tm = k.shape[0]
    c = _dotc(_tri_incl(tm), lf, pieces=3) + carry[...]
    c_ref[0] = c
    carry[...] = c[tm - 1:, :]
    pieces = _bias_pieces(-c * LOG2E, FOX_KEY_BIAS)
    lane = _iota((tm, LANES), 1)
    for h in range(N_HEADS):
        bias = pieces + _bias_ones(pieces.shape, FOX_QUERY_BIAS, h)
        ko_ref[0, h] = jnp.where(lane < HEAD_DIM, _head_slot(k, h), bias).astype(ko_ref.dtype)
        vo_ref[0, h, 0] = jnp.where(lane < HEAD_DIM, _head_slot(v, h),
                                    (lane == HEAD_DIM).astype(F32)).T.astype(vo_ref.dtype)


def _fox_q_call(q, c_new):
    b, l, w = q.shape
    tm = _token_tile(l)
    return pl.pallas_call(
        _fox_q_kernel,
        grid=(b, l // tm),
        in_specs=[pl.BlockSpec((1, tm, w), lambda i, j: (i, j, 0)),
                  pl.BlockSpec((1, tm, LANES), lambda i, j: (i, j, 0))],
        out_specs=pl.BlockSpec((1, N_HEADS, tm, LANES), lambda i, j: (i, 0, j, 0)),
        out_shape=jax.ShapeDtypeStruct((b, N_HEADS, l, LANES), BF16),
        compiler_params=_cparams("parallel", "parallel"),
        name="fox_q",
    )(q, c_new)


def _fox_kv_call(k_all, v_all, lf_all):
    b, t, w = k_all.shape
    tm = _token_tile(t, ATTN_KV_TILE)
    tok = pl.BlockSpec((1, tm, w), lambda i, j: (i, j, 0))
    wide = pl.BlockSpec((1, tm, LANES), lambda i, j: (i, j, 0))
    out_specs, out_shape = _kv_specs(b, t, tm)
    return pl.pallas_call(
        _fox_kv_kernel,
        grid=(b, t // tm),
        in_specs=[tok, tok, wide],
        out_specs=out_specs + [wide],
        out_shape=out_shape + [jax.ShapeDtypeStruct((b, t, LANES), F32)],
        scratch_shapes=[pltpu.VMEM((1, LANES), F32)],
        compiler_params=_cparams("parallel", "arbitrary"),
        name="fox_kv",
    )(k_all, v_all, lf_all)


def _flash_kernel(q_ref, k_ref, v_ref, o_ref, m_scr, acc_scr, s_scr, *, tq, tk, past, t_valid, chunk_mask):
    qi = pl.program_id(1)
    q_start = past + qi * tq
    if chunk_mask:
        vis_end = jnp.minimum(((q_start + tq + CHUNK - 1) // CHUNK) * CHUNK, t_valid)
    else:
        vis_end = q_start + tq
    n_blocks = (vis_end + tk - 1) // tk
    n_full = q_start // tk

    m_scr[...] = jnp.full(m_scr.shape, MASK_VALUE, F32)
    acc_scr[...] = jnp.zeros(acc_scr.shape, F32)

    def logits(h, kb):
        ks = pl.multiple_of(kb * tk, tk)
        return _dg(k_ref[0, h, pl.ds(ks, tk), :], q_ref[0, h], NT)

    s_scr[...] = logits(0, 0)

    def block(kb, masked, s_next):
        if masked:
            kpos = kb * tk + _iota((tk, tq), 0)
            qpos = q_start + _iota((tk, tq), 1)
            if chunk_mask:
                keep = ((kpos // CHUNK) <= (qpos // CHUNK)) & (kpos < t_valid)
            else:
                keep = kpos <= qpos
        for h in range(N_HEADS):
            s = jnp.where(keep, s_next, MASK_VALUE) if masked else s_next
            if h + 1 < N_HEADS:
                s_next = logits(h + 1, kb)
            else:
                s_next = logits(0, jnp.minimum(kb + 1, n_blocks - 1))
            m_old = m_scr[h]
            m_new = jnp.maximum(m_old, jnp.max(s, axis=0, keepdims=True))
            p = jnp.exp2(s - m_new[:1])
            pv = _dg(v_ref[0, h, kb], p.astype(BF16))
            acc_scr[h] = acc_scr[h] * jnp.exp2(m_old - m_new)[:1] + pv
            m_scr[h] = m_new
        return s_next

    def full_body(kb, carry):
        s_scr[...] = block(kb, False, s_scr[...])
        return carry

    def masked_body(kb, carry):
        s_scr[...] = block(kb, True, s_scr[...])
        return carry

    lax.fori_loop(0, n_full, full_body, 0)
    lax.fori_loop(n_full, n_blocks, masked_body, 0)
    outs = [acc_scr[h, :HEAD_DIM, :] / acc_scr[h, HEAD_DIM:HEAD_DIM + 1, :] for h in range(N_HEADS)]
    for pair in range(N_HEADS // 2):
        o = jnp.concatenate(outs[2 * pair:2 * pair + 2], axis=0).T
        o_ref[0, :, pair * LANES:(pair + 1) * LANES] = o.astype(o_ref.dtype)


def _flash_call(q, k, v, *, past, t_valid, chunk_mask, name):
    b, nh, l, _ = q.shape
    t = k.shape[2]
    tk = v.shape[-1]
    lq = max(l, LANES)
    if lq != l:
        q = jnp.pad(q, ((0, 0), (0, 0), (0, lq - l), (0, 0)))
    tq = _token_tile(lq, ATTN_Q_TILE)
    out = pl.pallas_call(
        functools.partial(_flash_kernel, tq=tq, tk=tk, past=past, t_valid=t_valid, chunk_mask=chunk_mask),
        grid=(b, lq // tq),
        in_specs=[pl.BlockSpec((1, nh, tq, LANES), lambda i, j: (i, 0, j, 0)),
                  pl.BlockSpec((1, nh, t, LANES), lambda i, j: (i, 0, 0, 0)),
                  pl.BlockSpec((1, nh, t // tk, LANES, tk), lambda i, j: (i, 0, 0, 0, 0))],
        out_specs=pl.BlockSpec((1, tq, BRANCH_W), lambda i, j: (i, j, 0)),
        out_shape=jax.ShapeDtypeStruct((b, lq, BRANCH_W), BF16),
        scratch_shapes=[pltpu.VMEM((nh, SUBLANES, tq), F32), pltpu.VMEM((nh, LANES, tq), F32),
                        pltpu.VMEM((tk, tq), F32)],
        compiler_params=_cparams("parallel", "arbitrary"),
        name=name,
    )(q, k, v)
    return out[:, :l]


def _mix_kernel(x_ref, oa_ref, ob_ref, oc_ref, od_ref, wmg_ref, wbr_ref, wo_ref, g_ref, b_ref, o_ref, *, alpha):
    x = x_ref[...]
    xb = x.astype(BF16)
    d = D_MODEL
    acc = None
    for n, br_ref in enumerate((oa_ref, ob_ref, oc_ref, od_ref)):
        gate = jax.nn.sigmoid(_dg(xb, wmg_ref[:, n * d:(n + 1) * d]))
        term = gate * _dg(br_ref[...], wbr_ref[n])
        acc = term if acc is None else acc + term
    mix = _dg(acc.astype(BF16), wo_ref[...])
    o_ref[...] = _layer_norm(alpha * x + mix, g_ref[...], b_ref[...])


def _mix_call(x, oa, ob, oc, od, w_mg, w_br, w_o, g, b, alpha):
    t, d = x.shape
    tm = _token_tile(t)
    tok = lambda n: pl.BlockSpec((tm, n), lambda i: (i, 0))
    return pl.pallas_call(
        functools.partial(_mix_kernel, alpha=alpha),
        grid=(t // tm,),
        in_specs=[tok(d)] + [tok(BRANCH_W)] * 4 + [_resident(w_mg.shape), _resident(w_br.shape),
                                                   _resident(w_o.shape), _resident((1, d)), _resident((1, d))],
        out_specs=tok(d),
        out_shape=jax.ShapeDtypeStruct((t, d), F32),
        compiler_params=_cparams("parallel"),
        name="mix_ln1",
    )(x, oa, ob, oc, od, w_mg, w_br, w_o, g.reshape(1, d), b.reshape(1, d))


def _route(x, rw_ref, rb_ref):
    scores = jax.nn.sigmoid(_dot3(rw_ref[...], x, NT)[:N_EXPERTS])
    biased = scores + rb_ref[...][:N_EXPERTS]
    col = [biased[e:e + 1] for e in range(N_EXPERTS)]
    gs = []
    for g in range(N_GROUPS):
        v = col[g * GROUP_SIZE:(g + 1) * GROUP_SIZE]
        best = None
        for i in range(GROUP_SIZE):
            for j in range(i + 1, GROUP_SIZE):
                s = v[i] + v[j]
                best = s if best is None else jnp.maximum(best, s)
        gs.append(best)
    gates, sels = [], []
    for g in range(N_GROUPS):
        sel = None
        for o in range(N_GROUPS):
            if o == g:
                continue
            cond = (gs[g] > gs[o]) if o < g else (gs[g] >= gs[o])
            sel = cond if sel is None else sel & cond
        sels.append(sel)
        v = col[g * GROUP_SIZE:(g + 1) * GROUP_SIZE]
        for i in range(GROUP_SIZE):
            rank = None
            for j in range(GROUP_SIZE):
                if j == i:
                    continue
                ahead = (v[j] >= v[i]) if j < i else (v[j] > v[i])
                ahead = ahead.astype(F32)
                rank = ahead if rank is None else rank + ahead
            e = g * GROUP_SIZE + i
            gates.append(jnp.where(sel & (rank < 2.0), scores[e:e + 1], 0.0))
    total = gates[0]
    for gt in gates[1:]:
        total = total + gt
    return [gt / total for gt in gates], sels


MOE_POS_LANE = N_EXPERTS
MOE_ROW_FORM = 24
MOE_META_GROUP = 32
MOE_META_COUNT = 64


def _moe_route_kernel(x_ref, rw_ref, rb_ref, gp_ref, posr_ref, meta_ref, *, rows):
    tm = x_ref.shape[0]
    gates, sels = _route(x_ref[...], rw_ref, rb_ref)
    gsel = jnp.concatenate([s.astype(F32) for s in sels] + [jnp.zeros((SUBLANES - N_GROUPS, tm), F32)], axis=0)
    before = (_iota((tm, tm), 0) < _iota((tm, tm), 1)).astype(BF16)
    rank = _dg(gsel.astype(BF16), before)
    count = jnp.sum(gsel, axis=1, keepdims=True)
    offs = [jnp.zeros((1, 1), F32)]
    for g in range(1, N_GROUPS):
        offs.append(offs[-1] + count[g - 1:g])
    pos = gsel[0:1] * rank[0:1]
    for g in range(1, N_GROUPS):
        pos = pos + gsel[g:g + 1] * (offs[g] + rank[g:g + 1])
    gt = jnp.concatenate(gates + [pos, jnp.zeros((LANES - N_EXPERTS - 1, tm), F32)], axis=0)
    posr_ref[0] = gt[:MOE_ROW_FORM]
    gp_ref[...] = gt.T

    lane = _iota((1, LANES), 1)
    lane_g = lane - MOE_META_GROUP
    shift = int(math.log2(rows))
    meta = jnp.zeros((1, LANES), jnp.int32)
    start = jnp.zeros((1, 1), jnp.int32)
    for g in range(N_GROUPS):
        n_g = count[g:g + 1].astype(jnp.int32)
        off_g = offs[g].astype(jnp.int32)
        first = lax.shift_right_arithmetic(off_g, shift)
        last = lax.shift_right_arithmetic(off_g + n_g - 1, shift)
        k_g = jnp.where(n_g > 0, last - first + 1, 0)
        meta = jnp.where((lane >= start) & (lane < start + k_g), first + (lane - start), meta)
        meta = jnp.where((lane_g >= start) & (lane_g < start + k_g), g, meta)
        start = start + k_g
    meta_ref[0] = jnp.where(lane == MOE_META_COUNT, start, meta)


def _moe_sparse_kernel(meta_ref, x_ref, gp_ref, posr_ref, wg_ref, wu_ref, wd_ref, g_ref, b_ref, o_ref,
                       xb_scr, acc_scr, xs_scr, gs_scr, y_scr, back_scr, *, alpha, rows):
    i = pl.program_id(0)
    w = pl.program_id(1)
    tm = x_ref.shape[0]
    blk = meta_ref[i, w]
    grp = meta_ref[i, MOE_META_GROUP + w]
    n_items = meta_ref[i, MOE_META_COUNT]
    valid = w < n_items
    first_of_blk = valid & ((w == 0) | (meta_ref[i, jnp.maximum(w - 1, 0)] != blk))
    last_of_blk = valid & ((w + 1 >= n_items) | (meta_ref[i, w + 1] != blk))
    base = (blk * rows).astype(F32)

    @pl.when(w == 0)
    def _():
        xb_scr[...] = x_ref[...].astype(BF16)
        acc_scr[...] = jnp.zeros(acc_scr.shape, F32)

    @pl.when(first_of_blk)
    def _():
        rowform = posr_ref[0]
        pos_row = rowform[MOE_POS_LANE:MOE_POS_LANE + 1]
        perm = (pos_row == base + _iota((rows, tm), 0).astype(F32)).astype(BF16)
        back = (gp_ref[:, MOE_POS_LANE:MOE_POS_LANE + 1]
                == base + _iota((tm, rows), 1).astype(F32)).astype(BF16)
        back_scr[...] = back
        xs_scr[...] = _dg(perm, xb_scr[...]).astype(BF16)
        gst = _dotc(rowform[:N_EXPERTS], back, pieces=3)
        gst = jnp.concatenate([gst, jnp.zeros((LANES - N_EXPERTS, rows), F32)], axis=0)
        gs_scr[...] = gst.T
        y_scr[...] = jnp.zeros(y_scr.shape, F32)

    @pl.when(valid)
    def _():
        xs = xs_scr[...]
        gs = gs_scr[...]
        lane = _iota(gs.shape, 1)
        y = y_scr[...]
        for e in range(GROUP_SIZE):
            idx = grp * GROUP_SIZE + e
            gate = jnp.sum(jnp.where(lane == idx, gs, 0.0), axis=-1, keepdims=True)
            h = _silu(_dg(xs, wg_ref[idx])) * _dg(xs, wu_ref[idx]) * gate
            y = y + _dg(h.astype(BF16), wd_ref[idx])
        y_scr[...] = y

    @pl.when(last_of_blk)
    def _():
        acc_scr[...] += _dg(back_scr[...], y_scr[...].astype(BF16))

    @pl.when(w == pl.num_programs(1) - 1)
    def _():
        o_ref[...] = _layer_norm(alpha * x_ref[...] + acc_scr[...], g_ref[...], b_ref[...])


def _moe_call(x, router_w, router_b, w_g, w_u, w_d, g, b, alpha):
    t, d = x.shape
    tm = _token_tile(t, MOE_TOKEN_TILE)
    nt = t // tm
    rows = min(MOE_BLOCK, tm)
    tok = pl.BlockSpec((tm, d), lambda i: (i, 0))
    gp, posr, meta = pl.pallas_call(
        functools.partial(_moe_route_kernel, rows=rows),
        grid=(nt,),
        in_specs=[tok, _resident((LANES, d)), _resident((LANES, 1))],
        out_specs=[pl.BlockSpec((tm, LANES), lambda i: (i, 0)), pl.BlockSpec((1, MOE_ROW_FORM, tm), lambda i: (i, 0, 0)),
                   pl.BlockSpec((1, 1, LANES), lambda i: (i, 0, 0))],
        out_shape=[jax.ShapeDtypeStruct((t, LANES), F32), jax.ShapeDtypeStruct((nt, MOE_ROW_FORM, tm), F32),
                   jax.ShapeDtypeStruct((nt, 1, LANES), jnp.int32)],
        compiler_params=_cparams("parallel"),
        name="moe_route",
    )(x, jnp.pad(router_w.T, ((0, LANES - N_EXPERTS), (0, 0))),
      jnp.pad(router_b, (0, LANES - N_EXPERTS)).reshape(LANES, 1))

    n_items = tm // rows + N_GROUPS - 1
    once = lambda shape: pl.BlockSpec(shape, lambda i, w, m: (0,) * len(shape), pipeline_mode=pl.Buffered(1))
    tok2 = lambda n: pl.BlockSpec((tm, n), lambda i, w, m: (i, 0))
    return pl.pallas_call(
        functools.partial(_moe_sparse_kernel, alpha=alpha, rows=rows),
        grid_spec=pltpu.PrefetchScalarGridSpec(
            num_scalar_prefetch=1,
            grid=(nt, n_items),
            in_specs=[tok2(d), tok2(LANES), pl.BlockSpec((1, MOE_ROW_FORM, tm), lambda i, w, m: (i, 0, 0)),
                      once(w_g.shape), once(w_u.shape), once(w_d.shape), once((1, d)), once((1, d))],
            out_specs=tok2(d),
            scratch_shapes=[pltpu.VMEM((tm, d), BF16), pltpu.VMEM((tm, d), F32), pltpu.VMEM((rows, d), BF16),
                            pltpu.VMEM((rows, LANES), F32), pltpu.VMEM((rows, d), F32),
                            pltpu.VMEM((tm, rows), BF16)]),
        out_shape=jax.ShapeDtypeStruct((t, d), F32),
        compiler_params=_cparams("parallel", "arbitrary"),
        name="moe_ln2",
    )(meta.reshape(nt, LANES), x, gp, posr, w_g, w_u, w_d, g.reshape(1, d), b.reshape(1, d))


def _ple_kernel(x_ref, p_ref, pw_ref, gw_ref, g_ref, b_ref, o_ref, *, alpha):
    x = x_ref[...]
    ple = _bdot(p_ref[...], pw_ref[...]) * jax.nn.sigmoid(_bdot(x, gw_ref[...]))
    o_ref[...] = _layer_norm(alpha * x + ple, g_ref[...], b_ref[...])


def _ple_call(x, p, ple_w, gate_w, g, b, alpha):
    t, d = x.shape
    tm = _token_tile(t)
    tok = lambda n: pl.BlockSpec((tm, n), lambda i: (i, 0))
    return pl.pallas_call(
        functools.partial(_ple_kernel, alpha=alpha),
        grid=(t // tm,),
        in_specs=[tok(d), tok(PLE_DIM), _resident(ple_w.shape), _resident(gate_w.shape), _resident((1, d)),
                  _resident((1, d))],
        out_specs=tok(d),
        out_shape=jax.ShapeDtypeStruct((t, d), F32),
        compiler_params=_cparams("parallel"),
        name="ple_ln3",
    )(x, p, ple_w, gate_w, g.reshape(1, d), b.reshape(1, d))


def _lb_kernel(x_ref, o_ref):
    x = x_ref[...]
    depth = x.shape[0]
    m = jnp.max(x, axis=0, keepdims=True)
    e = jnp.exp(x - m)
    pr = e / jnp.sum(e, axis=0, keepdims=True)
    run = jnp.zeros_like(pr[0:1])
    for i in range(depth):
        o_ref[i:i + 1, :] = run
        run = run + pr[i:i + 1]


def _lb_call(logits):
    return pl.pallas_call(
        _lb_kernel,
        out_shape=jax.ShapeDtypeStruct(logits.shape, F32),
        name="hgrn_lb",
    )(logits)


def _pad_rows(x, t):
    pad = t - x.shape[1]
    if pad == 0:
        return x
    return jnp.pad(x, ((0, 0), (0, pad)) + ((0, 0),) * (x.ndim - 2))


def _layer(x, p, st, lw, alpha):
    b, l, d = x.shape
    t = b * l
    hgrn_s, rwkv_s, rwkv_prev, lat_past, kr_past, fk_past, fv_past, flf_past = st
    past = 0 if lat_past is None else lat_past.shape[1]
    t_valid = past + l
    t_pad = -(-t_valid // LANES) * LANES if past else t_valid

    cols = _in_call(x.reshape(t, d), lw["w_in"], lw["fox_bf"])
    ca, cb, qlat, kvlat, kr, fq, fk, fv, lf_wide, lf_new = [c.reshape(b, l, -1) for c in cols]

    st0 = jnp.zeros((b, BRANCH_W, BRANCH_W), F32) if hgrn_s is None else _hgrn_state_to_blockdiag(hgrn_s)
    prev_row = jnp.zeros((b, B_COLS), F32) if rwkv_prev is None else rwkv_prev
    s0 = jnp.zeros((b, N_HEADS, HEAD_DIM, HEAD_DIM), F32) if rwkv_s is None else rwkv_s
    o_a, hgrn_bd, o_b, rwkv_new = _recurrent_call(
        ca, lw["lb"], lw["hgrn_norm_g"], st0, cb, prev_row, s0, lw["rwkv_mu"], lw["rwkv_w0"], lw["rwkv_w2"],
        lw["rwkv_a0"], lw["rwkv_a2"], lw["rwkv_g2"], lw["rwkv_kk"], lw["rwkv_ka"], lw["rwkv_rk"],
        lw["rwkv_lnx_g"], lw["rwkv_lnx_b"])
    hgrn_new = _hgrn_state_from_blockdiag(hgrn_bd)
    shift_new = cb[:, l - 1]

    cos, sin = _rope_tables(past, l)
    q_c, lat_new, kr_new = _mla_q_call(qlat, kvlat, kr, cos, sin, lw["mla_qn_g"], lw["mla_w_uq"], lw["mla_kvn_g"])
    if past:
        c_all = _pad_rows(jnp.concatenate([lat_past, lat_new], axis=1), t_pad)
        kr_all = _pad_rows(jnp.concatenate([kr_past, kr_new], axis=1), t_pad)
    else:
        c_all, kr_all = lat_new, kr_new
    k_c, v_c = _mla_kv_call(c_all, kr_all, lw["mla_w_k"], lw["mla_w_v"])
    o_c = _flash_call(q_c, k_c, v_c, past=past, t_valid=t_valid, chunk_mask=True, name="mla_attn")

    if past:
        flf_wide = jnp.pad(flf_past, ((0, 0), (0, 0), (0, LANES - N_HEADS)))
        lf_all = _pad_rows(jnp.concatenate([flf_wide, lf_wide], axis=1), t_pad)
        k_all = _pad_rows(jnp.concatenate([fk_past.reshape(b, past, BRANCH_W), fk], axis=1), t_pad)
        v_all = _pad_rows(jnp.concatenate([fv_past.reshape(b, past, BRANCH_W), fv], axis=1), t_pad)
    else:
        lf_all, k_all, v_all = lf_wide, fk, fv
    k_d, v_d, c_all_f = _fox_kv_call(k_all, v_all, lf_all)
    q_d = _fox_q_call(fq, c_all_f[:, past:past + l])
    o_d = _flash_call(q_d, k_d, v_d, past=past, t_valid=t_valid, chunk_mask=False, name="fox_attn")

    flat = lambda o: o.reshape(t, BRANCH_W)
    x1 = _mix_call(x.reshape(t, d), flat(o_a), flat(o_b), flat(o_c), flat(o_d), lw["w_mg"], lw["w_br"], lw["w_o"],
                   lw["ln1_g"], lw["ln1_b"], alpha)
    x2 = _moe_call(x1, lw["router_w"], lw["router_b"], lw["w_g"], lw["w_u"], lw["w_d"], lw["ln2_g"], lw["ln2_b"],
                   alpha)
    x3 = _ple_call(x2, p.reshape(t, PLE_DIM), lw["ple_w"], lw["ple_gate_w"], lw["ln3_g"], lw["ln3_b"], alpha)
    new = (hgrn_new, rwkv_new, shift_new, lat_new, kr_new, fk.reshape(b, l, N_HEADS, HEAD_DIM),
           fv.reshape(b, l, N_HEADS, HEAD_DIM), lf_new)
    return x3.reshape(b, l, d), new


def _relayout_w_uq(w_uq):
    half = ROPE // 2
    w = w_uq.reshape(Q_LORA, N_HEADS, NOPE + ROPE)
    partner = jnp.concatenate([jnp.zeros_like(w[..., :NOPE]), w[..., NOPE + half:], w[..., NOPE:NOPE + half]], axis=-1)
    pad = lambda a: jnp.pad(a, ((0, 256 - Q_LORA), (0, 0), (0, LANES - NOPE - ROPE))).reshape(256, N_HEADS * LANES)
    return pad(w).astype(BF16), pad(partner).astype(BF16)


def _relayout_w_ukv(w_ukv):
    w = w_ukv.reshape(KV_LORA, N_HEADS, NOPE + HEAD_DIM)
    pad = lambda x: jnp.pad(x, ((0, 0), (0, 0), (0, LANES - x.shape[-1]))).reshape(KV_LORA, N_HEADS * LANES)
    return pad(w[..., :NOPE]).astype(BF16), pad(w[..., NOPE:]).astype(BF16)


def kernel(x_prompt, x_sample, state_hgrn, state_rwkv, state_rwkv_shift, cache_mla_latent, cache_mla_krope, cache_fox_k, cache_fox_v, cache_fox_logf, p_prompt, p_sample, ln_in_g, ln_in_b, w_in, hgrn_lb_logits, hgrn_norm_g, rwkv_mu, rwkv_w0, rwkv_w2, rwkv_a0, rwkv_a2, rwkv_g2, rwkv_kk, rwkv_ka, rwkv_rk, rwkv_lnx_g, rwkv_lnx_b, mla_qnorm_g, mla_w_uq, mla_kvnorm_g, mla_w_ukv, fox_bf, w_br, w_mg, w_o, ln1_g, ln1_b, router_w, router_b, exp_w_gate, exp_w_up, exp_w_down, ln2_g, ln2_b, ple_w, ple_gate_w, ln3_g, ln3_b):
    depth = w_in.shape[0]
    alpha = (2 * depth) ** DEPTH_ALPHA_POW
    d = x_prompt.shape[-1]
    lb_all = _lb_call(hgrn_lb_logits)

    def ln_in(x):
        return _ln_call(x.reshape(-1, d), ln_in_g, ln_in_b).reshape(x.shape)

    xp, xs = ln_in(x_prompt), ln_in(x_sample)
    new_p, new_s = [], []
    for i in range(depth):
        w_k, w_v = _relayout_w_ukv(mla_w_ukv[i])
        lw = dict(
            w_in=_split_w_in(w_in[i]), lb=lb_all[i], hgrn_norm_g=hgrn_norm_g[i], rwkv_mu=rwkv_mu[i],
            rwkv_w0=rwkv_w0[i], rwkv_w2=rwkv_w2[i], rwkv_a0=rwkv_a0[i], rwkv_a2=rwkv_a2[i], rwkv_g2=rwkv_g2[i],
            rwkv_kk=rwkv_kk[i], rwkv_ka=rwkv_ka[i], rwkv_rk=rwkv_rk[i], rwkv_lnx_g=rwkv_lnx_g[i],
            rwkv_lnx_b=rwkv_lnx_b[i],
            mla_qn_g=jnp.pad(mla_qnorm_g[i], (0, 256 - Q_LORA)).reshape(1, 256), mla_w_uq=_relayout_w_uq(mla_w_uq[i]),
            mla_kvn_g=mla_kvnorm_g[i].reshape(1, KV_LORA), mla_w_k=w_k, mla_w_v=w_v, fox_bf=fox_bf[i],
            w_br=w_br[i].astype(BF16), w_mg=w_mg[i].astype(BF16), w_o=w_o[i].astype(BF16),
            ln1_g=ln1_g[i], ln1_b=ln1_b[i], router_w=router_w, router_b=router_b,
            w_g=exp_w_gate[i].astype(BF16), w_u=exp_w_up[i].astype(BF16), w_d=exp_w_down[i].astype(BF16),
            ln2_g=ln2_g[i], ln2_b=ln2_b[i], ple_w=ple_w[i].astype(BF16), ple_gate_w=ple_gate_w[i].astype(BF16),
            ln3_g=ln3_g[i], ln3_b=ln3_b[i])
        xp, st_p = _layer(xp, p_prompt[i], (None,) * 8, lw, alpha)
        new_p.append(st_p)
        st_in = (state_hgrn[i], state_rwkv[i], state_rwkv_shift[i], cache_mla_latent[i], cache_mla_krope[i],
                 cache_fox_k[i], cache_fox_v[i], cache_fox_logf[i])
        xs, st_s = _layer(xs, p_sample[i], st_in, lw, alpha)
        new_s.append(st_s)
    stack = lambda sts, j: jnp.stack([s[j] for s in sts], axis=0)
    outs_p = tuple(stack(new_p, j) for j in range(8))
    outs_s = tuple(stack(new_s, j) for j in range(8))
    return (xp, xs) + outs_p + outs_s
```
